```python
import math
import jax, jax.numpy as jnp
from jax import lax
import numpy as np

D_MODEL = 1024
BATCH = 8
SEQ = 4096
DEPTH = 4
DEC_BATCH = 32
DEC_SEQ = 64
PAST_LEN = 4096

CHUNK = 64
N_A_LAYERS = DEPTH // 2
N_B_LAYERS = DEPTH - N_A_LAYERS
M_HEADS = 8
M_DV = D_MODEL // M_HEADS
M_DK = M_DV // 2
M_IN = 2 * M_HEADS * M_DK + 2 * D_MODEL + 2 * M_HEADS
A_HEADS = 8
A_DH = D_MODEL // (2 * A_HEADS)
N_BUCKETS = 32
MAX_DISTANCE = 128
Q_BLOCK = 128
D_FF = -(-8 * D_MODEL // (3 * 256)) * 256
EPS = 1e-6
NEG_BIG = -1e30

kernel_name = 'yoco_mlstm_diffattn_stream_step'


def _rms(x, g):
    xf = x.astype(jnp.float32)
    y = xf * lax.rsqrt(jnp.mean(xf * xf, axis=-1, keepdims=True) + EPS)
    return (y * g).astype(x.dtype)


def _modulate(x, g, shift, scale):
    return _rms(x, g) * (1 + scale[:, None, :]) + shift[:, None, :]


def _swiglu(x, w_gate, w_up, w_down):
    return (jax.nn.silu(x @ w_gate) * (x @ w_up)) @ w_down


def _rel_bucket(rel):
    nb = N_BUCKETS // 2
    max_exact = nb // 2
    ret = jnp.where(rel > 0, nb, 0)
    n = jnp.abs(rel)
    large = max_exact + (jnp.log(jnp.maximum(n, 1).astype(jnp.float32) / max_exact)
                         / math.log(MAX_DISTANCE / max_exact) * (nb - max_exact)).astype(jnp.int32)
    large = jnp.minimum(large, nb - 1)
    return ret + jnp.where(n < max_exact, n, large)


def _mlstm_chunk_step(carry, inp):
    C, n, m = carry
    q, k, v, ig, lf = inp
    L = q.shape[2]
    tri = jnp.tril(jnp.ones((L, L), dtype=bool))
    b = jnp.cumsum(lf, axis=-1)
    dmat = jnp.where(tri, b[..., :, None] - b[..., None, :] + ig[..., None, :], -jnp.inf)
    inter = b + m[..., None]
    m_t = jnp.maximum(inter, jnp.max(dmat, axis=-1))
    w_intra = jnp.exp(dmat - m_t[..., None])
    w_inter = jnp.exp(inter - m_t)
    s = jnp.einsum('bhtd,bhsd->bhts', q, k) * w_intra
    num = w_inter[..., None] * jnp.einsum('bhtd,bhde->bhte', q, C) + jnp.einsum('bhts,bhse->bhte', s, v)
    den = w_inter * jnp.einsum('bhtd,bhd->bht', q, n) + jnp.sum(s, axis=-1)
    h = num / jnp.maximum(jnp.abs(den), jnp.exp(-m_t))[..., None]
    b_end = b[..., -1]
    g = b_end[..., None] - b + ig
    m_new = jnp.maximum(b_end + m, jnp.max(g, axis=-1))
    w_old = jnp.exp(b_end + m - m_new)
    w_new = jnp.exp(g - m_new[..., None])
    C_new = w_old[..., None, None] * C + jnp.einsum('bhs,bhsd,bhse->bhde', w_new, k, v)
    n_new = w_old[..., None] * n + jnp.einsum('bhs,bhsd->bhd', w_new, k)
    return (C_new, n_new, m_new), h


def _mlstm_mixer(xm, w_in, b_gates, head_g, w_out, carry, chunk_len):
    B, T, _ = xm.shape
    nc = T // chunk_len
    qk = M_HEADS * M_DK
    proj = xm @ w_in
    q, k, v, o, gates = jnp.split(proj, [qk, 2 * qk, 2 * qk + D_MODEL, 2 * qk + 2 * D_MODEL], axis=-1)
    gates = gates.astype(jnp.float32) + b_gates.astype(jnp.float32)
    ig = gates[..., :M_HEADS]
    lf = jax.nn.log_sigmoid(gates[..., M_HEADS:])

    def to_chunks(a, d):
        return a.reshape(B, nc, chunk_len, M_HEADS, d).transpose(1, 0, 3, 2, 4)

    def gate_chunks(a):
        return a.reshape(B, nc, chunk_len, M_HEADS).transpose(1, 0, 3, 2)

    qc = to_chunks(q, M_DK)
    kc = to_chunks(k, M_DK) * (M_DK ** -0.5)
    vc = to_chunks(v, M_DV)
    carry = tuple(a.astype(jnp.float32) for a in carry)
    carry, h = lax.scan(_mlstm_chunk_step, carry, (qc, kc, vc, gate_chunks(ig), gate_chunks(lf)))
    h = h.transpose(1, 0, 3, 2, 4).reshape(B, T, M_HEADS, M_DV)
    h = _rms(h, head_g).reshape(B, T, D_MODEL) * jax.nn.sigmoid(o)
    return (h @ w_out).astype(xm.dtype), carry


def _shared_kv(x, kv_g, w_k, w_v, k_g):
    B, T, _ = x.shape
    h = _rms(x, kv_g)
    k = _rms((h @ w_k).reshape(B, T, A_HEADS, 2, A_DH), k_g)
    v = (h @ w_v).reshape(B, T, A_HEADS, 2 * A_DH)
    return k, v


def _diff_attention(xm, k_all, v_all, q_offset, w_q, q_g, lam_p, head_g, w_o, rel_bias, lam_init):
    B, T, _ = xm.shape
    Tk = k_all.shape[1]
    q = _rms((xm @ w_q).reshape(B, T, A_HEADS, 2, A_DH), q_g) * (A_DH ** -0.5)
    lp = lam_p.astype(jnp.float32)
    lam = jnp.exp(jnp.sum(lp[0] * lp[1])) - jnp.exp(jnp.sum(lp[2] * lp[3])) + lam_init
    qb = min(T, Q_BLOCK)
    nb = T // qb
    q_blocks = q.reshape(B, nb, qb, A_HEADS, 2, A_DH).transpose(1, 0, 2, 3, 4, 5)
    k_pos = jnp.arange(Tk)
    k_chunk = k_pos // CHUNK
    table = rel_bias.astype(jnp.float32)

    def block(args):
        qblk, start = args
        q_pos = q_offset + start + jnp.arange(qb)
        s = jnp.einsum('bqhcd,bkhcd->bhcqk', qblk, k_all).astype(jnp.float32)
        bias = table[_rel_bucket(k_pos[None, :] - q_pos[:, None])]
        bias = bias.transpose(2, 0, 1)[None, :, None]
        mask = k_chunk[None, :] <= (q_pos // CHUNK)[:, None]
        p = jax.nn.softmax(jnp.where(mask, s + bias, NEG_BIG), axis=-1)
        a = p[:, :, 0] - lam * p[:, :, 1]
        return jnp.einsum('bhqk,bkhe->bqhe', a, v_all)

    o = lax.map(block, (q_blocks, jnp.arange(nb) * qb))
    o = o.transpose(1, 0, 2, 3, 4).reshape(B, T, A_HEADS, 2 * A_DH)
    o = _rms(o, head_g) * (1 - lam_init)
    return (o.reshape(B, T, D_MODEL) @ w_o).astype(xm.dtype)


def _trunk(x, c, mstate, past_k, past_v, q_offset, chunk_len, p):
    new_C, new_n, new_m = [], [], []
    k_all = v_all = new_k = new_v = None
    for l in range(DEPTH):
        mod = jax.nn.silu(c) @ p['ada_w'][l] + p['ada_b'][l]
        sh1, sc1, g1, sh2, sc2, g2 = jnp.split(mod, 6, axis=-1)
        xm = _modulate(x, p['norm_g'][l, 0], sh1, sc1)
        if l < N_A_LAYERS:
            carry = (mstate[0][l], mstate[1][l], mstate[2][l])
            y, (C, n, m) = _mlstm_mixer(xm, p['mlstm_w_in'][l], p['mlstm_b_gates'][l], p['mlstm_head_g'][l],
                                        p['mlstm_w_out'][l], carry, chunk_len)
            new_C.append(C)
            new_n.append(n)
            new_m.append(m)
        else:
            j = l - N_A_LAYERS
            lam_init = 0.8 - 0.6 * math.exp(-0.3 * l)
            y = _diff_attention(xm, k_all, v_all, q_offset, p['attn_w_q'][j], p['q_norm_g'][j],
                                p['attn_lambda'][j], p['attn_head_g'][j], p['attn_w_o'][j],
                                p['rel_bias'], lam_init)
        x = x + g1[:, None, :] * y
        xm = _modulate(x, p['norm_g'][l, 1], sh2, sc2)
        x = x + g2[:, None, :] * _swiglu(xm, p['ffn_w_gate'][l], p['ffn_w_up'][l], p['ffn_w_down'][l])
        if l == N_A_LAYERS - 1:
            new_k, new_v = _shared_kv(x, p['kv_norm_g'], p['w_k'], p['w_v'], p['k_norm_g'])
            if past_k is None:
                k_all, v_all = new_k, new_v
            else:
                k_all = jnp.concatenate([past_k, new_k], axis=1)
                v_all = jnp.concatenate([past_v, new_v], axis=1)
    return x, jnp.stack(new_C), jnp.stack(new_n), jnp.stack(new_m), new_k, new_v


def setup_inputs(seed: int = 0) -> dict:
    key = jax.random.key(seed)
    ks = jax.random.split(key, 40)

    def nrm(k, shape, scale):
        return jax.random.normal(k, shape, jnp.float32) * scale

    def gain(k, shape):
        return 1.0 + nrm(k, shape, 0.02)

    fan = D_MODEL ** -0.5
    b_in = nrm(ks[12], (N_A_LAYERS, M_HEADS), 0.1)
    b_f = jnp.linspace(3.0, 6.0, M_HEADS, dtype=jnp.float32)[None, :] + nrm(ks[13], (N_A_LAYERS, M_HEADS), 0.1)
    return {
        'x_prompt': nrm(ks[0], (BATCH, SEQ, D_MODEL), 1.0),
        'x_sample': nrm(ks[1], (DEC_BATCH, DEC_SEQ, D_MODEL), 1.0),
        'c_prompt': nrm(ks[2], (BATCH, D_MODEL), 1.0),
        'c_sample': nrm(ks[3], (DEC_BATCH, D_MODEL), 1.0),
        'state_mlstm_C': nrm(ks[4], (N_A_LAYERS, DEC_BATCH, M_HEADS, M_DK, M_DV), 0.1),
        'state_mlstm_n': nrm(ks[5], (N_A_LAYERS, DEC_BATCH, M_HEADS, M_DK), 0.1),
        'state_mlstm_m': jax.random.uniform(ks[6], (N_A_LAYERS, DEC_BATCH, M_HEADS), jnp.float32, 0.0, 2.0),
        'cache_k': nrm(ks[7], (DEC_BATCH, PAST_LEN, A_HEADS, 2, A_DH), 1.0),
        'cache_v': nrm(ks[8], (DEC_BATCH, PAST_LEN, A_HEADS, 2 * A_DH), 1.0),
        'ada_w': nrm(ks[9], (DEPTH, D_MODEL, 6 * D_MODEL), 0.5 * fan),
        'ada_b': nrm(ks[10], (DEPTH, 6 * D_MODEL), 0.02),
        'norm_g': gain(ks[11], (DEPTH, 2, D_MODEL)),
        'mlstm_w_in': nrm(ks[14], (N_A_LAYERS, D_MODEL, M_IN), fan),
        'mlstm_b_gates': jnp.concatenate([b_in, b_f], axis=-1),
        'mlstm_head_g': gain(ks[15], (N_A_LAYERS, M_HEADS, M_DV)),
        'mlstm_w_out': nrm(ks[16], (N_A_LAYERS, D_MODEL, D_MODEL), fan),
        'kv_norm_g': gain(ks[17], (D_MODEL,)),
        'w_k': nrm(ks[18], (D_MODEL, D_MODEL), fan),
        'w_v': nrm(ks[19], (D_MODEL, D_MODEL), fan),
        'k_norm_g': gain(ks[20], (A_DH,)),
        'attn_w_q': nrm(ks[21], (N_B_LAYERS, D_MODEL, D_MODEL), fan),
        'q_norm_g': gain(ks[22], (N_B_LAYERS, A_DH)),
        'attn_lambda': nrm(ks[23], (N_B_LAYERS, 4, A_DH), 0.1),
        'attn_head_g': gain(ks[24], (N_B_LAYERS, 2 * A_DH)),
        'attn_w_o': nrm(ks[25], (N_B_LAYERS, D_MODEL, D_MODEL), fan),
        'rel_bias': nrm(ks[26], (N_BUCKETS, A_HEADS), 0.5),
        'ffn_w_gate': nrm(ks[27], (DEPTH, D_MODEL, D_FF), fan),
        'ffn_w_up': nrm(ks[28], (DEPTH, D_MODEL, D_FF), fan),
        'ffn_w_down': nrm(ks[29], (DEPTH, D_FF, D_MODEL), D_FF ** -0.5),
    }


def reference(x_prompt, x_sample, c_prompt, c_sample, state_mlstm_C, state_mlstm_n, state_mlstm_m,
              cache_k, cache_v, ada_w, ada_b, norm_g, mlstm_w_in, mlstm_b_gates, mlstm_head_g, mlstm_w_out,
              kv_norm_g, w_k, w_v, k_norm_g, attn_w_q, q_norm_g, attn_lambda, attn_head_g, attn_w_o,
              rel_bias, ffn_w_gate, ffn_w_up, ffn_w_down):
    p = dict(ada_w=ada_w, ada_b=ada_b, norm_g=norm_g, mlstm_w_in=mlstm_w_in, mlstm_b_gates=mlstm_b_gates,
             mlstm_head_g=mlstm_head_g, mlstm_w_out=mlstm_w_out, kv_norm_g=kv_norm_g, w_k=w_k, w_v=w_v,
             k_norm_g=k_norm_g, attn_w_q=attn_w_q, q_norm_g=q_norm_g, attn_lambda=attn_lambda,
             attn_head_g=attn_head_g, attn_w_o=attn_w_o, rel_bias=rel_bias, ffn_w_gate=ffn_w_gate,
             ffn_w_up=ffn_w_up, ffn_w_down=ffn_w_down)
    B = x_prompt.shape[0]
    zero_state = (jnp.zeros((N_A_LAYERS, B, M_HEADS, M_DK, M_DV), jnp.float32),
                  jnp.zeros((N_A_LAYERS, B, M_HEADS, M_DK), jnp.float32),
                  jnp.zeros((N_A_LAYERS, B, M_HEADS), jnp.float32))
    y_prompt, prompt_C, prompt_n, prompt_m, prompt_k, prompt_v = _trunk(
        x_prompt, c_prompt, zero_state, None, None, 0, CHUNK, p)
    y_sample, sample_C, sample_n, sample_m, sample_k, sample_v = _trunk(
        x_sample, c_sample, (state_mlstm_C, state_mlstm_n, state_mlstm_m), cache_k, cache_v,
        PAST_LEN, x_sample.shape[1], p)
    return (y_prompt, y_sample, prompt_C, prompt_n, prompt_m, prompt_k, prompt_v,
            sample_C, sample_n, sample_m, sample_k, sample_v)
```

```python
import functools
import math

import jax
import jax.numpy as jnp
from jax import lax
from jax.experimental import pallas as pl
from jax.experimental.pallas import tpu as pltpu

F32 = jnp.float32
BF16 = jnp.bfloat16

EPS = 1e-6
NEG_BIG = -1e30
CHUNK = 64
N_BUCKETS = 32
MAX_DISTANCE = 128
M_HEADS = 8
A_HEADS = 8
GATE_PAD = 128

V7X_VMEM_BYTES = 64 * 1024 * 1024
VMEM_LIMIT = V7X_VMEM_BYTES - 8 * 1024 * 1024

ATTN_TILE = 256
MLSTM_PROMPT_CHUNK = 128
FFN_CHUNK = 256


def _params(*sem):
    return pltpu.CompilerParams(dimension_semantics=sem, vmem_limit_bytes=VMEM_LIMIT)


def _rms(x, g):
    return x * lax.rsqrt(jnp.mean(x * x, axis=-1, keepdims=True) + EPS) * g


def _log_sigmoid(x):
    return jnp.minimum(x, 0.0) - jnp.log1p(jnp.exp(-jnp.abs(x)))


def _dot(a, b):
    return jnp.dot(a, b, preferred_element_type=F32)


def _dot_nt(a, b):
    return lax.dot_general(a, b, (((1,), (1,)), ((), ())), preferred_element_type=F32)


def _dot_tn(a, b):
    return lax.dot_general(a, b, (((0,), (0,)), ((), ())), preferred_element_type=F32)


def _resident(shape):
    nd = len(shape)
    return pl.BlockSpec(shape, lambda *_: (0,) * nd, pipeline_mode=pl.Buffered(1))


def _adaln_kernel(c_ref, w_ref, b_ref, o_ref):
    c = c_ref[...]
    a = (c * jax.nn.sigmoid(c)).astype(BF16)
    o_ref[0] = _dot(a, w_ref[0].astype(BF16)) + b_ref[0]


def _adaln(c, ada_w, ada_b):
    depth, d, n = ada_w.shape
    rows = c.shape[0]
    tn = 1536
    assert n % tn == 0
    return pl.pallas_call(
        _adaln_kernel,
        grid=(depth, n // tn),
        in_specs=[
            pl.BlockSpec((rows, d), lambda l, j: (0, 0)),
            pl.BlockSpec((1, d, tn), lambda l, j: (l, 0, j)),
            pl.BlockSpec((1, 1, tn), lambda l, j: (l, 0, j)),
        ],
        out_specs=pl.BlockSpec((1, rows, tn), lambda l, j: (l, 0, j)),
        out_shape=jax.ShapeDtypeStruct((depth, rows, n), F32),
        compiler_params=_params("parallel", "parallel"),
        name="adaln",
    )(c, ada_w, ada_b.reshape(depth, 1, n))


def _modulated(x_ref, g_ref, sh_ref, sc_ref):
    x = x_ref[...]
    bb, tt, d = x.shape
    xm = _rms(x, g_ref[...]) * (1.0 + sc_ref[...]) + sh_ref[...]
    return xm.reshape(bb * tt, d).astype(BF16)


def _group_rms_store(src_ref, dst_ref, gain_ref, post_scale, width):
    bb, tt, n = dst_ref.shape
    for g in range(n // width):
        seg = src_ref[:, g * width:(g + 1) * width]
        y = _rms(seg, gain_ref[...]) * post_scale
        dst_ref[:, :, g * width:(g + 1) * width] = y.reshape(bb, tt, width).astype(dst_ref.dtype)


def _norm_proj_kernel(*refs, n_chunk, gates, qk_scale):
    x_ref, g_ref, sh_ref, sc_ref, w_ref = refs[:5]
    pos = 5
    if gates:
        wg_ref, wgt_ref = refs[pos:pos + 2]
        pos += 2
    if qk_scale is not None:
        qg_ref = refs[pos]
        pos += 1
    o_ref = refs[pos]
    pos += 1
    if gates:
        og_ref, ogt_ref = refs[pos:pos + 2]
        pos += 2
    if qk_scale is not None:
        y_scr = refs[pos]

    bb, tt, n = o_ref.shape
    xm = _modulated(x_ref, g_ref, sh_ref, sc_ref)
    for c in range(n // n_chunk):
        y = _dot(xm, w_ref[:, c * n_chunk:(c + 1) * n_chunk])
        if qk_scale is None:
            o_ref[:, :, c * n_chunk:(c + 1) * n_chunk] = y.reshape(bb, tt, n_chunk).astype(o_ref.dtype)
        else:
            y_scr[:, c * n_chunk:(c + 1) * n_chunk] = y
    if qk_scale is not None:
        _group_rms_store(y_scr, o_ref, qg_ref, qk_scale, qg_ref.shape[-1])
    if gates:
        og_ref[...] = _dot(xm, wg_ref[...]).reshape(bb, tt, GATE_PAD)
        for i in range(bb):
            ogt_ref[i] = _dot_nt(wgt_ref[...], xm[i * tt:(i + 1) * tt])


def _row_blocks(b, t, rows):
    if t >= rows:
        assert t % rows == 0
        return 1, rows
    assert rows % t == 0 and b % (rows // t) == 0
    return rows // t, t


def _norm_proj(x, gain, shift, scale, w, *, rows, w_gate=None, qk_gain=None, qk_scale=None):
    b, t, d = x.shape
    n = w.shape[1]
    bb, tt = _row_blocks(b, t, rows)
    grid = (b // bb, t // tt)
    tok = lambda i, j: (i, j, 0)
    per_b = lambda i, j: (i, 0, 0)
    in_specs = [
        pl.BlockSpec((bb, tt, d), tok),
        _resident((1, d)),
        pl.BlockSpec((bb, 1, d), per_b),
        pl.BlockSpec((bb, 1, d), per_b),
        _resident((d, n)),
    ]
    args = [x, gain.reshape(1, d), shift, scale, w]
    out_specs = [pl.BlockSpec((bb, tt, n), tok)]
    out_shape = [jax.ShapeDtypeStruct((b, t, n), BF16)]
    scratch = []
    gates = w_gate is not None
    if gates:
        ng = w_gate.shape[1]
        wg = jnp.zeros((d, GATE_PAD), BF16).at[:, :ng].set(w_gate)
        in_specs += [_resident((d, GATE_PAD)), _resident((ng, d))]
        args += [wg, w_gate.T]
        out_specs += [pl.BlockSpec((bb, tt, GATE_PAD), tok),
                      pl.BlockSpec((bb, ng, tt), lambda i, j: (i, 0, j))]
        out_shape += [jax.ShapeDtypeStruct((b, t, GATE_PAD), F32),
                      jax.ShapeDtypeStruct((b, ng, t), F32)]
    if qk_scale is not None:
        in_specs.append(_resident((1, qk_gain.shape[-1])))
        args.append(qk_gain.reshape(1, -1))
        scratch.append(pltpu.VMEM((bb * tt, n), F32))
    kern = functools.partial(_norm_proj_kernel, n_chunk=min(n, 1024), gates=gates, qk_scale=qk_scale)
    out = pl.pallas_call(
        kern, grid=grid, in_specs=in_specs, out_specs=out_specs, out_shape=out_shape,
        scratch_shapes=scratch, compiler_params=_params("parallel", "parallel"), name="norm_proj",
    )(*args)
    return out if gates else out[0]


def _kv_kernel(x_ref, g_ref, wk_ref, wv_ref, kg_ref, k_ref, v_ref, kb_ref, vb_ref, y_scr):
    x = x_ref[...]
    bb, tt, d = x.shape
    h = _rms(x, g_ref[...]).reshape(bb * tt, d).astype(BF16)
    y_scr[...] = _dot(h, wk_ref[...])
    _group_rms_store(y_scr, k_ref, kg_ref, 1.0, kg_ref.shape[-1])
    kb_ref[...] = k_ref[...].astype(BF16)
    v = _dot(h, wv_ref[...]).reshape(bb, tt, d)
    v_ref[...] = v
    vb_ref[...] = v.astype(BF16)


def _shared_kv(x, kv_g, w_k, w_v, k_g, *, rows):
    b, t, d = x.shape
    bb, tt = _row_blocks(b, t, rows)
    tok = lambda i, j: (i, j, 0)
    blk = pl.BlockSpec((bb, tt, d), tok)
    return pl.pallas_call(
        _kv_kernel,
        grid=(b // bb, t // tt),
        in_specs=[blk, _resident((1, d)), _resident((d, d)), _resident((d, d)),
                  _resident((1, k_g.shape[-1]))],
        out_specs=[blk, blk, blk, blk],
        out_shape=[jax.ShapeDtypeStruct((b, t, d), F32), jax.ShapeDtypeStruct((b, t, d), F32),
                   jax.ShapeDtypeStruct((b, t, d), BF16), jax.ShapeDtypeStruct((b, t, d), BF16)],
        scratch_shapes=[pltpu.VMEM((bb * tt, d), F32)],
        compiler_params=_params("parallel", "parallel"),
        name="shared_kv",
    )(x, kv_g.reshape(1, d), w_k, w_v, k_g.reshape(1, -1))


def _proj_residual_kernel(a_ref, w_ref, x_ref, gate_ref, o_ref):
    bb, tt, d = x_ref.shape
    a = a_ref[...].reshape(bb * tt, a_ref.shape[-1])
    y = _dot(a, w_ref[...]).reshape(bb, tt, d)
    o_ref[...] = x_ref[...] + gate_ref[...] * y


def _proj_residual(a, w, x, gate, *, rows):
    b, t, d = x.shape
    k = a.shape[-1]
    bb, tt = _row_blocks(b, t, rows)
    tok = lambda i, j: (i, j, 0)
    return pl.pallas_call(
        _proj_residual_kernel,
        grid=(b // bb, t // tt),
        in_specs=[pl.BlockSpec((bb, tt, k), tok), _resident((k, d)),
                  pl.BlockSpec((bb, tt, d), tok), pl.BlockSpec((bb, 1, d), lambda i, j: (i, 0, 0))],
        out_specs=pl.BlockSpec((bb, tt, d), tok),
        out_shape=jax.ShapeDtypeStruct((b, t, d), F32),
        compiler_params=_params("parallel", "parallel"),
        name="proj_residual",
    )(a, w, x, gate)


def _ffn_kernel(x_ref, g_ref, sh_ref, sc_ref, gate_ref, wg_ref, wu_ref, wd_ref, o_ref, acc_ref):
    bb, tt, d = x_ref.shape
    f = wg_ref.shape[1]
    xm = _modulated(x_ref, g_ref, sh_ref, sc_ref)
    for c in range(f // FFN_CHUNK):
        sl = slice(c * FFN_CHUNK, (c + 1) * FFN_CHUNK)
        hg = _dot(xm, wg_ref[:, sl])
        hu = _dot(xm, wu_ref[:, sl])
        act = (hg * jax.nn.sigmoid(hg) * hu).astype(BF16)
        part = _dot(act, wd_ref[sl, :])
        if c == 0:
            acc_ref[...] = part
        else:
            acc_ref[...] += part
    o_ref[...] = x_ref[...] + gate_ref[...] * acc_ref[...].reshape(bb, tt, d)


def _ffn(x, gain, shift, scale, gate, w_gate, w_up, w_down, *, rows):
    b, t, d = x.shape
    f = w_gate.shape[1]
    assert f % FFN_CHUNK == 0
    bb, tt = _row_blocks(b, t, rows)
    tok = lambda i, j: (i, j, 0)
    per_b = lambda i, j: (i, 0, 0)
    return pl.pallas_call(
        _ffn_kernel,
        grid=(b // bb, t // tt),
        in_specs=[pl.BlockSpec((bb, tt, d), tok), _resident((1, d)),
                  pl.BlockSpec((bb, 1, d), per_b), pl.BlockSpec((bb, 1, d), per_b),
                  pl.BlockSpec((bb, 1, d), per_b),
                  _resident((d, f)), _resident((d, f)), _resident((f, d))],
        out_specs=pl.BlockSpec((bb, tt, d), tok),
        out_shape=jax.ShapeDtypeStruct((b, t, d), F32),
        scratch_shapes=[pltpu.VMEM((bb * tt, d), F32)],
        compiler_params=_params("parallel", "parallel"),
        name="ffn",
    )(x, gain.reshape(1, d), shift, scale, gate, w_gate, w_up, w_down)


def _mlstm_kernel(q_ref, k_ref, v_ref, o_ref, g_ref, gt_ref, brow_ref, bcol_ref, hg_ref,
                  c0_ref, n0_ref, m0_ref, h_ref, c_out, n_out, m_out, c_s, n_s, m_s,
                  gc_s, bc_s, gr_s, br_s):
    ci = pl.program_id(1)
    length = q_ref.shape[1]
    dk = c_s.shape[1]
    dv = c_s.shape[2]

    @pl.when(ci == 0)
    def _():
        c_s[...] = c0_ref[0]
        n_s[...] = n0_ref[0]
        m_s[...] = m0_ref[0]

    row = lax.broadcasted_iota(jnp.int32, (length, length), 0)
    col = lax.broadcasted_iota(jnp.int32, (length, length), 1)
    tri = col <= row
    hi = lax.Precision.HIGHEST
    g = g_ref[0] + brow_ref[...]
    gt = gt_ref[0] + bcol_ref[...]
    gc_s[...] = g
    gr_s[...] = gt
    bc_s[...] = jnp.dot(tri.astype(F32), _log_sigmoid(g), precision=hi, preferred_element_type=F32)
    br_s[...] = jnp.dot(_log_sigmoid(gt), (row <= col).astype(F32), precision=hi,
                        preferred_element_type=F32)

    for h in range(M_HEADS):
        qh = q_ref[0, :, h * dk:(h + 1) * dk]
        kf = k_ref[0, :, h * dk:(h + 1) * dk].astype(F32) * (dk ** -0.5)
        kh = kf.astype(BF16)
        vh = v_ref[0, :, h * dv:(h + 1) * dv]
        bc = bc_s[:, M_HEADS + h:M_HEADS + h + 1]
        br = br_s[M_HEADS + h:M_HEADS + h + 1, :]
        igc = gc_s[:, h:h + 1]
        igr = gr_s[h:h + 1, :]
        b_end = br_s[M_HEADS + h:M_HEADS + h + 1, length - 1:length]
        m_prev = m_s[0:1, h:h + 1]
        c_prev = c_s[h]
        n_prev = n_s[h:h + 1, :]

        dmat = jnp.where(tri, bc - br + igr, -jnp.inf)
        inter = bc + m_prev
        m_t = jnp.maximum(inter, jnp.max(dmat, axis=-1, keepdims=True))
        w_intra = jnp.exp(dmat - m_t)
        w_inter = jnp.exp(inter - m_t)
        s = _dot_nt(qh, kh) * w_intra
        num = w_inter * _dot(qh, c_prev.astype(BF16)) + _dot(s.astype(BF16), vh)
        den = (w_inter * jnp.sum(qh.astype(F32) * n_prev, axis=-1, keepdims=True)
               + jnp.sum(s, axis=-1, keepdims=True))
        hh = num / jnp.maximum(jnp.abs(den), jnp.exp(-m_t))
        hh = _rms(hh, hg_ref[:, h * dv:(h + 1) * dv])
        og = o_ref[0, :, h * dv:(h + 1) * dv].astype(F32)
        h_ref[0, :, h * dv:(h + 1) * dv] = (hh * jax.nn.sigmoid(og)).astype(h_ref.dtype)

        g_row = b_end - br + igr
        g_col = b_end - bc + igc
        m_new = jnp.maximum(b_end + m_prev, jnp.max(g_row, axis=-1, keepdims=True))
        w_old = jnp.exp(b_end + m_prev - m_new)
        kw = kf * jnp.exp(g_col - m_new)
        c_s[h] = w_old * c_prev + _dot_tn(kw.astype(BF16), vh)
        n_s[h:h + 1, :] = w_old * n_prev + jnp.sum(kw, axis=0, keepdims=True)
        m_s[0:1, h:h + 1] = m_new

    @pl.when(ci == pl.num_programs(1) - 1)
    def _():
        c_out[0] = c_s[...]
        n_out[0] = n_s[...]
        m_out[0] = m_s[...]


def _mlstm(qkvo, g, gt, b_gates, head_g, c0, n0, m0, *, chunk):
    b, t, _ = qkvo.shape
    heads, dk, dv = c0.shape[1:]
    d = heads * dv
    qk = heads * dk
    assert qk * 2 == d and t % chunk == 0
    nc = t // chunk
    two_h = 2 * heads
    brow = jnp.zeros((1, GATE_PAD), F32).at[0, :two_h].set(b_gates)
    st4 = lambda i, c: (i, 0, 0, 0)
    st3 = lambda i, c: (i, 0, 0)
    outs = pl.pallas_call(
        _mlstm_kernel,
        grid=(b, nc),
        in_specs=[
            pl.BlockSpec((1, chunk, qk), lambda i, c: (i, c, 0)),
            pl.BlockSpec((1, chunk, qk), lambda i, c: (i, c, 1)),
            pl.BlockSpec((1, chunk, d), lambda i, c: (i, c, 1)),
            pl.BlockSpec((1, chunk, d), lambda i, c: (i, c, 2)),
            pl.BlockSpec((1, chunk, GATE_PAD), lambda i, c: (i, c, 0)),
            pl.BlockSpec((1, two_h, chunk), lambda i, c: (i, 0, c)),
            _resident((1, GATE_PAD)), _resident((two_h, 1)), _resident((1, d)),
            pl.BlockSpec((1, heads, dk, dv), st4),
            pl.BlockSpec((1, heads, dk), st3),
            pl.BlockSpec((1, 1, heads), st3),
        ],
        out_specs=[
            pl.BlockSpec((1, chunk, d), lambda i, c: (i, c, 0)),
            pl.BlockSpec((1, heads, dk, dv), st4),
            pl.BlockSpec((1, heads, dk), st3),
            pl.BlockSpec((1, 1, heads), st3),
        ],
        out_shape=[
            jax.ShapeDtypeStruct((b, t, d), BF16),
            jax.ShapeDtypeStruct((b, heads, dk, dv), F32),
            jax.ShapeDtypeStruct((b, heads, dk), F32),
            jax.ShapeDtypeStruct((b, 1, heads), F32),
        ],
        scratch_shapes=[pltpu.VMEM((heads, dk, dv), F32), pltpu.VMEM((heads, dk), F32),
                        pltpu.VMEM((1, heads), F32),
                        pltpu.VMEM((chunk, GATE_PAD), F32), pltpu.VMEM((chunk, GATE_PAD), F32),
                        pltpu.VMEM((two_h, chunk), F32), pltpu.VMEM((two_h, chunk), F32)],
        compiler_params=_params("parallel", "arbitrary"),
        name="mlstm",
    )(qkvo, qkvo, qkvo, qkvo, g, gt, brow, b_gates.reshape(two_h, 1), head_g.reshape(1, d),
      c0, n0, m0.reshape(b, 1, heads))
    hn, c_new, n_new, m_new = outs
    return hn, c_new, n_new, m_new.reshape(b, heads)


def _rel_bucket(rel):
    nb = N_BUCKETS // 2
    max_exact = nb // 2
    ret = jnp.where(rel > 0, nb, 0)
    n = jnp.abs(rel)
    large = max_exact + (jnp.log(jnp.maximum(n, 1).astype(F32) / max_exact)
                         / math.log(MAX_DISTANCE / max_exact) * (nb - max_exact)).astype(jnp.int32)
    large = jnp.minimum(large, nb - 1)
    return ret + jnp.where(n < max_exact, n, large)


def _bias_tile(rel_bias, q_pos, k_pos):
    table = rel_bias.astype(F32)
    rel = k_pos[None, :] - q_pos[:, None]
    bias = table[_rel_bucket(rel)].transpose(2, 0, 1)
    bias = bias - table[N_BUCKETS // 2 - 1][:, None, None]
    mask = (k_pos // CHUNK)[None, :] <= (q_pos // CHUNK)[:, None]
    return jnp.where(mask[None], bias, NEG_BIG)


def _flash_update(hc, q, k, v, bias, m_s, l_s, acc_s):
    s = _dot_nt(q, k)
    if bias is not None:
        s = s + bias
    m_prev = m_s[hc]
    m_new = jnp.maximum(m_prev, jnp.max(s, axis=-1, keepdims=True))
    alpha = jnp.exp(m_prev - m_new)
    p = jnp.exp(s - m_new)
    l_s[hc] = alpha * l_s[hc] + jnp.sum(p, axis=-1, keepdims=True)
    acc_s[hc] = alpha * acc_s[hc] + _dot(p.astype(BF16), v)
    m_s[hc] = m_new


def _flash_init(m_s, l_s, acc_s):
    m_s[...] = jnp.full(m_s.shape, NEG_BIG, F32)
    l_s[...] = jnp.zeros(l_s.shape, F32)
    acc_s[...] = jnp.zeros(acc_s.shape, F32)


def _flash_tile(q_ref, k_ref, v_ref, bias_fn, m_s, l_s, acc_s):
    dv = acc_s.shape[-1]
    dh = dv // 2
    for h in range(A_HEADS):
        v = v_ref[0, :, h * dv:(h + 1) * dv].astype(BF16)
        bias = bias_fn(h)
        for c in range(2):
            lo = h * dv + c * dh
            q = q_ref[0, :, lo:lo + dh]
            k = k_ref[0, :, lo:lo + dh].astype(BF16)
            _flash_update(2 * h + c, q, k, v, bias, m_s, l_s, acc_s)


def _flash_finish(o_ref, lam_ref, hg_ref, lam_init, m_s, l_s, acc_s):
    dv = acc_s.shape[-1]
    lp = lam_ref[...]
    lam = (jnp.exp(jnp.sum(lp[0:1] * lp[1:2], axis=-1, keepdims=True))
           - jnp.exp(jnp.sum(lp[2:3] * lp[3:4], axis=-1, keepdims=True)) + lam_init)
    for h in range(A_HEADS):
        o = acc_s[2 * h] / l_s[2 * h] - lam * (acc_s[2 * h + 1] / l_s[2 * h + 1])
        o = _rms(o, hg_ref[...]) * (1.0 - lam_init)
        o_ref[0, :, h * dv:(h + 1) * dv] = o.astype(o_ref.dtype)


def _attn_prompt_kernel(q_ref, k_ref, v_ref, bias_ref, lam_ref, hg_ref, o_ref, m_s, l_s, acc_s,
                        *, lam_init):
    qi = pl.program_id(1)
    ki = pl.program_id(2)

    @pl.when(ki == 0)
    def _():
        _flash_init(m_s, l_s, acc_s)

    slot = jnp.where(ki >= qi - 1, ki - (qi - 1), 2)

    @pl.when(ki <= qi)
    def _():
        _flash_tile(q_ref, k_ref, v_ref, lambda h: bias_ref[h, slot], m_s, l_s, acc_s)

    @pl.when(ki == qi)
    def _():
        _flash_finish(o_ref, lam_ref, hg_ref, lam_init, m_s, l_s, acc_s)


def _attn_prompt(q, k, v, rel_bias, lam_p, head_g, lam_init):
    b, t, d = q.shape
    tile = ATTN_TILE
    assert t % tile == 0 and tile % CHUNK == 0 and tile >= MAX_DISTANCE
    n = t // tile
    dv = d // A_HEADS
    pos = jnp.arange(tile)
    bias = jnp.stack([_bias_tile(rel_bias, pos + tile, pos), _bias_tile(rel_bias, pos, pos),
                      jnp.zeros((A_HEADS, tile, tile), F32)], axis=1)
    kv_spec = pl.BlockSpec((1, tile, d), lambda i, qi, ki: (i, jnp.minimum(ki, qi), 0))
    q_spec = pl.BlockSpec((1, tile, d), lambda i, qi, ki: (i, qi, 0))
    return pl.pallas_call(
        functools.partial(_attn_prompt_kernel, lam_init=lam_init),
        grid=(b, n, n),
        in_specs=[q_spec, kv_spec, kv_spec, _resident(bias.shape), _resident(lam_p.shape),
                  _resident((1, dv))],
        out_specs=q_spec,
        out_shape=jax.ShapeDtypeStruct((b, t, d), BF16),
        scratch_shapes=[pltpu.VMEM((2 * A_HEADS, tile, 1), F32), pltpu.VMEM((2 * A_HEADS, tile, 1), F32),
                        pltpu.VMEM((2 * A_HEADS, tile, dv), F32)],
        compiler_params=_params("parallel", "parallel", "arbitrary"),
        name="attn_prompt",
    )(q, k, v, bias, lam_p, head_g.reshape(1, dv))


def _attn_sample_kernel(q_ref, ck_ref, cv_ref, nk_ref, nv_ref, bias_c_ref, bias_n_ref, lam_ref, hg_ref,
                        o_ref, m_s, l_s, acc_s, *, lam_init):
    j = pl.program_id(1)
    n_cache = pl.num_programs(1) - 1

    @pl.when(j == 0)
    def _():
        _flash_init(m_s, l_s, acc_s)

    @pl.when(j < n_cache - 1)
    def _():
        _flash_tile(q_ref, ck_ref, cv_ref, lambda h: None, m_s, l_s, acc_s)

    @pl.when(j == n_cache - 1)
    def _():
        _flash_tile(q_ref, ck_ref, cv_ref, lambda h: bias_c_ref[h], m_s, l_s, acc_s)

    @pl.when(j == n_cache)
    def _():
        _flash_tile(q_ref, nk_ref, nv_ref, lambda h: bias_n_ref[h], m_s, l_s, acc_s)
        _flash_finish(o_ref, lam_ref, hg_ref, lam_init, m_s, l_s, acc_s)


def _attn_sample(q, cache_k, cache_v, new_k, new_v, rel_bias, lam_p, head_g, lam_init):
    b, t, d = q.shape
    past = cache_k.shape[1]
    tile = ATTN_TILE
    assert past % tile == 0 and past % CHUNK == 0 and tile >= MAX_DISTANCE
    n_cache = past // tile
    dv = d // A_HEADS
    q_pos = past + jnp.arange(t)
    bias_c = _bias_tile(rel_bias, q_pos, past - tile + jnp.arange(tile))
    bias_n = _bias_tile(rel_bias, q_pos, q_pos)
    last = n_cache - 1
    cache_spec = pl.BlockSpec((1, tile, d), lambda i, j: (i, jnp.minimum(j, last), 0))
    tok_spec = pl.BlockSpec((1, t, d), lambda i, j: (i, 0, 0))
    return pl.pallas_call(
        functools.partial(_attn_sample_kernel, lam_init=lam_init),
        grid=(b, n_cache + 1),
        in_specs=[tok_spec, cache_spec, cache_spec, tok_spec, tok_spec,
                  _resident(bias_c.shape), _resident(bias_n.shape), _resident(lam_p.shape),
                  _resident((1, dv))],
        out_specs=tok_spec,
        out_shape=jax.ShapeDtypeStruct((b, t, d), BF16),
        scratch_shapes=[pltpu.VMEM((2 * A_HEADS, t, 1), F32), pltpu.VMEM((2 * A_HEADS, t, 1), F32),
                        pltpu.VMEM((2 * A_HEADS, t, dv), F32)],
        compiler_params=_params("parallel", "arbitrary"),
        name="attn_sample",
    )(q, cache_k.reshape(b, past, d), cache_v.reshape(b, past, d), new_k, new_v,
      bias_c, bias_n, lam_p, head_g.reshape(1, dv))


def _trunk(x, mod, mstate, past_k, past_v, chunk, rows, p):
    b, t, d = x.shape
    depth = p["norm_g"].shape[0]
    n_a = p["mlstm_w_in"].shape[0]
    qk = M_HEADS * (d // M_HEADS // 2)
    new_c, new_n, new_m = [], [], []
    k_f32 = v_f32 = k_bf = v_bf = None
    for l in range(depth):
        sh1, sc1, g1, sh2, sc2, g2 = [mod[l, :, i * d:(i + 1) * d].reshape(b, 1, d) for i in range(6)]
        if l < n_a:
            w_in = p["mlstm_w_in"][l]
            n_main = 2 * qk + 2 * d
            qkvo, g, gt = _norm_proj(x, p["norm_g"][l, 0], sh1, sc1, w_in[:, :n_main], rows=rows,
                                     w_gate=w_in[:, n_main:])
            hn, c_l, n_l, m_l = _mlstm(qkvo, g, gt, p["mlstm_b_gates"][l], p["mlstm_head_g"][l],
                                       mstate[0][l], mstate[1][l], mstate[2][l], chunk=chunk)
            new_c.append(c_l)
            new_n.append(n_l)
            new_m.append(m_l)
            x = _proj_residual(hn, p["mlstm_w_out"][l], x, g1, rows=rows)
        else:
            j = l - n_a
            lam_init = 0.8 - 0.6 * math.exp(-0.3 * l)
            dh = p["q_norm_g"].shape[-1]
            q = _norm_proj(x, p["norm_g"][l, 0], sh1, sc1, p["attn_w_q"][j], rows=rows,
                           qk_gain=p["q_norm_g"][j], qk_scale=dh ** -0.5)
            if past_k is None:
                o = _attn_prompt(q, k_bf, v_bf, p["rel_bias"], p["attn_lambda"][j], p["attn_head_g"][j],
                                 lam_init)
            else:
                o = _attn_sample(q, past_k, past_v, k_f32, v_f32, p["rel_bias"], p["attn_lambda"][j],
                                 p["attn_head_g"][j], lam_init)
            x = _proj_residual(o, p["attn_w_o"][j], x, g1, rows=rows)
        x = _ffn(x, p["norm_g"][l, 1], sh2, sc2, g2, p["ffn_w_gate"][l], p["ffn_w_up"][l],
                 p["ffn_w_down"][l], rows=rows)
        if l == n_a - 1:
            k_f32, v_f32, k_bf, v_bf = _shared_kv(x, p["kv_norm_g"], p["w_k"], p["w_v"], p["k_norm_g"],
                                                  rows=rows)
    dh = p["k_norm_g"].shape[-1]
    return (x, jnp.stack(new_c), jnp.stack(new_n), jnp.stack(new_m),
            k_f32.reshape(b, t, A_HEADS, 2, dh), v_f32.reshape(b, t, A_HEADS, 2 * dh))


def kernel(x_prompt, x_sample, c_prompt, c_sample, state_mlstm_C, state_mlstm_n, state_mlstm_m, cache_k, cache_v, ada_w, ada_b, norm_g, mlstm_w_in, mlstm_b_gates, mlstm_head_g, mlstm_w_out, kv_norm_g, w_k, w_v, k_norm_g, attn_w_q, q_norm_g, attn_lambda, attn_head_g, attn_w_o, rel_bias, ffn_w_gate, ffn_w_up, ffn_w_down):
    bf = lambda w: w.astype(BF16)
    p = dict(norm_g=norm_g, mlstm_w_in=bf(mlstm_w_in), mlstm_b_gates=mlstm_b_gates,
             mlstm_head_g=mlstm_head_g, mlstm_w_out=bf(mlstm_w_out), kv_norm_g=kv_norm_g, w_k=bf(w_k),
             w_v=bf(w_v), k_norm_g=k_norm_g, attn_w_q=bf(attn_w_q), q_norm_g=q_norm_g,
             attn_lambda=attn_lambda, attn_head_g=attn_head_g, attn_w_o=bf(attn_w_o), rel_bias=rel_bias,
             ffn_w_gate=bf(ffn_w_gate), ffn_w_up=bf(ffn_w_up), ffn_w_down=bf(ffn_w_down))
    bp = x_prompt.shape[0]
    n_a, _, heads, dk, dv = state_mlstm_C.shape
    mod = _adaln(jnp.concatenate([c_prompt, c_sample], axis=0), ada_w, ada_b)
    zero_state = (jnp.zeros((n_a, bp, heads, dk, dv), F32), jnp.zeros((n_a, bp, heads, dk), F32),
                  jnp.zeros((n_a, bp, heads), F32))
    y_p, c_p, n_p, m_p, k_p, v_p = _trunk(x_prompt, mod[:, :bp], zero_state, None, None,
                                          MLSTM_PROMPT_CHUNK, 512, p)
    y_s, c_s, n_s, m_s, k_s, v_s = _trunk(x_sample, mod[:, bp:], (state_mlstm_C, state_mlstm_n, state_mlstm_m),
                                          cache_k, cache_v, x_sample.shape[1], 512, p)
    return (y_p, y_s, c_p, n_p, m_p, k_p, v_p, c_s, n_s, m_s, k_s, v_s)
```

```python
import functools
import math

import jax
import jax.numpy as jnp
from jax import lax
from jax.experimental import pallas as pl
from jax.experimental.pallas import tpu as pltpu

F32 = jnp.float32
BF16 = jnp.bfloat16

EPS = 1e-6
NEG_BIG = -1e30
CHUNK = 64
N_BUCKETS = 32
MAX_DISTANCE = 128
M_HEADS = 8
A_HEADS = 8
GATE_PAD = 128
ONES_ROWS = 16
LOG2E = math.log2(math.e)

V7X_VMEM_BYTES = 64 * 1024 * 1024
VMEM_LIMIT = V7X_VMEM_BYTES - 8 * 1024 * 1024

ATTN_TILE = 256
MLSTM_PROMPT_CHUNK = 128
FFN_CHUNK = 256


def _params(*sem, flags=None):
    return pltpu.CompilerParams(dimension_semantics=sem, vmem_limit_bytes=VMEM_LIMIT, flags=flags)


def _rms(x, g):
    return x * lax.rsqrt(jnp.mean(x * x, axis=-1, keepdims=True) + EPS) * g


def _log_sigmoid(x):
    return jnp.minimum(x, 0.0) - jnp.log1p(jnp.exp(-jnp.abs(x)))


def _dot(a, b):
    return jnp.dot(a, b, preferred_element_type=F32)


def _dot_nt(a, b):
    return lax.dot_general(a, b, (((1,), (1,)), ((), ())), preferred_element_type=F32)


def _dot_tn(a, b):
    return lax.dot_general(a, b, (((0,), (0,)), ((), ())), preferred_element_type=F32)


def _resident(shape):
    nd = len(shape)
    return pl.BlockSpec(shape, lambda *_: (0,) * nd, pipeline_mode=pl.Buffered(1))


def _adaln_kernel(c_ref, w_ref, b_ref, o_ref):
    c = c_ref[...]
    a = (c * jax.nn.sigmoid(c)).astype(BF16)
    o_ref[0] = _dot(a, w_ref[0].astype(BF16)) + b_ref[0]


def _adaln(c, ada_w, ada_b):
    depth, d, n = ada_w.shape
    rows = c.shape[0]
    tn = 1536
    assert n % tn == 0
    return pl.pallas_call(
        _adaln_kernel,
        grid=(depth, n // tn),
        in_specs=[
            pl.BlockSpec((rows, d), lambda l, j: (0, 0)),
            pl.BlockSpec((1, d, tn), lambda l, j: (l, 0, j)),
            pl.BlockSpec((1, 1, tn), lambda l, j: (l, 0, j)),
        ],
        out_specs=pl.BlockSpec((1, rows, tn), lambda l, j: (l, 0, j)),
        out_shape=jax.ShapeDtypeStruct((depth, rows, n), F32),
        compiler_params=_params("parallel", "parallel"),
        name="adaln",
    )(c, ada_w, ada_b.reshape(depth, 1, n))


def _modulated(x_ref, g_ref, sh_ref, sc_ref):
    x = x_ref[...]
    bb, tt, d = x.shape
    xm = _rms(x, g_ref[...]) * (1.0 + sc_ref[...]) + sh_ref[...]
    return xm.reshape(bb * tt, d).astype(BF16)


def _group_rms_store(src_ref, dst_ref, gain_ref, post_scale, width):
    bb, tt, n = dst_ref.shape
    for g in range(n // width):
        seg = src_ref[:, g * width:(g + 1) * width]
        y = _rms(seg, gain_ref[...]) * post_scale
        dst_ref[:, :, g * width:(g + 1) * width] = y.reshape(bb, tt, width).astype(dst_ref.dtype)


def _norm_proj_kernel(*refs, n_chunk, gates, qk_scale):
    x_ref, g_ref, sh_ref, sc_ref, w_ref = refs[:5]
    pos = 5
    if gates:
        wg_ref, wgt_ref = refs[pos:pos + 2]
        pos += 2
    if qk_scale is not None:
        qg_ref = refs[pos]
        pos += 1
    o_ref = refs[pos]
    pos += 1
    if gates:
        og_ref, ogt_ref = refs[pos:pos + 2]
        pos += 2
    if qk_scale is not None:
        y_scr = refs[pos]

    bb, tt, n = o_ref.shape
    xm = _modulated(x_ref, g_ref, sh_ref, sc_ref)
    for c in range(n // n_chunk):
        y = _dot(xm, w_ref[:, c * n_chunk:(c + 1) * n_chunk])
        if qk_scale is None:
            o_ref[:, :, c * n_chunk:(c + 1) * n_chunk] = y.reshape(bb, tt, n_chunk).astype(o_ref.dtype)
        else:
            y_scr[:, c * n_chunk:(c + 1) * n_chunk] = y
    if qk_scale is not None:
        _group_rms_store(y_scr, o_ref, qg_ref, qk_scale, qg_ref.shape[-1])
    if gates:
        og_ref[...] = _dot(xm, wg_ref[...]).reshape(bb, tt, GATE_PAD)
        for i in range(bb):
            ogt_ref[i] = _dot_nt(wgt_ref[...], xm[i * tt:(i + 1) * tt])


def _row_blocks(b, t, rows):
    if t >= rows:
        assert t % rows == 0
        return 1, rows
    assert rows % t == 0 and b % (rows // t) == 0
    return rows // t, t


def _norm_proj(x, gain, shift, scale, w, *, rows, w_gate=None, qk_gain=None, qk_scale=None):
    b, t, d = x.shape
    n = w.shape[1]
    bb, tt = _row_blocks(b, t, rows)
    grid = (b // bb, t // tt)
    tok = lambda i, j: (i, j, 0)
    per_b = lambda i, j: (i, 0, 0)
    in_specs = [
        pl.BlockSpec((bb, tt, d), tok),
        _resident((1, d)),
        pl.BlockSpec((bb, 1, d), per_b),
        pl.BlockSpec((bb, 1, d), per_b),
        _resident((d, n)),
    ]
    args = [x, gain.reshape(1, d), shift, scale, w]
    out_specs = [pl.BlockSpec((bb, tt, n), tok)]
    out_shape = [jax.ShapeDtypeStruct((b, t, n), BF16)]
    scratch = []
    gates = w_gate is not None
    if gates:
        ng = w_gate.shape[1]
        wg = jnp.zeros((d, GATE_PAD), BF16).at[:, :ng].set(w_gate)
        in_specs += [_resident((d, GATE_PAD)), _resident((ng, d))]
        args += [wg, w_gate.T]
        out_specs += [pl.BlockSpec((bb, tt, GATE_PAD), tok),
                      pl.BlockSpec((bb, ng, tt), lambda i, j: (i, 0, j))]
        out_shape += [jax.ShapeDtypeStruct((b, t, GATE_PAD), F32),
                      jax.ShapeDtypeStruct((b, ng, t), F32)]
    if qk_scale is not None:
        in_specs.append(_resident((1, qk_gain.shape[-1])))
        args.append(qk_gain.reshape(1, -1))
        scratch.append(pltpu.VMEM((bb * tt, n), F32))
    kern = functools.partial(_norm_proj_kernel, n_chunk=min(n, 1024), gates=gates, qk_scale=qk_scale)
    out = pl.pallas_call(
        kern, grid=grid, in_specs=in_specs, out_specs=out_specs, out_shape=out_shape,
        scratch_shapes=scratch, compiler_params=_params("parallel", "parallel"), name="norm_proj",
    )(*args)
    return out if gates else out[0]


def _kv_kernel(*refs, attn_copies):
    x_ref, g_ref, wk_ref, wv_ref, kg_ref = refs[:5]
    if attn_copies:
        wvt_ref, k_ref, v_ref, kb_ref, vt_ref, y_scr = refs[5:]
    else:
        k_ref, v_ref, y_scr = refs[5:]
    x = x_ref[...]
    bb, tt, d = x.shape
    h = _rms(x, g_ref[...]).reshape(bb * tt, d).astype(BF16)
    y_scr[...] = _dot(h, wk_ref[...])
    _group_rms_store(y_scr, k_ref, kg_ref, 1.0, kg_ref.shape[-1])
    v_ref[...] = _dot(h, wv_ref[...]).reshape(bb, tt, d)
    if attn_copies:
        kb_ref[...] = k_ref[...].astype(BF16)
        vt = _dot_nt(wvt_ref[...], h).astype(BF16)
        tile = vt_ref.shape[-1]
        dv = d // A_HEADS
        ext = dv + ONES_ROWS
        ones = jnp.ones((ONES_ROWS, tile), BF16)
        for j in range(vt_ref.shape[1]):
            for hh in range(A_HEADS):
                vt_ref[0, j, hh * ext:hh * ext + dv, :] = vt[hh * dv:(hh + 1) * dv, j * tile:(j + 1) * tile]
                vt_ref[0, j, hh * ext + dv:(hh + 1) * ext, :] = ones


def _shared_kv(x, kv_g, w_k, w_v, k_g, *, rows, attn_tile=None):
    b, t, d = x.shape
    bb, tt = _row_blocks(b, t, rows)
    tok = lambda i, j: (i, j, 0)
    blk = pl.BlockSpec((bb, tt, d), tok)
    in_specs = [blk, _resident((1, d)), _resident((d, d)), _resident((d, d)), _resident((1, k_g.shape[-1]))]
    args = [x, kv_g.reshape(1, d), w_k, w_v, k_g.reshape(1, -1)]
    out_specs = [blk, blk]
    out_shape = [jax.ShapeDtypeStruct((b, t, d), F32), jax.ShapeDtypeStruct((b, t, d), F32)]
    if attn_tile is not None:
        assert bb == 1 and tt % attn_tile == 0
        per = tt // attn_tile
        in_specs.append(_resident((d, d)))
        args.append(w_v.T)
        d_ext = d + A_HEADS * ONES_ROWS
        out_specs += [blk, pl.BlockSpec((1, per, d_ext, attn_tile), lambda i, j: (i, j, 0, 0))]
        out_shape += [jax.ShapeDtypeStruct((b, t, d), BF16),
                      jax.ShapeDtypeStruct((b, t // attn_tile, d_ext, attn_tile), BF16)]
    return pl.pallas_call(
        functools.partial(_kv_kernel, attn_copies=attn_tile is not None),
        grid=(b // bb, t // tt),
        in_specs=in_specs, out_specs=out_specs, out_shape=out_shape,
        scratch_shapes=[pltpu.VMEM((bb * tt, d), F32)],
        compiler_params=_params("parallel", "parallel"),
        name="shared_kv",
    )(*args)


def _proj_residual_kernel(a_ref, w_ref, x_ref, gate_ref, o_ref):
    bb, tt, d = x_ref.shape
    a = a_ref[...].reshape(bb * tt, a_ref.shape[-1])
    y = _dot(a, w_ref[...]).reshape(bb, tt, d)
    o_ref[...] = x_ref[...] + gate_ref[...] * y


def _proj_residual(a, w, x, gate, *, rows):
    b, t, d = x.shape
    k = a.shape[-1]
    bb, tt = _row_blocks(b, t, rows)
    tok = lambda i, j: (i, j, 0)
    return pl.pallas_call(
        _proj_residual_kernel,
        grid=(b // bb, t // tt),
        in_specs=[pl.BlockSpec((bb, tt, k), tok), _resident((k, d)),
                  pl.BlockSpec((bb, tt, d), tok), pl.BlockSpec((bb, 1, d), lambda i, j: (i, 0, 0))],
        out_specs=pl.BlockSpec((bb, tt, d), tok),
        out_shape=jax.ShapeDtypeStruct((b, t, d), F32),
        compiler_params=_params("parallel", "parallel"),
        name="proj_residual",
    )(a, w, x, gate)


def _ffn_kernel(x_ref, g_ref, sh_ref, sc_ref, gate_ref, wg_ref, wu_ref, wd_ref, o_ref, acc_ref):
    bb, tt, d = x_ref.shape
    f = wg_ref.shape[1]
    xm = _modulated(x_ref, g_ref, sh_ref, sc_ref)
    for c in range(f // FFN_CHUNK):
        sl = slice(c * FFN_CHUNK, (c + 1) * FFN_CHUNK)
        hg = _dot(xm, wg_ref[:, sl])
        hu = _dot(xm, wu_ref[:, sl])
        act = (hg * jax.nn.sigmoid(hg) * hu).astype(BF16)
        part = _dot(act, wd_ref[sl, :])
        if c == 0:
            acc_ref[...] = part
        else:
            acc_ref[...] += part
    o_ref[...] = x_ref[...] + gate_ref[...] * acc_ref[...].reshape(bb, tt, d)


def _ffn(x, gain, shift, scale, gate, w_gate, w_up, w_down, *, rows):
    b, t, d = x.shape
    f = w_gate.shape[1]
    assert f % FFN_CHUNK == 0
    bb, tt = _row_blocks(b, t, rows)
    tok = lambda i, j: (i, j, 0)
    per_b = lambda i, j: (i, 0, 0)
    return pl.pallas_call(
        _ffn_kernel,
        grid=(b // bb, t // tt),
        in_specs=[pl.BlockSpec((bb, tt, d), tok), _resident((1, d)),
                  pl.BlockSpec((bb, 1, d), per_b), pl.BlockSpec((bb, 1, d), per_b),
                  pl.BlockSpec((bb, 1, d), per_b),
                  _resident((d, f)), _resident((d, f)), _resident((f, d))],
        out_specs=pl.BlockSpec((bb, tt, d), tok),
        out_shape=jax.ShapeDtypeStruct((b, t, d), F32),
        scratch_shapes=[pltpu.VMEM((bb * tt, d), F32)],
        compiler_params=_params("parallel", "parallel"),
        name="ffn",
    )(x, gain.reshape(1, d), shift, scale, gate, w_gate, w_up, w_down)


def _mlstm_kernel(q_ref, k_ref, v_ref, o_ref, g_ref, gt_ref, brow_ref, bcol_ref, hg_ref,
                  c0_ref, n0_ref, m0_ref, h_ref, c_out, n_out, m_out, c_s, n_s, m_s,
                  gc_s, bc_s, gr_s, br_s):
    ci = pl.program_id(1)
    length = q_ref.shape[1]
    dk = c_s.shape[1]
    dv = c_s.shape[2]

    @pl.when(ci == 0)
    def _():
        c_s[...] = c0_ref[0]
        n_s[...] = n0_ref[0]
        m_s[...] = m0_ref[0]

    row = lax.broadcasted_iota(jnp.int32, (length, length), 0)
    col = lax.broadcasted_iota(jnp.int32, (length, length), 1)
    tri = col <= row
    hi = lax.Precision.HIGHEST
    g = g_ref[0] + brow_ref[...]
    gt = gt_ref[0] + bcol_ref[...]
    gc_s[...] = g
    gr_s[...] = gt
    bc_s[...] = jnp.dot(tri.astype(F32), _log_sigmoid(g), precision=hi, preferred_element_type=F32)
    br_s[...] = jnp.dot(_log_sigmoid(gt), (row <= col).astype(F32), precision=hi,
                        preferred_element_type=F32)

    for h in range(M_HEADS):
        qh = q_ref[0, :, h * dk:(h + 1) * dk]
        kf = k_ref[0, :, h * dk:(h + 1) * dk].astype(F32) * (dk ** -0.5)
        kh = kf.astype(BF16)
        vh = v_ref[0, :, h * dv:(h + 1) * dv]
        bc = bc_s[:, M_HEADS + h:M_HEADS + h + 1]
        br = br_s[M_HEADS + h:M_HEADS + h + 1, :]
        igc = gc_s[:, h:h + 1]
        igr = gr_s[h:h + 1, :]
        b_end = br_s[M_HEADS + h:M_HEADS + h + 1, length - 1:length]
        m_prev = m_s[0:1, h:h + 1]
        c_prev = c_s[h]
        n_prev = n_s[h:h + 1, :]

        dmat = jnp.where(tri, bc - br + igr, -jnp.inf)
        inter = bc + m_prev
        m_t = jnp.maximum(inter, jnp.max(dmat, axis=-1, keepdims=True))
        w_intra = jnp.exp(dmat - m_t)
        w_inter = jnp.exp(inter - m_t)
        s = _dot_nt(qh, kh) * w_intra
        num = w_inter * _dot(qh, c_prev.astype(BF16)) + _dot(s.astype(BF16), vh)
        den = (w_inter * jnp.sum(qh.astype(F32) * n_prev, axis=-1, keepdims=True)
               + jnp.sum(s, axis=-1, keepdims=True))
        hh = num / jnp.maximum(jnp.abs(den), jnp.exp(-m_t))
        hh = _rms(hh, hg_ref[:, h * dv:(h + 1) * dv])
        og = o_ref[0, :, h * dv:(h + 1) * dv].astype(F32)
        h_ref[0, :, h * dv:(h + 1) * dv] = (hh * jax.nn.sigmoid(og)).astype(h_ref.dtype)

        g_row = b_end - br + igr
        g_col = b_end - bc + igc
        m_new = jnp.maximum(b_end + m_prev, jnp.max(g_row, axis=-1, keepdims=True))
        w_old = jnp.exp(b_end + m_prev - m_new)
        kw = kf * jnp.exp(g_col - m_new)
        c_s[h] = w_old * c_prev + _dot_tn(kw.astype(BF16), vh)
        n_s[h:h + 1, :] = w_old * n_prev + jnp.sum(kw, axis=0, keepdims=True)
        m_s[0:1, h:h + 1] = m_new

    @pl.when(ci == pl.num_programs(1) - 1)
    def _():
        c_out[0] = c_s[...]
        n_out[0] = n_s[...]
        m_out[0] = m_s[...]


def _mlstm(qkvo, g, gt, b_gates, head_g, c0, n0, m0, *, chunk):
    b, t, _ = qkvo.shape
    heads, dk, dv = c0.shape[1:]
    d = heads * dv
    qk = heads * dk
    assert qk * 2 == d and t % chunk == 0
    nc = t // chunk
    two_h = 2 * heads
    brow = jnp.zeros((1, GATE_PAD), F32).at[0, :two_h].set(b_gates)
    st4 = lambda i, c: (i, 0, 0, 0)
    st3 = lambda i, c: (i, 0, 0)
    outs = pl.pallas_call(
        _mlstm_kernel,
        grid=(b, nc),
        in_specs=[
            pl.BlockSpec((1, chunk, qk), lambda i, c: (i, c, 0)),
            pl.BlockSpec((1, chunk, qk), lambda i, c: (i, c, 1)),
            pl.BlockSpec((1, chunk, d), lambda i, c: (i, c, 1)),
            pl.BlockSpec((1, chunk, d), lambda i, c: (i, c, 2)),
            pl.BlockSpec((1, chunk, GATE_PAD), lambda i, c: (i, c, 0)),
            pl.BlockSpec((1, two_h, chunk), lambda i, c: (i, 0, c)),
            _resident((1, GATE_PAD)), _resident((two_h, 1)), _resident((1, d)),
            pl.BlockSpec((1, heads, dk, dv), st4),
            pl.BlockSpec((1, heads, dk), st3),
            pl.BlockSpec((1, 1, heads), st3),
        ],
        out_specs=[
            pl.BlockSpec((1, chunk, d), lambda i, c: (i, c, 0)),
            pl.BlockSpec((1, heads, dk, dv), st4),
            pl.BlockSpec((1, heads, dk), st3),
            pl.BlockSpec((1, 1, heads), st3),
        ],
        out_shape=[
            jax.ShapeDtypeStruct((b, t, d), BF16),
            jax.ShapeDtypeStruct((b, heads, dk, dv), F32),
            jax.ShapeDtypeStruct((b, heads, dk), F32),
            jax.ShapeDtypeStruct((b, 1, heads), F32),
        ],
        scratch_shapes=[pltpu.VMEM((heads, dk, dv), F32), pltpu.VMEM((heads, dk), F32),
                        pltpu.VMEM((1, heads), F32),
                        pltpu.VMEM((chunk, GATE_PAD), F32), pltpu.VMEM((chunk, GATE_PAD), F32),
                        pltpu.VMEM((two_h, chunk), F32), pltpu.VMEM((two_h, chunk), F32)],
        compiler_params=_params("parallel", "arbitrary"),
        name="mlstm",
    )(qkvo, qkvo, qkvo, qkvo, g, gt, brow, b_gates.reshape(two_h, 1), head_g.reshape(1, d),
      c0, n0, m0.reshape(b, 1, heads))
    hn, c_new, n_new, m_new = outs
    return hn, c_new, n_new, m_new.reshape(b, heads)


def _rel_bucket(rel):
    nb = N_BUCKETS // 2
    max_exact = nb // 2
    ret = jnp.where(rel > 0, nb, 0)
    n = jnp.abs(rel)
    large = max_exact + (jnp.log(jnp.maximum(n, 1).astype(F32) / max_exact)
                         / math.log(MAX_DISTANCE / max_exact) * (nb - max_exact)).astype(jnp.int32)
    large = jnp.minimum(large, nb - 1)
    return ret + jnp.where(n < max_exact, n, large)


def _bias_tile(rel_bias, q0, k0, tq, tk):
    table = rel_bias.astype(F32)
    span = tq + tk
    r = jnp.arange(span)
    r = jnp.where(r < tk, r, r - span)
    vec = (table[_rel_bucket(r + (k0 - q0))] - table[N_BUCKETS // 2 - 1]).T
    bias = jnp.tile(vec, (1, tq))[:, :tq * (span - 1)].reshape(-1, tq, span - 1)[:, :, :tk]
    q_chunk = (q0 + lax.broadcasted_iota(jnp.int32, (tq, tk), 0)) // CHUNK
    k_chunk = (k0 + lax.broadcasted_iota(jnp.int32, (tq, tk), 1)) // CHUNK
    return jnp.where((k_chunk <= q_chunk)[None], bias * LOG2E, NEG_BIG)


def _flash_update(hc, q, k, v, bias, m_s, l_s, acc_s):
    s = _dot_nt(q, k)
    if bias is not None:
        s = s + bias
    m_prev = m_s[hc]
    m_new = jnp.maximum(m_prev, jnp.max(s, axis=-1, keepdims=True))
    alpha = jnp.exp2(m_prev - m_new)
    p = jnp.exp2(s - m_new)
    l_s[hc] = alpha * l_s[hc] + jnp.sum(p, axis=-1, keepdims=True)
    acc_s[hc] = alpha * acc_s[hc] + _dot(p.astype(BF16), v)
    m_s[hc] = m_new


def _flash_init(m_s, l_s, acc_s):
    m_s[...] = jnp.full(m_s.shape, NEG_BIG, F32)
    l_s[...] = jnp.zeros(l_s.shape, F32)
    acc_s[...] = jnp.zeros(acc_s.shape, F32)


def _flash_tile(q_ref, k_ref, v_ref, bias_fn, m_s, l_s, acc_s):
    dv = acc_s.shape[-1]
    dh = dv // 2
    for h in range(A_HEADS):
        v = v_ref[0, :, h * dv:(h + 1) * dv].astype(BF16)
        bias = bias_fn(h)
        for c in range(2):
            lo = h * dv + c * dh
            q = q_ref[0, :, lo:lo + dh]
            k = k_ref[0, :, lo:lo + dh].astype(BF16)
            _flash_update(2 * h + c, q, k, v, bias, m_s, l_s, acc_s)


def _flash_finish(o_ref, lam_ref, hg_ref, lam_init, m_s, l_s, acc_s):
    dv = acc_s.shape[-1]
    lp = lam_ref[...]
    lam = (jnp.exp(jnp.sum(lp[0:1] * lp[1:2], axis=-1, keepdims=True))
           - jnp.exp(jnp.sum(lp[2:3] * lp[3:4], axis=-1, keepdims=True)) + lam_init)
    for h in range(A_HEADS):
        o = acc_s[2 * h] / l_s[2 * h] - lam * (acc_s[2 * h + 1] / l_s[2 * h + 1])
        o = _rms(o, hg_ref[...]) * (1.0 - lam_init)
        o_ref[0, :, h * dv:(h + 1) * dv] = o.astype(o_ref.dtype)


def _attn_prompt_kernel(q_ref, k_ref, vt_ref, bias_ref, lam_ref, hg_ref, o_ref,
                        m_s, alpha_s, acc_s, s_scr, p_scr, *, lam_init):
    qi = pl.program_id(1)
    tile = q_ref.shape[1]
    ext = acc_s.shape[1]
    dv = o_ref.shape[2] // A_HEADS
    dh = dv // 2
    n_chain = 2 * A_HEADS
    m_s[...] = jnp.full(m_s.shape, NEG_BIG, F32)
    acc_s[...] = jnp.zeros(acc_s.shape, F32)

    def kv_tile(j, slot):
        koff = pl.multiple_of(j * tile, tile)
        for hc in range(n_chain):
            lo = hc * dh
            s_scr[hc] = _dot_nt(k_ref[0, pl.ds(koff, tile), lo:lo + dh], q_ref[0, :, lo:lo + dh])
        for hc in range(n_chain):
            s = s_scr[hc]
            if slot is not None:
                s = s + bias_ref[hc // 2, slot]
                s_scr[hc] = s
            m_prev = m_s[hc]
            m_new = jnp.maximum(m_prev, jnp.max(s, axis=0, keepdims=True))
            alpha_s[hc] = jnp.exp2(m_prev - m_new)
            m_s[hc] = m_new
        for hc in range(n_chain):
            p_scr[hc] = jnp.exp2(s_scr[hc] - m_s[hc]).astype(BF16)
        for hc in range(n_chain):
            h = hc // 2
            acc_s[hc] = alpha_s[hc] * acc_s[hc] + _dot(vt_ref[0, j, h * ext:(h + 1) * ext, :], p_scr[hc])

    def far_body(j, carry):
        kv_tile(j, None)
        return carry

    lax.fori_loop(0, jnp.maximum(qi - 1, 0), far_body, 0)

    @pl.when(qi > 0)
    def _():
        kv_tile(qi - 1, 0)

    kv_tile(qi, 1)

    lp = lam_ref[...]
    lam = (jnp.exp(jnp.sum(lp[0:1] * lp[1:2], axis=-1, keepdims=True))
           - jnp.exp(jnp.sum(lp[2:3] * lp[3:4], axis=-1, keepdims=True)) + lam_init)
    for h in range(A_HEADS):
        o = (acc_s[2 * h, :dv] / acc_s[2 * h, dv:dv + 1]
             - lam * (acc_s[2 * h + 1, :dv] / acc_s[2 * h + 1, dv:dv + 1]))
        o = o * lax.rsqrt(jnp.mean(o * o, axis=0, keepdims=True) + EPS) * hg_ref[...] * (1.0 - lam_init)
        o_ref[0, :, h * dv:(h + 1) * dv] = o.T.astype(o_ref.dtype)


def _attn_prompt(q, k, vt, rel_bias, lam_p, head_g, lam_init):
    b, t, d = q.shape
    tile = ATTN_TILE
    assert t % tile == 0 and tile % CHUNK == 0 and tile >= MAX_DISTANCE and vt.shape[-1] == tile
    n = t // tile
    dv = d // A_HEADS
    ext = vt.shape[2] // A_HEADS
    n_chain = 2 * A_HEADS
    bias = jnp.stack([_bias_tile(rel_bias, tile, 0, tile, tile), _bias_tile(rel_bias, 0, 0, tile, tile)],
                     axis=1).swapaxes(-1, -2)
    q_spec = pl.BlockSpec((1, tile, d), lambda i, qi: (i, qi, 0))
    return pl.pallas_call(
        functools.partial(_attn_prompt_kernel, lam_init=lam_init),
        grid=(b, n),
        in_specs=[q_spec,
                  pl.BlockSpec((1, t, d), lambda i, qi: (i, 0, 0), pipeline_mode=pl.Buffered(1)),
                  pl.BlockSpec((1,) + vt.shape[1:], lambda i, qi: (i, 0, 0, 0), pipeline_mode=pl.Buffered(1)),
                  _resident(bias.shape), _resident(lam_p.shape), _resident((dv, 1))],
        out_specs=q_spec,
        out_shape=jax.ShapeDtypeStruct((b, t, d), BF16),
        scratch_shapes=[pltpu.VMEM((n_chain, 1, tile), F32), pltpu.VMEM((n_chain, 1, tile), F32),
                        pltpu.VMEM((n_chain, ext, tile), F32),
                        pltpu.VMEM((n_chain, tile, tile), F32), pltpu.VMEM((n_chain, tile, tile), BF16)],
        compiler_params=_params("parallel", "arbitrary"),
        name="attn_prompt",
    )(q, k, vt, bias, lam_p, head_g.reshape(dv, 1))


def _attn_sample_kernel(q_ref, ck_ref, cv_ref, nk_ref, nv_ref, bias_c_ref, bias_n_ref, lam_ref, hg_ref,
                        o_ref, m_s, l_s, acc_s, *, lam_init):
    j = pl.program_id(1)
    n_cache = pl.num_programs(1) - 1

    @pl.when(j == 0)
    def _():
        _flash_init(m_s, l_s, acc_s)

    @pl.when(j < n_cache - 1)
    def _():
        _flash_tile(q_ref, ck_ref, cv_ref, lambda h: None, m_s, l_s, acc_s)

    @pl.when(j == n_cache - 1)
    def _():
        _flash_tile(q_ref, ck_ref, cv_ref, lambda h: bias_c_ref[h], m_s, l_s, acc_s)

    @pl.when(j == n_cache)
    def _():
        _flash_tile(q_ref, nk_ref, nv_ref, lambda h: bias_n_ref[h], m_s, l_s, acc_s)
        _flash_finish(o_ref, lam_ref, hg_ref, lam_init, m_s, l_s, acc_s)


def _attn_sample(q, cache_k, cache_v, new_k, new_v, rel_bias, lam_p, head_g, lam_init):
    b, t, d = q.shape
    past = cache_k.shape[1]
    tile = ATTN_TILE
    assert past % tile == 0 and past % CHUNK == 0 and tile >= MAX_DISTANCE
    n_cache = past // tile
    dv = d // A_HEADS
    bias_c = _bias_tile(rel_bias, past, past - tile, t, tile)
    bias_n = _bias_tile(rel_bias, past, past, t, t)
    last = n_cache - 1
    cache_spec = pl.BlockSpec((1, tile, d), lambda i, j: (i, jnp.minimum(j, last), 0))
    tok_spec = pl.BlockSpec((1, t, d), lambda i, j: (i, 0, 0))
    return pl.pallas_call(
        functools.partial(_attn_sample_kernel, lam_init=lam_init),
        grid=(b, n_cache + 1),
        in_specs=[tok_spec, cache_spec, cache_spec, tok_spec, tok_spec,
                  _resident(bias_c.shape), _resident(bias_n.shape), _resident(lam_p.shape),
                  _resident((1, dv))],
        out_specs=tok_spec,
        out_shape=jax.ShapeDtypeStruct((b, t, d), BF16),
        scratch_shapes=[pltpu.VMEM((2 * A_HEADS, t, 1), F32), pltpu.VMEM((2 * A_HEADS, t, 1), F32),
                        pltpu.VMEM((2 * A_HEADS, t, dv), F32)],
        compiler_params=_params("parallel", "arbitrary"),
        name="attn_sample",
    )(q, cache_k.reshape(b, past, d), cache_v.reshape(b, past, d), new_k, new_v,
      bias_c, bias_n, lam_p, head_g.reshape(1, dv))


def _trunk(x, mod, mstate, past_k, past_v, chunk, rows, p):
    b, t, d = x.shape
    depth = p["norm_g"].shape[0]
    n_a = p["mlstm_w_in"].shape[0]
    qk = M_HEADS * (d // M_HEADS // 2)
    new_c, new_n, new_m = [], [], []
    prompt = past_k is None
    k_f32 = v_f32 = k_bf = vt_bf = None
    for l in range(depth):
        sh1, sc1, g1, sh2, sc2, g2 = [mod[l, :, i * d:(i + 1) * d].reshape(b, 1, d) for i in range(6)]
        if l < n_a:
            w_in = p["mlstm_w_in"][l]
            n_main = 2 * qk + 2 * d
            qkvo, g, gt = _norm_proj(x, p["norm_g"][l, 0], sh1, sc1, w_in[:, :n_main], rows=rows,
                                     w_gate=w_in[:, n_main:])
            hn, c_l, n_l, m_l = _mlstm(qkvo, g, gt, p["mlstm_b_gates"][l], p["mlstm_head_g"][l],
                                       mstate[0][l], mstate[1][l], mstate[2][l], chunk=chunk)
            new_c.append(c_l)
            new_n.append(n_l)
            new_m.append(m_l)
            x = _proj_residual(hn, p["mlstm_w_out"][l], x, g1, rows=rows)
        else:
            j = l - n_a
            lam_init = 0.8 - 0.6 * math.exp(-0.3 * l)
            dh = p["q_norm_g"].shape[-1]
            q = _norm_proj(x, p["norm_g"][l, 0], sh1, sc1, p["attn_w_q"][j], rows=rows,
                           qk_gain=p["q_norm_g"][j], qk_scale=dh ** -0.5 * LOG2E)
            if prompt:
                o = _attn_prompt(q, k_bf, vt_bf, p["rel_bias"], p["attn_lambda"][j], p["attn_head_g"][j],
                                 lam_init)
            else:
                o = _attn_sample(q, past_k, past_v, k_f32, v_f32, p["rel_bias"], p["attn_lambda"][j],
                                 p["attn_head_g"][j], lam_init)
            x = _proj_residual(o, p["attn_w_o"][j], x, g1, rows=rows)
        x = _ffn(x, p["norm_g"][l, 1], sh2, sc2, g2, p["ffn_w_gate"][l], p["ffn_w_up"][l],
                 p["ffn_w_down"][l], rows=rows)
        if l == n_a - 1:
            kv = _shared_kv(x, p["kv_norm_g"], p["w_k"], p["w_v"], p["k_norm_g"], rows=rows,
                            attn_tile=ATTN_TILE if prompt else None)
            if prompt:
                k_f32, v_f32, k_bf, vt_bf = kv
            else:
                k_f32, v_f32 = kv
    dh = p["k_norm_g"].shape[-1]
    return (x, jnp.stack(new_c), jnp.stack(new_n), jnp.stack(new_m),
            k_f32.reshape(b, t, A_HEADS, 2, dh), v_f32.reshape(b, t, A_HEADS, 2 * dh))


def kernel(x_prompt, x_sample, c_prompt, c_sample, state_mlstm_C, state_mlstm_n, state_mlstm_m, cache_k, cache_v, ada_w, ada_b, norm_g, mlstm_w_in, mlstm_b_gates, mlstm_head_g, mlstm_w_out, kv_norm_g, w_k, w_v, k_norm_g, attn_w_q, q_norm_g, attn_lambda, attn_head_g, attn_w_o, rel_bias, ffn_w_gate, ffn_w_up, ffn_w_down):
    bf = lambda w: w.astype(BF16)
    p = dict(norm_g=norm_g, mlstm_w_in=bf(mlstm_w_in), mlstm_b_gates=mlstm_b_gates,
             mlstm_head_g=mlstm_head_g, mlstm_w_out=bf(mlstm_w_out), kv_norm_g=kv_norm_g, w_k=bf(w_k),
             w_v=bf(w_v), k_norm_g=k_norm_g, attn_w_q=bf(attn_w_q), q_norm_g=q_norm_g,
             attn_lambda=attn_lambda, attn_head_g=attn_head_g, attn_w_o=bf(attn_w_o), rel_bias=rel_bias,
             ffn_w_gate=bf(ffn_w_gate), ffn_w_up=bf(ffn_w_up), ffn_w_down=bf(ffn_w_down))
    bp = x_prompt.shape[0]
    n_a, _, heads, dk, dv = state_mlstm_C.shape
    mod = _adaln(jnp.concatenate([c_prompt, c_sample], axis=0), ada_w, ada_b)
    zero_state = (jnp.zeros((n_a, bp, heads, dk, dv), F32), jnp.zeros((n_a, bp, heads, dk), F32),
                  jnp.zeros((n_a, bp, heads), F32))
    y_p, c_p, n_p, m_p, k_p, v_p = _trunk(x_prompt, mod[:, :bp], zero_state, None, None,
                                          MLSTM_PROMPT_CHUNK, 512, p)
    y_s, c_s, n_s, m_s, k_s, v_s = _trunk(x_sample, mod[:, bp:], (state_mlstm_C, state_mlstm_n, state_mlstm_m),
                                          cache_k, cache_v, x_sample.shape[1], 512, p)
    return (y_p, y_s, c_p, n_p, m_p, k_p, v_p, c_s, n_s, m_s, k_s, v_s)
```

```python
import functools
import math

import jax
import jax.numpy as jnp
from jax import lax
from jax.experimental import pallas as pl
from jax.experimental.pallas import tpu as pltpu

F32 = jnp.float32
BF16 = jnp.bfloat16

EPS = 1e-6
NEG_BIG = -1e30
CHUNK = 64
N_BUCKETS = 32
MAX_DISTANCE = 128
M_HEADS = 8
A_HEADS = 8
GATE_PAD = 128
ONES_ROWS = 16
LOG2E = math.log2(math.e)

V7X_VMEM_BYTES = 64 * 1024 * 1024
VMEM_LIMIT = V7X_VMEM_BYTES - 8 * 1024 * 1024

ATTN_TILE = 256
MLSTM_PROMPT_CHUNK = 128
FFN_CHUNK = 256


def _params(*sem, flags=None):
    return pltpu.CompilerParams(dimension_semantics=sem, vmem_limit_bytes=VMEM_LIMIT, flags=flags)


def _rms(x, g):
    return x * lax.rsqrt(jnp.mean(x * x, axis=-1, keepdims=True) + EPS) * g


def _log_sigmoid(x):
    return jnp.minimum(x, 0.0) - jnp.log1p(jnp.exp(-jnp.abs(x)))


def _dot(a, b):
    return jnp.dot(a, b, preferred_element_type=F32)


def _dot_nt(a, b):
    return lax.dot_general(a, b, (((1,), (1,)), ((), ())), preferred_element_type=F32)


def _dot_tn(a, b):
    return lax.dot_general(a, b, (((0,), (0,)), ((), ())), preferred_element_type=F32)


def _resident(shape):
    nd = len(shape)
    return pl.BlockSpec(shape, lambda *_: (0,) * nd, pipeline_mode=pl.Buffered(1))


def _adaln_kernel(c_ref, w_ref, b_ref, o_ref):
    c = c_ref[...]
    a = (c * jax.nn.sigmoid(c)).astype(BF16)
    o_ref[0] = _dot(a, w_ref[0].astype(BF16)) + b_ref[0]


def _adaln(c, ada_w, ada_b):
    depth, d, n = ada_w.shape
    rows = c.shape[0]
    tn = 1536
    assert n % tn == 0
    return pl.pallas_call(
        _adaln_kernel,
        grid=(depth, n // tn),
        in_specs=[
            pl.BlockSpec((rows, d), lambda l, j: (0, 0)),
            pl.BlockSpec((1, d, tn), lambda l, j: (l, 0, j)),
            pl.BlockSpec((1, 1, tn), lambda l, j: (l, 0, j)),
        ],
        out_specs=pl.BlockSpec((1, rows, tn), lambda l, j: (l, 0, j)),
        out_shape=jax.ShapeDtypeStruct((depth, rows, n), F32),
        compiler_params=_params("parallel", "parallel"),
        name="adaln",
    )(c, ada_w, ada_b.reshape(depth, 1, n))


def _modulated(x_ref, g_ref, sh_ref, sc_ref):
    x = x_ref[...]
    bb, tt, d = x.shape
    xm = _rms(x, g_ref[...]) * (1.0 + sc_ref[...]) + sh_ref[...]
    return xm.reshape(bb * tt, d).astype(BF16)


def _group_rms_store(src_ref, dst_ref, gain_ref, post_scale, width):
    bb, tt, n = dst_ref.shape
    for g in range(n // width):
        seg = src_ref[:, g * width:(g + 1) * width]
        y = _rms(seg, gain_ref[...]) * post_scale
        dst_ref[:, :, g * width:(g + 1) * width] = y.reshape(bb, tt, width).astype(dst_ref.dtype)


def _norm_proj_kernel(*refs, n_chunk, gates, qk_scale):
    x_ref, g_ref, sh_ref, sc_ref, w_ref = refs[:5]
    pos = 5
    if gates:
        wg_ref, wgt_ref = refs[pos:pos + 2]
        pos += 2
    if qk_scale is not None:
        qg_ref = refs[pos]
        pos += 1
    o_ref = refs[pos]
    pos += 1
    if gates:
        og_ref, ogt_ref = refs[pos:pos + 2]
        pos += 2
    if qk_scale is not None:
        y_scr = refs[pos]

    bb, tt, n = o_ref.shape
    xm = _modulated(x_ref, g_ref, sh_ref, sc_ref)
    for c in range(n // n_chunk):
        y = _dot(xm, w_ref[:, c * n_chunk:(c + 1) * n_chunk])
        if qk_scale is None:
            o_ref[:, :, c * n_chunk:(c + 1) * n_chunk] = y.reshape(bb, tt, n_chunk).astype(o_ref.dtype)
        else:
            y_scr[:, c * n_chunk:(c + 1) * n_chunk] = y
    if qk_scale is not None:
        _group_rms_store(y_scr, o_ref, qg_ref, qk_scale, qg_ref.shape[-1])
    if gates:
        og_ref[...] = _dot(xm, wg_ref[...]).reshape(bb, tt, GATE_PAD)
        for i in range(bb):
            ogt_ref[i] = _dot_nt(wgt_ref[...], xm[i * tt:(i + 1) * tt])


def _row_blocks(b, t, rows):
    if t >= rows:
        assert t % rows == 0
        return 1, rows
    assert rows % t == 0 and b % (rows // t) == 0
    return rows // t, t


def _norm_proj(x, gain, shift, scale, w, *, rows, w_gate=None, qk_gain=None, qk_scale=None):
    b, t, d = x.shape
    n = w.shape[1]
    bb, tt = _row_blocks(b, t, rows)
    grid = (b // bb, t // tt)
    tok = lambda i, j: (i, j, 0)
    per_b = lambda i, j: (i, 0, 0)
    in_specs = [
        pl.BlockSpec((bb, tt, d), tok),
        _resident((1, d)),
        pl.BlockSpec((bb, 1, d), per_b),
        pl.BlockSpec((bb, 1, d), per_b),
        _resident((d, n)),
    ]
    args = [x, gain.reshape(1, d), shift, scale, w]
    out_specs = [pl.BlockSpec((bb, tt, n), tok)]
    out_shape = [jax.ShapeDtypeStruct((b, t, n), BF16)]
    scratch = []
    gates = w_gate is not None
    if gates:
        ng = w_gate.shape[1]
        wg = jnp.zeros((d, GATE_PAD), BF16).at[:, :ng].set(w_gate)
        in_specs += [_resident((d, GATE_PAD)), _resident((ng, d))]
        args += [wg, w_gate.T]
        out_specs += [pl.BlockSpec((bb, tt, GATE_PAD), tok),
                      pl.BlockSpec((bb, ng, tt), lambda i, j: (i, 0, j))]
        out_shape += [jax.ShapeDtypeStruct((b, t, GATE_PAD), F32),
                      jax.ShapeDtypeStruct((b, ng, t), F32)]
    if qk_scale is not None:
        in_specs.append(_resident((1, qk_gain.shape[-1])))
        args.append(qk_gain.reshape(1, -1))
        scratch.append(pltpu.VMEM((bb * tt, n), F32))
    kern = functools.partial(_norm_proj_kernel, n_chunk=min(n, 1024), gates=gates, qk_scale=qk_scale)
    out = pl.pallas_call(
        kern, grid=grid, in_specs=in_specs, out_specs=out_specs, out_shape=out_shape,
        scratch_shapes=scratch, compiler_params=_params("parallel", "parallel"), name="norm_proj",
    )(*args)
    return out if gates else out[0]


def _kv_kernel(x_ref, g_ref, wk_ref, wv_ref, wvt_ref, kg_ref, k_ref, v_ref, kb_ref, vt_ref, y_scr):
    x = x_ref[...]
    bb, tt, d = x.shape
    h = _rms(x, g_ref[...]).reshape(bb * tt, d).astype(BF16)
    y_scr[...] = _dot(h, wk_ref[...])
    _group_rms_store(y_scr, k_ref, kg_ref, 1.0, kg_ref.shape[-1])
    v_ref[...] = _dot(h, wv_ref[...]).reshape(bb, tt, d)
    kb_ref[...] = k_ref[...].astype(BF16)
    vt = _dot_nt(wvt_ref[...], h).astype(BF16)
    per, tile = vt_ref.shape[1], vt_ref.shape[3]
    dv = d // A_HEADS
    ext = dv + ONES_ROWS
    ones = jnp.ones((ONES_ROWS, tile), BF16)
    for i in range(bb):
        for j in range(per):
            col = (i * per + j) * tile
            for hh in range(A_HEADS):
                vt_ref[i, j, hh * ext:hh * ext + dv, :] = vt[hh * dv:(hh + 1) * dv, col:col + tile]
                vt_ref[i, j, hh * ext + dv:(hh + 1) * ext, :] = ones


def _shared_kv(x, kv_g, w_k, w_v, k_g, *, rows, attn_tile):
    b, t, d = x.shape
    bb, tt = _row_blocks(b, t, rows)
    assert tt % attn_tile == 0
    per = tt // attn_tile
    d_ext = d + A_HEADS * ONES_ROWS
    tok = lambda i, j: (i, j, 0)
    blk = pl.BlockSpec((bb, tt, d), tok)
    return pl.pallas_call(
        _kv_kernel,
        grid=(b // bb, t // tt),
        in_specs=[blk, _resident((1, d)), _resident((d, d)), _resident((d, d)), _resident((d, d)),
                  _resident((1, k_g.shape[-1]))],
        out_specs=[blk, blk, blk, pl.BlockSpec((bb, per, d_ext, attn_tile), lambda i, j: (i, j, 0, 0))],
        out_shape=[jax.ShapeDtypeStruct((b, t, d), F32), jax.ShapeDtypeStruct((b, t, d), F32),
                   jax.ShapeDtypeStruct((b, t, d), BF16),
                   jax.ShapeDtypeStruct((b, t // attn_tile, d_ext, attn_tile), BF16)],
        scratch_shapes=[pltpu.VMEM((bb * tt, d), F32)],
        compiler_params=_params("parallel", "parallel"),
        name="shared_kv",
    )(x, kv_g.reshape(1, d), w_k, w_v, w_v.T, k_g.reshape(1, -1))


def _proj_residual_kernel(a_ref, w_ref, x_ref, gate_ref, o_ref):
    bb, tt, d = x_ref.shape
    a = a_ref[...].reshape(bb * tt, a_ref.shape[-1])
    y = _dot(a, w_ref[...]).reshape(bb, tt, d)
    o_ref[...] = x_ref[...] + gate_ref[...] * y


def _proj_residual(a, w, x, gate, *, rows):
    b, t, d = x.shape
    k = a.shape[-1]
    bb, tt = _row_blocks(b, t, rows)
    tok = lambda i, j: (i, j, 0)
    return pl.pallas_call(
        _proj_residual_kernel,
        grid=(b // bb, t // tt),
        in_specs=[pl.BlockSpec((bb, tt, k), tok), _resident((k, d)),
                  pl.BlockSpec((bb, tt, d), tok), pl.BlockSpec((bb, 1, d), lambda i, j: (i, 0, 0))],
        out_specs=pl.BlockSpec((bb, tt, d), tok),
        out_shape=jax.ShapeDtypeStruct((b, t, d), F32),
        compiler_params=_params("parallel", "parallel"),
        name="proj_residual",
    )(a, w, x, gate)


def _ffn_kernel(x_ref, g_ref, sh_ref, sc_ref, gate_ref, wg_ref, wu_ref, wd_ref, o_ref, acc_ref):
    bb, tt, d = x_ref.shape
    f = wg_ref.shape[1]
    xm = _modulated(x_ref, g_ref, sh_ref, sc_ref)
    for c in range(f // FFN_CHUNK):
        sl = slice(c * FFN_CHUNK, (c + 1) * FFN_CHUNK)
        hg = _dot(xm, wg_ref[:, sl])
        hu = _dot(xm, wu_ref[:, sl])
        act = (hg * jax.nn.sigmoid(hg) * hu).astype(BF16)
        part = _dot(act, wd_ref[sl, :])
        if c == 0:
            acc_ref[...] = part
        else:
            acc_ref[...] += part
    o_ref[...] = x_ref[...] + gate_ref[...] * acc_ref[...].reshape(bb, tt, d)


def _ffn(x, gain, shift, scale, gate, w_gate, w_up, w_down, *, rows):
    b, t, d = x.shape
    f = w_gate.shape[1]
    assert f % FFN_CHUNK == 0
    bb, tt = _row_blocks(b, t, rows)
    tok = lambda i, j: (i, j, 0)
    per_b = lambda i, j: (i, 0, 0)
    return pl.pallas_call(
        _ffn_kernel,
        grid=(b // bb, t // tt),
        in_specs=[pl.BlockSpec((bb, tt, d), tok), _resident((1, d)),
                  pl.BlockSpec((bb, 1, d), per_b), pl.BlockSpec((bb, 1, d), per_b),
                  pl.BlockSpec((bb, 1, d), per_b),
                  _resident((d, f)), _resident((d, f)), _resident((f, d))],
        out_specs=pl.BlockSpec((bb, tt, d), tok),
        out_shape=jax.ShapeDtypeStruct((b, t, d), F32),
        scratch_shapes=[pltpu.VMEM((bb * tt, d), F32)],
        compiler_params=_params("parallel", "parallel"),
        name="ffn",
    )(x, gain.reshape(1, d), shift, scale, gate, w_gate, w_up, w_down)


def _mlstm_kernel(q_ref, k_ref, v_ref, o_ref, g_ref, gt_ref, brow_ref, bcol_ref, hg_ref,
                  c0_ref, n0_ref, m0_ref, h_ref, c_out, n_out, m_out, c_s, n_s, m_s,
                  gc_s, bc_s, gr_s, br_s):
    ci = pl.program_id(1)
    length = q_ref.shape[1]
    dk = c_s.shape[1]
    dv = c_s.shape[2]

    @pl.when(ci == 0)
    def _():
        c_s[...] = c0_ref[0]
        n_s[...] = n0_ref[0]
        m_s[...] = m0_ref[0]

    row = lax.broadcasted_iota(jnp.int32, (length, length), 0)
    col = lax.broadcasted_iota(jnp.int32, (length, length), 1)
    tri = col <= row
    hi = lax.Precision.HIGHEST
    g = g_ref[0] + brow_ref[...]
    gt = gt_ref[0] + bcol_ref[...]
    gc_s[...] = g
    gr_s[...] = gt
    bc_s[...] = jnp.dot(tri.astype(F32), _log_sigmoid(g), precision=hi, preferred_element_type=F32)
    br_s[...] = jnp.dot(_log_sigmoid(gt), (row <= col).astype(F32), precision=hi,
                        preferred_element_type=F32)

    for h in range(M_HEADS):
        qh = q_ref[0, :, h * dk:(h + 1) * dk]
        kf = k_ref[0, :, h * dk:(h + 1) * dk].astype(F32) * (dk ** -0.5)
        kh = kf.astype(BF16)
        vh = v_ref[0, :, h * dv:(h + 1) * dv]
        bc = bc_s[:, M_HEADS + h:M_HEADS + h + 1]
        br = br_s[M_HEADS + h:M_HEADS + h + 1, :]
        igc = gc_s[:, h:h + 1]
        igr = gr_s[h:h + 1, :]
        b_end = br_s[M_HEADS + h:M_HEADS + h + 1, length - 1:length]
        m_prev = m_s[0:1, h:h + 1]
        c_prev = c_s[h]
        n_prev = n_s[h:h + 1, :]

        dmat = jnp.where(tri, bc - br + igr, -jnp.inf)
        inter = bc + m_prev
        m_t = jnp.maximum(inter, jnp.max(dmat, axis=-1, keepdims=True))
        w_intra = jnp.exp(dmat - m_t)
        w_inter = jnp.exp(inter - m_t)
        s = _dot_nt(qh, kh) * w_intra
        num = w_inter * _dot(qh, c_prev.astype(BF16)) + _dot(s.astype(BF16), vh)
        den = (w_inter * jnp.sum(qh.astype(F32) * n_prev, axis=-1, keepdims=True)
               + jnp.sum(s, axis=-1, keepdims=True))
        hh = num / jnp.maximum(jnp.abs(den), jnp.exp(-m_t))
        hh = _rms(hh, hg_ref[:, h * dv:(h + 1) * dv])
        og = o_ref[0, :, h * dv:(h + 1) * dv].astype(F32)
        h_ref[0, :, h * dv:(h + 1) * dv] = (hh * jax.nn.sigmoid(og)).astype(h_ref.dtype)

        g_row = b_end - br + igr
        g_col = b_end - bc + igc
        m_new = jnp.maximum(b_end + m_prev, jnp.max(g_row, axis=-1, keepdims=True))
        w_old = jnp.exp(b_end + m_prev - m_new)
        kw = kf * jnp.exp(g_col - m_new)
        c_s[h] = w_old * c_prev + _dot_tn(kw.astype(BF16), vh)
        n_s[h:h + 1, :] = w_old * n_prev + jnp.sum(kw, axis=0, keepdims=True)
        m_s[0:1, h:h + 1] = m_new

    @pl.when(ci == pl.num_programs(1) - 1)
    def _():
        c_out[0] = c_s[...]
        n_out[0] = n_s[...]
        m_out[0] = m_s[...]


def _mlstm(qkvo, g, gt, b_gates, head_g, c0, n0, m0, *, chunk):
    b, t, _ = qkvo.shape
    heads, dk, dv = c0.shape[1:]
    d = heads * dv
    qk = heads * dk
    assert qk * 2 == d and t % chunk == 0
    nc = t // chunk
    two_h = 2 * heads
    brow = jnp.zeros((1, GATE_PAD), F32).at[0, :two_h].set(b_gates)
    st4 = lambda i, c: (i, 0, 0, 0)
    st3 = lambda i, c: (i, 0, 0)
    outs = pl.pallas_call(
        _mlstm_kernel,
        grid=(b, nc),
        in_specs=[
            pl.BlockSpec((1, chunk, qk), lambda i, c: (i, c, 0)),
            pl.BlockSpec((1, chunk, qk), lambda i, c: (i, c, 1)),
            pl.BlockSpec((1, chunk, d), lambda i, c: (i, c, 1)),
            pl.BlockSpec((1, chunk, d), lambda i, c: (i, c, 2)),
            pl.BlockSpec((1, chunk, GATE_PAD), lambda i, c: (i, c, 0)),
            pl.BlockSpec((1, two_h, chunk), lambda i, c: (i, 0, c)),
            _resident((1, GATE_PAD)), _resident((two_h, 1)), _resident((1, d)),
            pl.BlockSpec((1, heads, dk, dv), st4),
            pl.BlockSpec((1, heads, dk), st3),
            pl.BlockSpec((1, 1, heads), st3),
        ],
        out_specs=[
            pl.BlockSpec((1, chunk, d), lambda i, c: (i, c, 0)),
            pl.BlockSpec((1, heads, dk, dv), st4),
            pl.BlockSpec((1, heads, dk), st3),
            pl.BlockSpec((1, 1, heads), st3),
        ],
        out_shape=[
            jax.ShapeDtypeStruct((b, t, d), BF16),
            jax.ShapeDtypeStruct((b, heads, dk, dv), F32),
            jax.ShapeDtypeStruct((b, heads, dk), F32),
            jax.ShapeDtypeStruct((b, 1, heads), F32),
        ],
        scratch_shapes=[pltpu.VMEM((heads, dk, dv), F32), pltpu.VMEM((heads, dk), F32),
                        pltpu.VMEM((1, heads), F32),
                        pltpu.VMEM((chunk, GATE_PAD), F32), pltpu.VMEM((chunk, GATE_PAD), F32),
                        pltpu.VMEM((two_h, chunk), F32), pltpu.VMEM((two_h, chunk), F32)],
        compiler_params=_params("parallel", "arbitrary"),
        name="mlstm",
    )(qkvo, qkvo, qkvo, qkvo, g, gt, brow, b_gates.reshape(two_h, 1), head_g.reshape(1, d),
      c0, n0, m0.reshape(b, 1, heads))
    hn, c_new, n_new, m_new = outs
    return hn, c_new, n_new, m_new.reshape(b, heads)


def _rel_bucket(rel):
    nb = N_BUCKETS // 2
    max_exact = nb // 2
    ret = jnp.where(rel > 0, nb, 0)
    n = jnp.abs(rel)
    large = max_exact + (jnp.log(jnp.maximum(n, 1).astype(F32) / max_exact)
                         / math.log(MAX_DISTANCE / max_exact) * (nb - max_exact)).astype(jnp.int32)
    large = jnp.minimum(large, nb - 1)
    return ret + jnp.where(n < max_exact, n, large)


def _bias_tile(rel_bias, q0, k0, tq, tk):
    table = rel_bias.astype(F32)
    span = tq + tk
    r = jnp.arange(span)
    r = jnp.where(r < tk, r, r - span)
    vec = (table[_rel_bucket(r + (k0 - q0))] - table[N_BUCKETS // 2 - 1]).T
    bias = jnp.tile(vec, (1, tq))[:, :tq * (span - 1)].reshape(-1, tq, span - 1)[:, :, :tk]
    q_chunk = (q0 + lax.broadcasted_iota(jnp.int32, (tq, tk), 0)) // CHUNK
    k_chunk = (k0 + lax.broadcasted_iota(jnp.int32, (tq, tk), 1)) // CHUNK
    return jnp.where((k_chunk <= q_chunk)[None], bias * LOG2E, NEG_BIG)


def _attn_prompt_kernel(q_ref, k_ref, vt_ref, bias_ref, lam_ref, hg_ref, o_ref,
                        m_s, alpha_s, acc_s, s_scr, p_scr, *, lam_init):
    qi = pl.program_id(1)
    tile = q_ref.shape[1]
    ext = acc_s.shape[1]
    dv = o_ref.shape[2] // A_HEADS
    dh = dv // 2
    n_chain = 2 * A_HEADS
    m_s[...] = jnp.full(m_s.shape, NEG_BIG, F32)
    acc_s[...] = jnp.zeros(acc_s.shape, F32)

    def kv_tile(j, slot):
        koff = pl.multiple_of(j * tile, tile)
        for hc in range(n_chain):
            lo = hc * dh
            s_scr[hc] = _dot_nt(k_ref[0, pl.ds(koff, tile), lo:lo + dh], q_ref[0, :, lo:lo + dh])
        for hc in range(n_chain):
            s = s_scr[hc]
            if slot is not None:
                s = s + bias_ref[hc // 2, slot]
                s_scr[hc] = s
            m_prev = m_s[hc]
            m_new = jnp.maximum(m_prev, jnp.max(s, axis=0, keepdims=True))
            alpha_s[hc] = jnp.exp2(m_prev - m_new)
            m_s[hc] = m_new
        for hc in range(n_chain):
            p_scr[hc] = jnp.exp2(s_scr[hc] - m_s[hc]).astype(BF16)
        for hc in range(n_chain):
            h = hc // 2
            acc_s[hc] = alpha_s[hc] * acc_s[hc] + _dot(vt_ref[0, j, h * ext:(h + 1) * ext, :], p_scr[hc])

    def far_body(j, carry):
        kv_tile(j, None)
        return carry

    lax.fori_loop(0, jnp.maximum(qi - 1, 0), far_body, 0)

    @pl.when(qi > 0)
    def _():
        kv_tile(qi - 1, 0)

    kv_tile(qi, 1)

    lp = lam_ref[...]
    lam = (jnp.exp(jnp.sum(lp[0:1] * lp[1:2], axis=-1, keepdims=True))
           - jnp.exp(jnp.sum(lp[2:3] * lp[3:4], axis=-1, keepdims=True)) + lam_init)
    for h in range(A_HEADS):
        o = (acc_s[2 * h, :dv] / acc_s[2 * h, dv:dv + 1]
             - lam * (acc_s[2 * h + 1, :dv] / acc_s[2 * h + 1, dv:dv + 1]))
        o = o * lax.rsqrt(jnp.mean(o * o, axis=0, keepdims=True) + EPS) * hg_ref[...] * (1.0 - lam_init)
        o_ref[0, :, h * dv:(h + 1) * dv] = o.T.astype(o_ref.dtype)


def _attn_prompt(q, k, vt, rel_bias, lam_p, head_g, lam_init):
    b, t, d = q.shape
    tile = ATTN_TILE
    assert t % tile == 0 and tile % CHUNK == 0 and tile >= MAX_DISTANCE and vt.shape[-1] == tile
    n = t // tile
    dv = d // A_HEADS
    ext = vt.shape[2] // A_HEADS
    n_chain = 2 * A_HEADS
    bias = jnp.stack([_bias_tile(rel_bias, tile, 0, tile, tile), _bias_tile(rel_bias, 0, 0, tile, tile)],
                     axis=1).swapaxes(-1, -2)
    q_spec = pl.BlockSpec((1, tile, d), lambda i, qi: (i, qi, 0))
    return pl.pallas_call(
        functools.partial(_attn_prompt_kernel, lam_init=lam_init),
        grid=(b, n),
        in_specs=[q_spec,
                  pl.BlockSpec((1, t, d), lambda i, qi: (i, 0, 0), pipeline_mode=pl.Buffered(1)),
                  pl.BlockSpec((1,) + vt.shape[1:], lambda i, qi: (i, 0, 0, 0), pipeline_mode=pl.Buffered(1)),
                  _resident(bias.shape), _resident(lam_p.shape), _resident((dv, 1))],
        out_specs=q_spec,
        out_shape=jax.ShapeDtypeStruct((b, t, d), BF16),
        scratch_shapes=[pltpu.VMEM((n_chain, 1, tile), F32), pltpu.VMEM((n_chain, 1, tile), F32),
                        pltpu.VMEM((n_chain, ext, tile), F32),
                        pltpu.VMEM((n_chain, tile, tile), F32), pltpu.VMEM((n_chain, tile, tile), BF16)],
        compiler_params=_params("parallel", "arbitrary"),
        name="attn_prompt",
    )(q, k, vt, bias, lam_p, head_g.reshape(dv, 1))


def _attn_sample_kernel(q_ref, kc_ref, vtc_ref, kn_ref, vtn_ref, bias_c_ref, bias_n_ref, lam_ref, hg_ref,
                        o_ref, qbd_s, m_s, alpha_s, acc_s, s_scr, p_scr, *, lam_init):
    tq = q_ref.shape[1]
    tile = vtc_ref.shape[3]
    n_cache = vtc_ref.shape[1]
    ext = acc_s.shape[1]
    dv = o_ref.shape[2] // A_HEADS
    gw = qbd_s.shape[1]
    n_group = qbd_s.shape[0]
    n_sub = gw // tq
    row_blk = lax.broadcasted_iota(jnp.int32, (gw, gw), 0) // tq
    col_blk = lax.broadcasted_iota(jnp.int32, (gw, gw), 1) // (gw // n_sub)
    for g in range(n_group):
        qg = q_ref[0, :, g * gw:(g + 1) * gw]
        qrep = jnp.concatenate([qg] * n_sub, axis=0)
        qbd_s[g] = jnp.where(row_blk == col_blk, qrep, jnp.zeros_like(qrep))
    m_s[...] = jnp.full(m_s.shape, NEG_BIG, F32)
    acc_s[...] = jnp.zeros(acc_s.shape, F32)

    def kv_tile(k_tile, vt_tile, bias_ref, rows):
        for g in range(n_group):
            s_scr[g, :rows] = _dot_nt(k_tile(g), qbd_s[g])
        for g in range(n_group):
            s = s_scr[g, :rows]
            if bias_ref is not None:
                s = s + bias_ref[g]
                s_scr[g, :rows] = s
            m_prev = m_s[g]
            m_new = jnp.maximum(m_prev, jnp.max(s, axis=0, keepdims=True))
            alpha_s[g] = jnp.exp2(m_prev - m_new)
            m_s[g] = m_new
        for g in range(n_group):
            p_scr[g, :rows] = jnp.exp2(s_scr[g, :rows] - m_s[g]).astype(BF16)
        for h in range(A_HEADS):
            g, half = divmod(h, 2)
            lanes = slice(half * 2 * tq, (half + 1) * 2 * tq)
            acc_s[h] = alpha_s[g, :, lanes] * acc_s[h] + _dot(vt_tile(h), p_scr[g, :rows, lanes])

    def cache_tile(j, bias_ref):
        koff = pl.multiple_of(j * tile, tile)
        kv_tile(lambda g: kc_ref[0, pl.ds(koff, tile), g * gw:(g + 1) * gw],
                lambda h: vtc_ref[0, j, h * ext:(h + 1) * ext, :], bias_ref, tile)

    def far_body(j, carry):
        cache_tile(j, None)
        return carry

    lax.fori_loop(0, n_cache - 1, far_body, 0)
    cache_tile(n_cache - 1, bias_c_ref)
    kv_tile(lambda g: kn_ref[0, :, g * gw:(g + 1) * gw],
            lambda h: vtn_ref[0, 0, h * ext:(h + 1) * ext, :], bias_n_ref, tq)

    lp = lam_ref[...]
    lam = (jnp.exp(jnp.sum(lp[0:1] * lp[1:2], axis=-1, keepdims=True))
           - jnp.exp(jnp.sum(lp[2:3] * lp[3:4], axis=-1, keepdims=True)) + lam_init)
    for h in range(A_HEADS):
        both = (acc_s[h, :dv] / acc_s[h, dv:dv + 1]).T
        o = both[:tq] - lam * both[tq:]
        o = _rms(o, hg_ref[...]) * (1.0 - lam_init)
        o_ref[0, :, h * dv:(h + 1) * dv] = o.astype(o_ref.dtype)


def _attn_sample(q, cache_k, cache_vt, new_k, new_vt, rel_bias, lam_p, head_g, lam_init):
    b, t, d = q.shape
    past = cache_k.shape[1]
    n_cache, d_ext, tile = cache_vt.shape[1:]
    assert past == n_cache * tile and past % CHUNK == 0 and tile >= MAX_DISTANCE
    dv = d // A_HEADS
    ext = d_ext // A_HEADS
    gw = 4 * t
    assert gw == 2 * dv and A_HEADS % 2 == 0
    n_group = A_HEADS // 2

    def packed(bias):
        bt = bias.swapaxes(-1, -2).reshape(n_group, 2, -1, t)
        return jnp.concatenate([bt[:, 0], bt[:, 0], bt[:, 1], bt[:, 1]], axis=-1)

    bias_c = packed(_bias_tile(rel_bias, past, past - tile, t, tile))
    bias_n = packed(_bias_tile(rel_bias, past, past, t, t))
    tok_spec = pl.BlockSpec((1, t, d), lambda i: (i, 0, 0))
    return pl.pallas_call(
        functools.partial(_attn_sample_kernel, lam_init=lam_init),
        grid=(b,),
        in_specs=[tok_spec,
                  pl.BlockSpec((1, past, d), lambda i: (i, 0, 0)),
                  pl.BlockSpec((1, n_cache, d_ext, tile), lambda i: (i, 0, 0, 0)),
                  tok_spec,
                  pl.BlockSpec((1, 1, d_ext, t), lambda i: (i, 0, 0, 0)),
                  _resident(bias_c.shape), _resident(bias_n.shape), _resident(lam_p.shape),
                  _resident((1, dv))],
        out_specs=tok_spec,
        out_shape=jax.ShapeDtypeStruct((b, t, d), BF16),
        scratch_shapes=[pltpu.VMEM((n_group, gw, gw), BF16),
                        pltpu.VMEM((n_group, 1, gw), F32), pltpu.VMEM((n_group, 1, gw), F32),
                        pltpu.VMEM((A_HEADS, ext, 2 * t), F32),
                        pltpu.VMEM((n_group, tile, gw), F32), pltpu.VMEM((n_group, tile, gw), BF16)],
        compiler_params=_params("parallel"),
        name="attn_sample",
    )(q, cache_k, cache_vt, new_k, new_vt, bias_c, bias_n, lam_p, head_g.reshape(1, dv))


def _cache_layouts(cache_k, cache_v, tile):
    b, past, heads, two, dh = cache_k.shape
    dv = two * dh
    n = past // tile
    k = cache_k.reshape(b, past, heads * dv).astype(BF16)
    vt = cache_v.astype(BF16).reshape(b, n, tile, heads, dv).transpose(0, 1, 3, 4, 2)
    vt = jnp.concatenate([vt, jnp.ones((b, n, heads, ONES_ROWS, tile), BF16)], axis=3)
    return k, vt.reshape(b, n, heads * (dv + ONES_ROWS), tile)


def _trunk(x, mod, mstate, past, chunk, rows, p):
    b, t, d = x.shape
    depth = p["norm_g"].shape[0]
    n_a = p["mlstm_w_in"].shape[0]
    qk = M_HEADS * (d // M_HEADS // 2)
    new_c, new_n, new_m = [], [], []
    prompt = past is None
    k_f32 = v_f32 = k_bf = vt_bf = None
    for l in range(depth):
        sh1, sc1, g1, sh2, sc2, g2 = [mod[l, :, i * d:(i + 1) * d].reshape(b, 1, d) for i in range(6)]
        if l < n_a:
            w_in = p["mlstm_w_in"][l]
            n_main = 2 * qk + 2 * d
            qkvo, g, gt = _norm_proj(x, p["norm_g"][l, 0], sh1, sc1, w_in[:, :n_main], rows=rows,
                                     w_gate=w_in[:, n_main:])
            hn, c_l, n_l, m_l = _mlstm(qkvo, g, gt, p["mlstm_b_gates"][l], p["mlstm_head_g"][l],
                                       mstate[0][l], mstate[1][l], mstate[2][l], chunk=chunk)
            new_c.append(c_l)
            new_n.append(n_l)
            new_m.append(m_l)
            x = _proj_residual(hn, p["mlstm_w_out"][l], x, g1, rows=rows)
        else:
            j = l - n_a
            lam_init = 0.8 - 0.6 * math.exp(-0.3 * l)
            dh = p["q_norm_g"].shape[-1]
            q = _norm_proj(x, p["norm_g"][l, 0], sh1, sc1, p["attn_w_q"][j], rows=rows,
                           qk_gain=p["q_norm_g"][j], qk_scale=dh ** -0.5 * LOG2E)
            if prompt:
                o = _attn_prompt(q, k_bf, vt_bf, p["rel_bias"], p["attn_lambda"][j], p["attn_head_g"][j],
                                 lam_init)
            else:
                o = _attn_sample(q, past[0], past[1], k_bf, vt_bf, p["rel_bias"], p["attn_lambda"][j],
                                 p["attn_head_g"][j], lam_init)
            x = _proj_residual(o, p["attn_w_o"][j], x, g1, rows=rows)
        x = _ffn(x, p["norm_g"][l, 1], sh2, sc2, g2, p["ffn_w_gate"][l], p["ffn_w_up"][l],
                 p["ffn_w_down"][l], rows=rows)
        if l == n_a - 1:
            k_f32, v_f32, k_bf, vt_bf = _shared_kv(x, p["kv_norm_g"], p["w_k"], p["w_v"], p["k_norm_g"],
                                                   rows=rows, attn_tile=ATTN_TILE if prompt else t)
    dh = p["k_norm_g"].shape[-1]
    return (x, jnp.stack(new_c), jnp.stack(new_n), jnp.stack(new_m),
            k_f32.reshape(b, t, A_HEADS, 2, dh), v_f32.reshape(b, t, A_HEADS, 2 * dh))


def kernel(x_prompt, x_sample, c_prompt, c_sample, state_mlstm_C, state_mlstm_n, state_mlstm_m, cache_k, cache_v, ada_w, ada_b, norm_g, mlstm_w_in, mlstm_b_gates, mlstm_head_g, mlstm_w_out, kv_norm_g, w_k, w_v, k_norm_g, attn_w_q, q_norm_g, attn_lambda, attn_head_g, attn_w_o, rel_bias, ffn_w_gate, ffn_w_up, ffn_w_down):
    bf = lambda w: w.astype(BF16)
    p = dict(norm_g=norm_g, mlstm_w_in=bf(mlstm_w_in), mlstm_b_gates=mlstm_b_gates,
             mlstm_head_g=mlstm_head_g, mlstm_w_out=bf(mlstm_w_out), kv_norm_g=kv_norm_g, w_k=bf(w_k),
             w_v=bf(w_v), k_norm_g=k_norm_g, attn_w_q=bf(attn_w_q), q_norm_g=q_norm_g,
             attn_lambda=attn_lambda, attn_head_g=attn_head_g, attn_w_o=bf(attn_w_o), rel_bias=rel_bias,
             ffn_w_gate=bf(ffn_w_gate), ffn_w_up=bf(ffn_w_up), ffn_w_down=bf(ffn_w_down))
    bp = x_prompt.shape[0]
    n_a, _, heads, dk, dv = state_mlstm_C.shape
    mod = _adaln(jnp.concatenate([c_prompt, c_sample], axis=0), ada_w, ada_b)
    zero_state = (jnp.zeros((n_a, bp, heads, dk, dv), F32), jnp.zeros((n_a, bp, heads, dk), F32),
                  jnp.zeros((n_a, bp, heads), F32))
    y_p, c_p, n_p, m_p, k_p, v_p = _trunk(x_prompt, mod[:, :bp], zero_state, None, MLSTM_PROMPT_CHUNK, 512, p)
    y_s, c_s, n_s, m_s, k_s, v_s = _trunk(x_sample, mod[:, bp:], (state_mlstm_C, state_mlstm_n, state_mlstm_m),
                                          _cache_layouts(cache_k, cache_v, ATTN_TILE), x_sample.shape[1], 512, p)
    return (y_p, y_s, c_p, n_p, m_p, k_p, v_p, c_s, n_s, m_s, k_s, v_s)
```

```python
import functools
import math

import jax
import jax.numpy as jnp
from jax import lax
from jax.experimental import pallas as pl
from jax.experimental.pallas import tpu as pltpu

F32 = jnp.float32
BF16 = jnp.bfloat16

EPS = 1e-6
NEG_BIG = -1e30
CHUNK = 64
N_BUCKETS = 32
MAX_DISTANCE = 128
M_HEADS = 8
A_HEADS = 8
GATE_PAD = 128
LANES = 128
ONES_ROWS = 16
LOG2E = math.log2(math.e)

V7X_VMEM_BYTES = 64 * 1024 * 1024
VMEM_LIMIT = V7X_VMEM_BYTES - 8 * 1024 * 1024

ATTN_TILE = 256
MLSTM_PROMPT_CHUNK = 256
FFN_CHUNK = 256


def _params(*sem, flags=None):
    return pltpu.CompilerParams(dimension_semantics=sem, vmem_limit_bytes=VMEM_LIMIT, flags=flags)


def _rms(x, g):
    return x * lax.rsqrt(jnp.mean(x * x, axis=-1, keepdims=True) + EPS) * g


def _log_sigmoid(x):
    return jnp.minimum(x, 0.0) - jnp.log1p(jnp.exp(-jnp.abs(x)))


def _dot(a, b):
    return jnp.dot(a, b, preferred_element_type=F32)


def _dot_nt(a, b):
    return lax.dot_general(a, b, (((1,), (1,)), ((), ())), preferred_element_type=F32)


def _dot_tn(a, b):
    return lax.dot_general(a, b, (((0,), (0,)), ((), ())), preferred_element_type=F32)


def _resident(shape):
    nd = len(shape)
    return pl.BlockSpec(shape, lambda *_: (0,) * nd, pipeline_mode=pl.Buffered(1))


def _adaln_kernel(c_ref, w_ref, b_ref, o_ref):
    c = c_ref[...]
    a = (c * jax.nn.sigmoid(c)).astype(BF16)
    o_ref[0] = _dot(a, w_ref[0].astype(BF16)) + b_ref[0]


def _adaln(c, ada_w, ada_b):
    depth, d, n = ada_w.shape
    rows = c.shape[0]
    tn = 1536
    assert n % tn == 0
    return pl.pallas_call(
        _adaln_kernel,
        grid=(depth, n // tn),
        in_specs=[
            pl.BlockSpec((rows, d), lambda l, j: (0, 0)),
            pl.BlockSpec((1, d, tn), lambda l, j: (l, 0, j)),
            pl.BlockSpec((1, 1, tn), lambda l, j: (l, 0, j)),
        ],
        out_specs=pl.BlockSpec((1, rows, tn), lambda l, j: (l, 0, j)),
        out_shape=jax.ShapeDtypeStruct((depth, rows, n), F32),
        compiler_params=_params("parallel", "parallel"),
        name="adaln",
    )(c, ada_w, ada_b.reshape(depth, 1, n))


def _modulated(x_ref, g_ref, sh_ref, sc_ref):
    x = x_ref[...]
    bb, tt, d = x.shape
    xm = _rms(x, g_ref[...]) * (1.0 + sc_ref[...]) + sh_ref[...]
    return xm.reshape(bb * tt, d).astype(BF16)


def _group_rms_store(src_ref, dst_ref, gain_ref, post_scale, width):
    bb, tt, n = dst_ref.shape
    for g in range(n // width):
        seg = src_ref[:, g * width:(g + 1) * width]
        y = _rms(seg, gain_ref[...]) * post_scale
        dst_ref[:, :, g * width:(g + 1) * width] = y.reshape(bb, tt, width).astype(dst_ref.dtype)


def _store_values_ext(dst_ref, lead, vt, col, width, heads):
    dv = vt.shape[0] // heads
    ext = dv + ONES_ROWS
    ones = jnp.ones((ONES_ROWS, width), BF16)
    for hh in range(heads):
        dst_ref[lead + (slice(hh * ext, hh * ext + dv), slice(None))] = vt[hh * dv:(hh + 1) * dv, col:col + width]
        dst_ref[lead + (slice(hh * ext + dv, (hh + 1) * ext), slice(None))] = ones


def _norm_proj_kernel(*refs, n_chunk, mlstm, qk_scale):
    x_ref, g_ref, sh_ref, sc_ref, w_ref = refs[:5]
    pos = 5
    if mlstm:
        wf_ref, wg_ref, wgt_ref = refs[pos:pos + 3]
        pos += 3
    if qk_scale is not None:
        qg_ref = refs[pos]
        pos += 1
    o_ref = refs[pos]
    pos += 1
    if mlstm:
        of_ref, og_ref, ogt_ref = refs[pos:pos + 3]
        pos += 3
    if qk_scale is not None:
        y_scr = refs[pos]

    bb, tt, n = o_ref.shape
    xm = _modulated(x_ref, g_ref, sh_ref, sc_ref)
    for c in range(n // n_chunk):
        y = _dot(xm, w_ref[:, c * n_chunk:(c + 1) * n_chunk])
        if qk_scale is None:
            o_ref[:, :, c * n_chunk:(c + 1) * n_chunk] = y.reshape(bb, tt, n_chunk).astype(o_ref.dtype)
        else:
            y_scr[:, c * n_chunk:(c + 1) * n_chunk] = y
    if qk_scale is not None:
        _group_rms_store(y_scr, o_ref, qg_ref, qk_scale, qg_ref.shape[-1])
    if mlstm:
        vt = _dot_nt(wf_ref[...], xm).astype(BF16)
        for i in range(bb):
            _store_values_ext(of_ref, (i,), vt, i * tt, tt, M_HEADS)
            ogt_ref[i] = _dot_nt(wgt_ref[...], xm[i * tt:(i + 1) * tt])
        og_ref[...] = _dot(xm, wg_ref[...]).reshape(bb, tt, 2 * GATE_PAD)


def _row_blocks(b, t, rows):
    if t >= rows:
        assert t % rows == 0
        return 1, rows
    assert rows % t == 0 and b % (rows // t) == 0
    return rows // t, t


def _norm_proj(x, gain, shift, scale, w, *, rows, w_feat=None, w_gate=None, qk_gain=None, qk_scale=None):
    b, t, d = x.shape
    n = w.shape[1]
    bb, tt = _row_blocks(b, t, rows)
    grid = (b // bb, t // tt)
    tok = lambda i, j: (i, j, 0)
    per_b = lambda i, j: (i, 0, 0)
    in_specs = [
        pl.BlockSpec((bb, tt, d), tok),
        _resident((1, d)),
        pl.BlockSpec((bb, 1, d), per_b),
        pl.BlockSpec((bb, 1, d), per_b),
        _resident((d, n)),
    ]
    args = [x, gain.reshape(1, d), shift, scale, w]
    out_specs = [pl.BlockSpec((bb, tt, n), tok)]
    out_shape = [jax.ShapeDtypeStruct((b, t, n), BF16)]
    scratch = []
    mlstm = w_gate is not None
    if mlstm:
        ng = w_gate.shape[1]
        half = ng // 2
        wg = (jnp.zeros((d, 2 * GATE_PAD), BF16).at[:, :half].set(w_gate[:, :half])
              .at[:, GATE_PAD:GATE_PAD + half].set(w_gate[:, half:]))
        n_feat = w_feat.shape[1]
        d_ext = n_feat + M_HEADS * ONES_ROWS
        in_specs += [_resident((n_feat, d)), _resident((d, 2 * GATE_PAD)), _resident((ng, d))]
        args += [w_feat.T, wg, w_gate.T]
        out_specs += [pl.BlockSpec((bb, d_ext, tt), lambda i, j: (i, 0, j)),
                      pl.BlockSpec((bb, tt, 2 * GATE_PAD), tok),
                      pl.BlockSpec((bb, ng, tt), lambda i, j: (i, 0, j))]
        out_shape += [jax.ShapeDtypeStruct((b, d_ext, t), BF16),
                      jax.ShapeDtypeStruct((b, t, 2 * GATE_PAD), F32),
                      jax.ShapeDtypeStruct((b, ng, t), F32)]
    if qk_scale is not None:
        in_specs.append(_resident((1, qk_gain.shape[-1])))
        args.append(qk_gain.reshape(1, -1))
        scratch.append(pltpu.VMEM((bb * tt, n), F32))
    kern = functools.partial(_norm_proj_kernel, n_chunk=min(n, 1024), mlstm=mlstm, qk_scale=qk_scale)
    out = pl.pallas_call(
        kern, grid=grid, in_specs=in_specs, out_specs=out_specs, out_shape=out_shape,
        scratch_shapes=scratch, compiler_params=_params("parallel", "parallel"), name="norm_proj",
    )(*args)
    return out if mlstm else out[0]


def _kv_kernel(x_ref, g_ref, wk_ref, wv_ref, wvt_ref, kg_ref, k_ref, v_ref, kb_ref, vt_ref, y_scr):
    x = x_ref[...]
    bb, tt, d = x.shape
    h = _rms(x, g_ref[...]).reshape(bb * tt, d).astype(BF16)
    y_scr[...] = _dot(h, wk_ref[...])
    _group_rms_store(y_scr, k_ref, kg_ref, 1.0, kg_ref.shape[-1])
    v_ref[...] = _dot(h, wv_ref[...]).reshape(bb, tt, d)
    kb_ref[...] = k_ref[...].astype(BF16)
    vt = _dot_nt(wvt_ref[...], h).astype(BF16)
    per, tile = vt_ref.shape[1], vt_ref.shape[3]
    for i in range(bb):
        for j in range(per):
            _store_values_ext(vt_ref, (i, j), vt, (i * per + j) * tile, tile, A_HEADS)


def _shared_kv(x, kv_g, w_k, w_v, k_g, *, rows, attn_tile):
    b, t, d = x.shape
    bb, tt = _row_blocks(b, t, rows)
    assert tt % attn_tile == 0
    per = tt // attn_tile
    d_ext = d + A_HEADS * ONES_ROWS
    tok = lambda i, j: (i, j, 0)
    blk = pl.BlockSpec((bb, tt, d), tok)
    return pl.pallas_call(
        _kv_kernel,
        grid=(b // bb, t // tt),
        in_specs=[blk, _resident((1, d)), _resident((d, d)), _resident((d, d)), _resident((d, d)),
                  _resident((1, k_g.shape[-1]))],
        out_specs=[blk, blk, blk, pl.BlockSpec((bb, per, d_ext, attn_tile), lambda i, j: (i, j, 0, 0))],
        out_shape=[jax.ShapeDtypeStruct((b, t, d), F32), jax.ShapeDtypeStruct((b, t, d), F32),
                   jax.ShapeDtypeStruct((b, t, d), BF16),
                   jax.ShapeDtypeStruct((b, t // attn_tile, d_ext, attn_tile), BF16)],
        scratch_shapes=[pltpu.VMEM((bb * tt, d), F32)],
        compiler_params=_params("parallel", "parallel"),
        name="shared_kv",
    )(x, kv_g.reshape(1, d), w_k, w_v, w_v.T, k_g.reshape(1, -1))


def _proj_residual_kernel(a_ref, w_ref, x_ref, gate_ref, o_ref):
    bb, tt, d = x_ref.shape
    a = a_ref[...].reshape(bb * tt, a_ref.shape[-1])
    y = _dot(a, w_ref[...]).reshape(bb, tt, d)
    o_ref[...] = x_ref[...] + gate_ref[...] * y


def _proj_residual(a, w, x, gate, *, rows):
    b, t, d = x.shape
    k = a.shape[-1]
    bb, tt = _row_blocks(b, t, rows)
    tok = lambda i, j: (i, j, 0)
    return pl.pallas_call(
        _proj_residual_kernel,
        grid=(b // bb, t // tt),
        in_specs=[pl.BlockSpec((bb, tt, k), tok), _resident((k, d)),
                  pl.BlockSpec((bb, tt, d), tok), pl.BlockSpec((bb, 1, d), lambda i, j: (i, 0, 0))],
        out_specs=pl.BlockSpec((bb, tt, d), tok),
        out_shape=jax.ShapeDtypeStruct((b, t, d), F32),
        compiler_params=_params("parallel", "parallel"),
        name="proj_residual",
    )(a, w, x, gate)


def _ffn_kernel(x_ref, g_ref, sh_ref, sc_ref, gate_ref, wg_ref, wu_ref, wd_ref, o_ref, acc_ref):
    bb, tt, d = x_ref.shape
    f = wg_ref.shape[1]
    xm = _modulated(x_ref, g_ref, sh_ref, sc_ref)
    for c in range(f // FFN_CHUNK):
        sl = slice(c * FFN_CHUNK, (c + 1) * FFN_CHUNK)
        hg = _dot(xm, wg_ref[:, sl])
        hu = _dot(xm, wu_ref[:, sl])
        act = (hg * jax.nn.sigmoid(hg) * hu).astype(BF16)
        part = _dot(act, wd_ref[sl, :])
        if c == 0:
            acc_ref[...] = part
        else:
            acc_ref[...] += part
    o_ref[...] = x_ref[...] + gate_ref[...] * acc_ref[...].reshape(bb, tt, d)


def _ffn(x, gain, shift, scale, gate, w_gate, w_up, w_down, *, rows):
    b, t, d = x.shape
    f = w_gate.shape[1]
    assert f % FFN_CHUNK == 0
    bb, tt = _row_blocks(b, t, rows)
    tok = lambda i, j: (i, j, 0)
    per_b = lambda i, j: (i, 0, 0)
    return pl.pallas_call(
        _ffn_kernel,
        grid=(b // bb, t // tt),
        in_specs=[pl.BlockSpec((bb, tt, d), tok), _resident((1, d)),
                  pl.BlockSpec((bb, 1, d), per_b), pl.BlockSpec((bb, 1, d), per_b),
                  pl.BlockSpec((bb, 1, d), per_b),
                  _resident((d, f)), _resident((d, f)), _resident((f, d))],
        out_specs=pl.BlockSpec((bb, tt, d), tok),
        out_shape=jax.ShapeDtypeStruct((b, t, d), F32),
        scratch_shapes=[pltpu.VMEM((bb * tt, d), F32)],
        compiler_params=_params("parallel", "parallel"),
        name="ffn",
    )(x, gain.reshape(1, d), shift, scale, gate, w_gate, w_up, w_down)


def _mlstm_kernel(q_ref, k_ref, o_ref, vt_ref, g_ref, gt_ref, brow_ref, bcol_ref, hg_ref,
                  c0_ref, n0_ref, m0_ref, h_ref, c_out, n_out, m_out,
                  ct_s, m_s, mnew_s, acol_s, br_s, inter_s, wn_s, wold_s, qk_s, qc_s, sb_s, sv_s, stat_s):
    ci = pl.program_id(1)
    length = q_ref.shape[1]
    ext = ct_s.shape[1]
    dv = ext - ONES_ROWS
    dk = q_ref.shape[2] // M_HEADS
    heads = range(M_HEADS)

    @pl.when(ci == 0)
    def _():
        ct_s[...] = jnp.zeros(ct_s.shape, F32)
        for h in heads:
            c0 = jnp.concatenate([c0_ref[0, h], jnp.zeros((dv - dk, dv), F32)], axis=0)
            ct_s[h, :dv, :] = c0.T
            ct_s[h, dv:, :dk] = jnp.broadcast_to(n0_ref[0, h:h + 1, :], (ONES_ROWS, dk))
        m_s[...] = m0_ref[0]

    row = lax.broadcasted_iota(jnp.int32, (length, length), 0)
    col = lax.broadcasted_iota(jnp.int32, (length, length), 1)
    causal = row <= col
    hi = lax.Precision.HIGHEST
    g = g_ref[0] + brow_ref[...]
    b_cols = jnp.dot((col <= row).astype(F32), _log_sigmoid(g[:, GATE_PAD:]), precision=hi,
                     preferred_element_type=F32)
    acol_s[...] = g[:, :GATE_PAD] - b_cols
    gt = gt_ref[0] + bcol_ref[...]
    ig_rows = gt[:M_HEADS]
    br_s[...] = jnp.dot(_log_sigmoid(gt[M_HEADS:]), causal.astype(F32), precision=hi,
                        preferred_element_type=F32)
    b_rows = br_s[...]
    b_end = br_s[:, length - 1:length]
    m_prev = m_s[...]
    g_rows = b_end - b_rows + ig_rows
    m_new = jnp.maximum(b_end + m_prev, jnp.max(g_rows, axis=-1, keepdims=True))
    mnew_s[...] = m_new
    wold_s[...] = jnp.exp(b_end + m_prev - m_new)
    wn_s[...] = jnp.exp(g_rows - m_new)
    inter_s[...] = b_rows + m_prev

    def q_of(h):
        return q_ref[0, :, h * dk:(h + 1) * dk]

    def k_of(h):
        return k_ref[0, :, h * dk:(h + 1) * dk] * (dk ** -0.5)

    def vt_of(h):
        return vt_ref[0, h * ext:(h + 1) * ext, :]

    for h in heads:
        qk_s[h] = _dot_nt(k_of(h), q_of(h))
        qc_s[h] = _dot_nt(ct_s[h, :, :dk].astype(BF16), q_of(h))
    for h in heads:
        inter = inter_s[h:h + 1, :]
        dmat = jnp.where(causal, acol_s[:, h:h + 1] + br_s[h:h + 1, :], -jnp.inf)
        m_t = jnp.maximum(inter, jnp.max(dmat, axis=0, keepdims=True))
        sb_s[h] = (qk_s[h] * jnp.exp(dmat - m_t)).astype(BF16)
        stat_s[0, h] = jnp.exp(inter - m_t)
        stat_s[1, h] = jnp.exp(-m_t)
    for h in heads:
        sv_s[h] = _dot(vt_of(h), sb_s[h])
    for h in heads:
        w_inter = stat_s[0, h]
        tot = w_inter * qc_s[h] + sv_s[h]
        hh = tot[:dv] / jnp.maximum(jnp.abs(tot[dv:dv + 1]), stat_s[1, h])
        hh = hh * lax.rsqrt(jnp.mean(hh * hh, axis=0, keepdims=True) + EPS)
        og = o_ref[0, :, h * dv:(h + 1) * dv].astype(F32)
        h_ref[0, :, h * dv:(h + 1) * dv] = (_to_token_major(hh) * hg_ref[:, h * dv:(h + 1) * dv]
                                            * jax.nn.sigmoid(og)).astype(h_ref.dtype)
    for h in heads:
        vtw = (vt_of(h).astype(F32) * wn_s[h:h + 1, :]).astype(BF16)
        ct_s[h, :, :dk] = wold_s[h:h + 1, :] * ct_s[h, :, :dk] + _dot(vtw, k_of(h))
    m_s[...] = mnew_s[...]

    @pl.when(ci == pl.num_programs(1) - 1)
    def _():
        for h in heads:
            c_out[0, h] = ct_s[h, :dv, :].T[:dk]
            n_out[0, h:h + 1, :] = ct_s[h, dv:dv + 1, :dk]
        m_out[0] = m_s[...]


def _to_token_major(x):
    dv, length = x.shape
    if length % LANES:
        pad = LANES - length % LANES
        x = jnp.concatenate([x, jnp.zeros((dv, pad), x.dtype)], axis=1)
    return x.T[:length]


def _mlstm(qko, vt, g, gt, b_gates, head_g, c0, n0, m0, *, chunk):
    b, t, _ = qko.shape
    heads, dk, dv = c0.shape[1:]
    d = heads * dv
    qk = heads * dk
    ext = dv + ONES_ROWS
    assert qk * 2 == d and t % chunk == 0 and heads == M_HEADS and vt.shape[1] == heads * ext
    nc = t // chunk
    two_h = 2 * heads
    brow = (jnp.zeros((1, 2 * GATE_PAD), F32).at[0, :heads].set(b_gates[:heads])
            .at[0, GATE_PAD:GATE_PAD + heads].set(b_gates[heads:]))
    st4 = lambda i, c: (i, 0, 0, 0)
    st3 = lambda i, c: (i, 0, 0)
    outs = pl.pallas_call(
        _mlstm_kernel,
        grid=(b, nc),
        in_specs=[
            pl.BlockSpec((1, chunk, qk), lambda i, c: (i, c, 0)),
            pl.BlockSpec((1, chunk, qk), lambda i, c: (i, c, 1)),
            pl.BlockSpec((1, chunk, d), lambda i, c: (i, c, 1)),
            pl.BlockSpec((1, heads * ext, chunk), lambda i, c: (i, 0, c)),
            pl.BlockSpec((1, chunk, 2 * GATE_PAD), lambda i, c: (i, c, 0)),
            pl.BlockSpec((1, two_h, chunk), lambda i, c: (i, 0, c)),
            _resident((1, 2 * GATE_PAD)), _resident((two_h, 1)), _resident((1, d)),
            pl.BlockSpec((1, heads, dk, dv), st4),
            pl.BlockSpec((1, heads, dk), st3),
            pl.BlockSpec((1, heads, 1), st3),
        ],
        out_specs=[
            pl.BlockSpec((1, chunk, d), lambda i, c: (i, c, 0)),
            pl.BlockSpec((1, heads, dk, dv), st4),
            pl.BlockSpec((1, heads, dk), st3),
            pl.BlockSpec((1, heads, 1), st3),
        ],
        out_shape=[
            jax.ShapeDtypeStruct((b, t, d), BF16),
            jax.ShapeDtypeStruct((b, heads, dk, dv), F32),
            jax.ShapeDtypeStruct((b, heads, dk), F32),
            jax.ShapeDtypeStruct((b, heads, 1), F32),
        ],
        scratch_shapes=[pltpu.VMEM((heads, ext, dv), F32),
                        pltpu.VMEM((heads, 1), F32), pltpu.VMEM((heads, 1), F32),
                        pltpu.VMEM((chunk, GATE_PAD), F32),
                        pltpu.VMEM((heads, chunk), F32), pltpu.VMEM((heads, chunk), F32),
                        pltpu.VMEM((heads, chunk), F32), pltpu.VMEM((heads, 1), F32),
                        pltpu.VMEM((heads, chunk, chunk), F32),
                        pltpu.VMEM((heads, ext, chunk), F32),
                        pltpu.VMEM((heads, chunk, chunk), BF16),
                        pltpu.VMEM((heads, ext, chunk), F32),
                        pltpu.VMEM((2, heads, 1, chunk), F32)],
        compiler_params=_params("parallel", "arbitrary"),
        name="mlstm",
    )(qko, qko, qko, vt, g, gt, brow, b_gates.reshape(two_h, 1), head_g.reshape(1, d),
      c0, n0, m0.reshape(b, heads, 1))
    hn, c_new, n_new, m_new = outs
    return hn, c_new, n_new, m_new.reshape(b, heads)


def _rel_bucket(rel):
    nb = N_BUCKETS // 2
    max_exact = nb // 2
    ret = jnp.where(rel > 0, nb, 0)
    n = jnp.abs(rel)
    large = max_exact + (jnp.log(jnp.maximum(n, 1).astype(F32) / max_exact)
                         / math.log(MAX_DISTANCE / max_exact) * (nb - max_exact)).astype(jnp.int32)
    large = jnp.minimum(large, nb - 1)
    return ret + jnp.where(n < max_exact, n, large)


def _bias_tile(rel_bias, q0, k0, tq, tk):
    table = rel_bias.astype(F32)
    span = tq + tk
    r = jnp.arange(span)
    r = jnp.where(r < tk, r, r - span)
    vec = (table[_rel_bucket(r + (k0 - q0))] - table[N_BUCKETS // 2 - 1]).T
    bias = jnp.tile(vec, (1, tq))[:, :tq * (span - 1)].reshape(-1, tq, span - 1)[:, :, :tk]
    q_chunk = (q0 + lax.broadcasted_iota(jnp.int32, (tq, tk), 0)) // CHUNK
    k_chunk = (k0 + lax.broadcasted_iota(jnp.int32, (tq, tk), 1)) // CHUNK
    return jnp.where((k_chunk <= q_chunk)[None], bias * LOG2E, NEG_BIG)


def _attn_prompt_kernel(q_ref, k_ref, vt_ref, bias_ref, lam_ref, hg_ref, o_ref,
                        m_s, alpha_s, acc_s, s_scr, p_scr, *, lam_init):
    qi = pl.program_id(1)
    tile = q_ref.shape[1]
    ext = acc_s.shape[1]
    dv = o_ref.shape[2] // A_HEADS
    dh = dv // 2
    n_chain = 2 * A_HEADS
    m_s[...] = jnp.full(m_s.shape, NEG_BIG, F32)
    acc_s[...] = jnp.zeros(acc_s.shape, F32)

    def kv_tile(j, slot):
        koff = pl.multiple_of(j * tile, tile)
        for hc in range(n_chain):
            lo = hc * dh
            s_scr[hc] = _dot_nt(k_ref[0, pl.ds(koff, tile), lo:lo + dh], q_ref[0, :, lo:lo + dh])
        for hc in range(n_chain):
            s = s_scr[hc]
            if slot is not None:
                s = s + bias_ref[hc // 2, slot]
                s_scr[hc] = s
            m_prev = m_s[hc]
            m_new = jnp.maximum(m_prev, jnp.max(s, axis=0, keepdims=True))
            alpha_s[hc] = jnp.exp2(m_prev - m_new)
            m_s[hc] = m_new
        for hc in range(n_chain):
            p_scr[hc] = jnp.exp2(s_scr[hc] - m_s[hc]).astype(BF16)
        for hc in range(n_chain):
            h = hc // 2
            acc_s[hc] = alpha_s[hc] * acc_s[hc] + _dot(vt_ref[0, j, h * ext:(h + 1) * ext, :], p_scr[hc])

    def far_body(j, carry):
        kv_tile(j, None)
        return carry

    lax.fori_loop(0, jnp.maximum(qi - 1, 0), far_body, 0)

    @pl.when(qi > 0)
    def _():
        kv_tile(qi - 1, 0)

    kv_tile(qi, 1)

    lp = lam_ref[...]
    lam = (jnp.exp(jnp.sum(lp[0:1] * lp[1:2], axis=-1, keepdims=True))
           - jnp.exp(jnp.sum(lp[2:3] * lp[3:4], axis=-1, keepdims=True)) + lam_init)
    for h in range(A_HEADS):
        o = (acc_s[2 * h, :dv] / acc_s[2 * h, dv:dv + 1]
             - lam * (acc_s[2 * h + 1, :dv] / acc_s[2 * h + 1, dv:dv + 1]))
        o = o * lax.rsqrt(jnp.mean(o * o, axis=0, keepdims=True) + EPS) * hg_ref[...] * (1.0 - lam_init)
        o_ref[0, :, h * dv:(h + 1) * dv] = o.T.astype(o_ref.dtype)


def _attn_prompt(q, k, vt, rel_bias, lam_p, head_g, lam_init):
    b, t, d = q.shape
    tile = ATTN_TILE
    assert t % tile == 0 and tile % CHUNK == 0 and tile >= MAX_DISTANCE and vt.shape[-1] == tile
    n = t // tile
    dv = d // A_HEADS
    ext = vt.shape[2] // A_HEADS
    n_chain = 2 * A_HEADS
    bias = jnp.stack([_bias_tile(rel_bias, tile, 0, tile, tile), _bias_tile(rel_bias, 0, 0, tile, tile)],
                     axis=1).swapaxes(-1, -2)
    q_spec = pl.BlockSpec((1, tile, d), lambda i, qi: (i, qi, 0))
    return pl.pallas_call(
        functools.partial(_attn_prompt_kernel, lam_init=lam_init),
        grid=(b, n),
        in_specs=[q_spec,
                  pl.BlockSpec((1, t, d), lambda i, qi: (i, 0, 0), pipeline_mode=pl.Buffered(1)),
                  pl.BlockSpec((1,) + vt.shape[1:], lambda i, qi: (i, 0, 0, 0), pipeline_mode=pl.Buffered(1)),
                  _resident(bias.shape), _resident(lam_p.shape), _resident((dv, 1))],
        out_specs=q_spec,
        out_shape=jax.ShapeDtypeStruct((b, t, d), BF16),
        scratch_shapes=[pltpu.VMEM((n_chain, 1, tile), F32), pltpu.VMEM((n_chain, 1, tile), F32),
                        pltpu.VMEM((n_chain, ext, tile), F32),
                        pltpu.VMEM((n_chain, tile, tile), F32), pltpu.VMEM((n_chain, tile, tile), BF16)],
        compiler_params=_params("parallel", "arbitrary"),
        name="attn_prompt",
    )(q, k, vt, bias, lam_p, head_g.reshape(dv, 1))


def _attn_sample_kernel(q_ref, kc_ref, vtc_ref, kn_ref, vtn_ref, bias_c_ref, bias_n_ref, lam_ref, hg_ref,
                        o_ref, qbd_s, m_s, alpha_s, acc_s, s_scr, p_scr, *, lam_init):
    tq = q_ref.shape[1]
    tile = vtc_ref.shape[3]
    n_cache = vtc_ref.shape[1]
    ext = acc_s.shape[1]
    dv = o_ref.shape[2] // A_HEADS
    gw = qbd_s.shape[1]
    n_group = qbd_s.shape[0]
    n_sub = gw // tq
    row_blk = lax.broadcasted_iota(jnp.int32, (gw, gw), 0) // tq
    col_blk = lax.broadcasted_iota(jnp.int32, (gw, gw), 1) // (gw // n_sub)
    for g in range(n_group):
        qg = q_ref[0, :, g * gw:(g + 1) * gw]
        qrep = jnp.concatenate([qg] * n_sub, axis=0)
        qbd_s[g] = jnp.where(row_blk == col_blk, qrep, jnp.zeros_like(qrep))
    m_s[...] = jnp.full(m_s.shape, NEG_BIG, F32)
    acc_s[...] = jnp.zeros(acc_s.shape, F32)

    def kv_tile(k_tile, vt_tile, bias_ref, rows):
        for g in range(n_group):
            s_scr[g, :rows] = _dot_nt(k_tile(g), qbd_s[g])
        for g in range(n_group):
            s = s_scr[g, :rows]
            if bias_ref is not None:
                s = s + bias_ref[g]
                s_scr[g, :rows] = s
            m_prev = m_s[g]
            m_new = jnp.maximum(m_prev, jnp.max(s, axis=0, keepdims=True))
            alpha_s[g] = jnp.exp2(m_prev - m_new)
            m_s[g] = m_new
        for g in range(n_group):
            p_scr[g, :rows] = jnp.exp2(s_scr[g, :rows] - m_s[g]).astype(BF16)
        for h in range(A_HEADS):
            g, half = divmod(h, 2)
            lanes = slice(half * 2 * tq, (half + 1) * 2 * tq)
            acc_s[h] = alpha_s[g, :, lanes] * acc_s[h] + _dot(vt_tile(h), p_scr[g, :rows, lanes])

    def cache_tile(j, bias_ref):
        koff = pl.multiple_of(j * tile, tile)
        kv_tile(lambda g: kc_ref[0, pl.ds(koff, tile), g * gw:(g + 1) * gw],
                lambda h: vtc_ref[0, j, h * ext:(h + 1) * ext, :], bias_ref, tile)

    def far_body(j, carry):
        cache_tile(j, None)
        return carry

    lax.fori_loop(0, n_cache - 1, far_body, 0)
    cache_tile(n_cache - 1, bias_c_ref)
    kv_tile(lambda g: kn_ref[0, :, g * gw:(g + 1) * gw],
            lambda h: vtn_ref[0, 0, h * ext:(h + 1) * ext, :], bias_n_ref, tq)

    lp = lam_ref[...]
    lam = (jnp.exp(jnp.sum(lp[0:1] * lp[1:2], axis=-1, keepdims=True))
           - jnp.exp(jnp.sum(lp[2:3] * lp[3:4], axis=-1, keepdims=True)) + lam_init)
    for h in range(A_HEADS):
        both = (acc_s[h, :dv] / acc_s[h, dv:dv + 1]).T
        o = both[:tq] - lam * both[tq:]
        o = _rms(o, hg_ref[...]) * (1.0 - lam_init)
        o_ref[0, :, h * dv:(h + 1) * dv] = o.astype(o_ref.dtype)


def _attn_sample(q, cache_k, cache_vt, new_k, new_vt, rel_bias, lam_p, head_g, lam_init):
    b, t, d = q.shape
    past = cache_k.shape[1]
    n_cache, d_ext, tile = cache_vt.shape[1:]
    assert past == n_cache * tile and past % CHUNK == 0 and tile >= MAX_DISTANCE
    dv = d // A_HEADS
    ext = d_ext // A_HEADS
    gw = 4 * t
    assert gw == 2 * dv and A_HEADS % 2 == 0
    n_group = A_HEADS // 2

    def packed(bias):
        bt = bias.swapaxes(-1, -2).reshape(n_group, 2, -1, t)
        return jnp.concatenate([bt[:, 0], bt[:, 0], bt[:, 1], bt[:, 1]], axis=-1)

    bias_c = packed(_bias_tile(rel_bias, past, past - tile, t, tile))
    bias_n = packed(_bias_tile(rel_bias, past, past, t, t))
    tok_spec = pl.BlockSpec((1, t, d), lambda i: (i, 0, 0))
    return pl.pallas_call(
        functools.partial(_attn_sample_kernel, lam_init=lam_init),
        grid=(b,),
        in_specs=[tok_spec,
                  pl.BlockSpec((1, past, d), lambda i: (i, 0, 0)),
                  pl.BlockSpec((1, n_cache, d_ext, tile), lambda i: (i, 0, 0, 0)),
                  tok_spec,
                  pl.BlockSpec((1, 1, d_ext, t), lambda i: (i, 0, 0, 0)),
                  _resident(bias_c.shape), _resident(bias_n.shape), _resident(lam_p.shape),
                  _resident((1, dv))],
        out_specs=tok_spec,
        out_shape=jax.ShapeDtypeStruct((b, t, d), BF16),
        scratch_shapes=[pltpu.VMEM((n_group, gw, gw), BF16),
                        pltpu.VMEM((n_group, 1, gw), F32), pltpu.VMEM((n_group, 1, gw), F32),
                        pltpu.VMEM((A_HEADS, ext, 2 * t), F32),
                        pltpu.VMEM((n_group, tile, gw), F32), pltpu.VMEM((n_group, tile, gw), BF16)],
        compiler_params=_params("parallel"),
        name="attn_sample",
    )(q, cache_k, cache_vt, new_k, new_vt, bias_c, bias_n, lam_p, head_g.reshape(1, dv))


def _cache_layouts(cache_k, cache_v, tile):
    b, past, heads, two, dh = cache_k.shape
    dv = two * dh
    n = past // tile
    k = cache_k.reshape(b, past, heads * dv).astype(BF16)
    vt = cache_v.astype(BF16).reshape(b, n, tile, heads, dv).transpose(0, 1, 3, 4, 2)
    vt = jnp.concatenate([vt, jnp.ones((b, n, heads, ONES_ROWS, tile), BF16)], axis=3)
    return k, vt.reshape(b, n, heads * (dv + ONES_ROWS), tile)


def _trunk(x, mod, mstate, past, chunk, rows, p):
    b, t, d = x.shape
    depth = p["norm_g"].shape[0]
    n_a = p["mlstm_w_in"].shape[0]
    qk = M_HEADS * (d // M_HEADS // 2)
    new_c, new_n, new_m = [], [], []
    prompt = past is None
    k_f32 = v_f32 = k_bf = vt_bf = None
    for l in range(depth):
        sh1, sc1, g1, sh2, sc2, g2 = [mod[l, :, i * d:(i + 1) * d].reshape(b, 1, d) for i in range(6)]
        if l < n_a:
            w_in = p["mlstm_w_in"][l]
            w_qko = jnp.concatenate([w_in[:, :2 * qk], w_in[:, 2 * qk + d:2 * qk + 2 * d]], axis=1)
            qko, vt, g, gt = _norm_proj(x, p["norm_g"][l, 0], sh1, sc1, w_qko, rows=rows,
                                        w_feat=w_in[:, 2 * qk:2 * qk + d], w_gate=w_in[:, 2 * qk + 2 * d:])
            hn, c_l, n_l, m_l = _mlstm(qko, vt, g, gt, p["mlstm_b_gates"][l], p["mlstm_head_g"][l],
                                       mstate[0][l], mstate[1][l], mstate[2][l], chunk=chunk)
            new_c.append(c_l)
            new_n.append(n_l)
            new_m.append(m_l)
            x = _proj_residual(hn, p["mlstm_w_out"][l], x, g1, rows=rows)
        else:
            j = l - n_a
            lam_init = 0.8 - 0.6 * math.exp(-0.3 * l)
            dh = p["q_norm_g"].shape[-1]
            q = _norm_proj(x, p["norm_g"][l, 0], sh1, sc1, p["attn_w_q"][j], rows=rows,
                           qk_gain=p["q_norm_g"][j], qk_scale=dh ** -0.5 * LOG2E)
            if prompt:
                o = _attn_prompt(q, k_bf, vt_bf, p["rel_bias"], p["attn_lambda"][j], p["attn_head_g"][j],
                                 lam_init)
            else:
                o = _attn_sample(q, past[0], past[1], k_bf, vt_bf, p["rel_bias"], p["attn_lambda"][j],
                                 p["attn_head_g"][j], lam_init)
            x = _proj_residual(o, p["attn_w_o"][j], x, g1, rows=rows)
        x = _ffn(x, p["norm_g"][l, 1], sh2, sc2, g2, p["ffn_w_gate"][l], p["ffn_w_up"][l],
                 p["ffn_w_down"][l], rows=rows)
        if l == n_a - 1:
            k_f32, v_f32, k_bf, vt_bf = _shared_kv(x, p["kv_norm_g"], p["w_k"], p["w_v"], p["k_norm_g"],
                                                   rows=rows, attn_tile=ATTN_TILE if prompt else t)
    dh = p["k_norm_g"].shape[-1]
    return (x, jnp.stack(new_c), jnp.stack(new_n), jnp.stack(new_m),
            k_f32.reshape(b, t, A_HEADS, 2, dh), v_f32.reshape(b, t, A_HEADS, 2 * dh))


def kernel(x_prompt, x_sample, c_prompt, c_sample, state_mlstm_C, state_mlstm_n, state_mlstm_m, cache_k, cache_v, ada_w, ada_b, norm_g, mlstm_w_in, mlstm_b_gates, mlstm_head_g, mlstm_w_out, kv_norm_g, w_k, w_v, k_norm_g, attn_w_q, q_norm_g, attn_lambda, attn_head_g, attn_w_o, rel_bias, ffn_w_gate, ffn_w_up, ffn_w_down):
    bf = lambda w: w.astype(BF16)
    p = dict(norm_g=norm_g, mlstm_w_in=bf(mlstm_w_in), mlstm_b_gates=mlstm_b_gates,
             mlstm_head_g=mlstm_head_g, mlstm_w_out=bf(mlstm_w_out), kv_norm_g=kv_norm_g, w_k=bf(w_k),
             w_v=bf(w_v), k_norm_g=k_norm_g, attn_w_q=bf(attn_w_q), q_norm_g=q_norm_g,
             attn_lambda=attn_lambda, attn_head_g=attn_head_g, attn_w_o=bf(attn_w_o), rel_bias=rel_bias,
             ffn_w_gate=bf(ffn_w_gate), ffn_w_up=bf(ffn_w_up), ffn_w_down=bf(ffn_w_down))
    bp = x_prompt.shape[0]
    n_a, _, heads, dk, dv = state_mlstm_C.shape
    mod = _adaln(jnp.concatenate([c_prompt, c_sample], axis=0), ada_w, ada_b)
    zero_state = (jnp.zeros((n_a, bp, heads, dk, dv), F32), jnp.zeros((n_a, bp, heads, dk), F32),
                  jnp.zeros((n_a, bp, heads), F32))
    y_p, c_p, n_p, m_p, k_p, v_p = _trunk(x_prompt, mod[:, :bp], zero_state, None, MLSTM_PROMPT_CHUNK, 512, p)
    y_s, c_s, n_s, m_s, k_s, v_s = _trunk(x_sample, mod[:, bp:], (state_mlstm_C, state_mlstm_n, state_mlstm_m),
                                          _cache_layouts(cache_k, cache_v, ATTN_TILE), x_sample.shape[1], 512, p)
    return (y_p, y_s, c_p, n_p, m_p, k_p, v_p, c_s, n_s, m_s, k_s, v_s)
```

```python
import functools
import math

import jax
import jax.numpy as jnp
from jax import lax
from jax.experimental import pallas as pl
from jax.experimental.pallas import tpu as pltpu

F32 = jnp.float32
BF16 = jnp.bfloat16

EPS = 1e-6
NEG_BIG = -1e30
CHUNK = 64
N_BUCKETS = 32
MAX_DISTANCE = 128
M_HEADS = 8
A_HEADS = 8
GATE_PAD = 128
LANES = 128
SUBLANES = 8
ONES_ROWS = 16
LOG2E = math.log2(math.e)

V7X_VMEM_BYTES = 64 * 1024 * 1024
VMEM_LIMIT = V7X_VMEM_BYTES - 8 * 1024 * 1024

ATTN_TILE = 256
MLSTM_PROMPT_CHUNK = 256
FFN_CHUNK = 256
ROW_CHUNK = 64


def _params(*sem, flags=None):
    return pltpu.CompilerParams(dimension_semantics=sem, vmem_limit_bytes=VMEM_LIMIT, flags=flags)


def _rms(x, g):
    return x * lax.rsqrt(jnp.mean(x * x, axis=-1, keepdims=True) + EPS) * g


def _log_sigmoid(x):
    return jnp.minimum(x, 0.0) - jnp.log1p(jnp.exp(-jnp.abs(x)))


def _dot(a, b):
    return jnp.dot(a, b, preferred_element_type=F32)


def _dot_nt(a, b):
    return lax.dot_general(a, b, (((1,), (1,)), ((), ())), preferred_element_type=F32)


def _dot_tn(a, b):
    return lax.dot_general(a, b, (((0,), (0,)), ((), ())), preferred_element_type=F32)


def _resident(shape):
    nd = len(shape)
    return pl.BlockSpec(shape, lambda *_: (0,) * nd, pipeline_mode=pl.Buffered(1))


def _adaln_kernel(c_ref, w_ref, b_ref, o_ref):
    c = c_ref[...]
    a = (c * jax.nn.sigmoid(c)).astype(BF16)
    o_ref[0] = _dot(a, w_ref[0].astype(BF16)) + b_ref[0]


def _adaln(c, ada_w, ada_b):
    depth, d, n = ada_w.shape
    rows = c.shape[0]
    tn = 1536
    assert n % tn == 0
    return pl.pallas_call(
        _adaln_kernel,
        grid=(depth, n // tn),
        in_specs=[
            pl.BlockSpec((rows, d), lambda l, j: (0, 0)),
            pl.BlockSpec((1, d, tn), lambda l, j: (l, 0, j)),
            pl.BlockSpec((1, 1, tn), lambda l, j: (l, 0, j)),
        ],
        out_specs=pl.BlockSpec((1, rows, tn), lambda l, j: (l, 0, j)),
        out_shape=jax.ShapeDtypeStruct((depth, rows, n), F32),
        compiler_params=_params("parallel", "parallel"),
        name="adaln",
    )(c, ada_w, ada_b.reshape(depth, 1, n))


def _modulated(x_ref, g_ref, sh_ref, sc_ref):
    x = x_ref[...]
    bb, tt, d = x.shape
    xm = _rms(x, g_ref[...]) * (1.0 + sc_ref[...]) + sh_ref[...]
    return xm.reshape(bb * tt, d).astype(BF16)


def _group_rms_store(src_ref, dst_ref, gain_ref, post_scale, width):
    bb, tt, n = dst_ref.shape
    for g in range(n // width):
        seg = src_ref[:, g * width:(g + 1) * width]
        y = _rms(seg, gain_ref[...]) * post_scale
        dst_ref[:, :, g * width:(g + 1) * width] = y.reshape(bb, tt, width).astype(dst_ref.dtype)


def _store_values_ext(dst_ref, lead, vt, col, width, heads):
    dv = vt.shape[0] // heads
    ext = dv + ONES_ROWS
    ones = jnp.ones((ONES_ROWS, width), BF16)
    for hh in range(heads):
        dst_ref[lead + (slice(hh * ext, hh * ext + dv), slice(None))] = vt[hh * dv:(hh + 1) * dv, col:col + width]
        dst_ref[lead + (slice(hh * ext + dv, (hh + 1) * ext), slice(None))] = ones


def _norm_proj_kernel(*refs, n_chunk, mlstm, qk_scale):
    x_ref, g_ref, sh_ref, sc_ref, w_ref = refs[:5]
    pos = 5
    if mlstm:
        wf_ref, wg_ref, wgt_ref = refs[pos:pos + 3]
        pos += 3
    if qk_scale is not None:
        qg_ref = refs[pos]
        pos += 1
    o_ref = refs[pos]
    pos += 1
    if mlstm:
        of_ref, og_ref, ogt_ref = refs[pos:pos + 3]
        pos += 3
    if qk_scale is not None:
        y_scr = refs[pos]

    bb, tt, n = o_ref.shape
    xm = _modulated(x_ref, g_ref, sh_ref, sc_ref)
    for c in range(n // n_chunk):
        y = _dot(xm, w_ref[:, c * n_chunk:(c + 1) * n_chunk])
        if qk_scale is None:
            o_ref[:, :, c * n_chunk:(c + 1) * n_chunk] = y.reshape(bb, tt, n_chunk).astype(o_ref.dtype)
        else:
            y_scr[:, c * n_chunk:(c + 1) * n_chunk] = y
    if qk_scale is not None:
        _group_rms_store(y_scr, o_ref, qg_ref, qk_scale, qg_ref.shape[-1])
    if mlstm:
        vt = _dot_nt(wf_ref[...], xm).astype(BF16)
        for i in range(bb):
            _store_values_ext(of_ref, (i,), vt, i * tt, tt, M_HEADS)
            ogt_ref[i] = _dot_nt(wgt_ref[...], xm[i * tt:(i + 1) * tt])
        og_ref[...] = _dot(xm, wg_ref[...]).reshape(bb, tt, 2 * GATE_PAD)


def _row_blocks(b, t, rows):
    if t >= rows:
        assert t % rows == 0
        return 1, rows
    assert rows % t == 0 and b % (rows // t) == 0
    return rows // t, t


def _norm_proj(x, gain, shift, scale, w, *, rows, w_feat=None, w_gate=None, qk_gain=None, qk_scale=None):
    b, t, d = x.shape
    n = w.shape[1]
    bb, tt = _row_blocks(b, t, rows)
    grid = (b // bb, t // tt)
    tok = lambda i, j: (i, j, 0)
    per_b = lambda i, j: (i, 0, 0)
    in_specs = [
        pl.BlockSpec((bb, tt, d), tok),
        _resident((1, d)),
        pl.BlockSpec((bb, 1, d), per_b),
        pl.BlockSpec((bb, 1, d), per_b),
        _resident((d, n)),
    ]
    args = [x, gain.reshape(1, d), shift, scale, w]
    out_specs = [pl.BlockSpec((bb, tt, n), tok)]
    out_shape = [jax.ShapeDtypeStruct((b, t, n), BF16)]
    scratch = []
    mlstm = w_gate is not None
    if mlstm:
        ng = w_gate.shape[1]
        half = ng // 2
        wg = (jnp.zeros((d, 2 * GATE_PAD), BF16).at[:, :half].set(w_gate[:, :half])
              .at[:, GATE_PAD:GATE_PAD + half].set(w_gate[:, half:]))
        n_feat = w_feat.shape[1]
        d_ext = n_feat + M_HEADS * ONES_ROWS
        in_specs += [_resident((n_feat, d)), _resident((d, 2 * GATE_PAD)), _resident((ng, d))]
        args += [w_feat.T, wg, w_gate.T]
        out_specs += [pl.BlockSpec((bb, d_ext, tt), lambda i, j: (i, 0, j)),
                      pl.BlockSpec((bb, tt, 2 * GATE_PAD), tok),
                      pl.BlockSpec((bb, ng, tt), lambda i, j: (i, 0, j))]
        out_shape += [jax.ShapeDtypeStruct((b, d_ext, t), BF16),
                      jax.ShapeDtypeStruct((b, t, 2 * GATE_PAD), F32),
                      jax.ShapeDtypeStruct((b, ng, t), F32)]
    if qk_scale is not None:
        in_specs.append(_resident((1, qk_gain.shape[-1])))
        args.append(qk_gain.reshape(1, -1))
        scratch.append(pltpu.VMEM((bb * tt, n), F32))
    kern = functools.partial(_norm_proj_kernel, n_chunk=min(n, 1024), mlstm=mlstm, qk_scale=qk_scale)
    out = pl.pallas_call(
        kern, grid=grid, in_specs=in_specs, out_specs=out_specs, out_shape=out_shape,
        scratch_shapes=scratch, compiler_params=_params("parallel", "parallel"), name="norm_proj",
    )(*args)
    return out if mlstm else out[0]


def _kv_kernel(x_ref, g_ref, wk_ref, wv_ref, wvt_ref, kg_ref, k_ref, v_ref, kb_ref, vt_ref, y_scr):
    x = x_ref[...]
    bb, tt, d = x.shape
    h = _rms(x, g_ref[...]).reshape(bb * tt, d).astype(BF16)
    y_scr[...] = _dot(h, wk_ref[...])
    _group_rms_store(y_scr, k_ref, kg_ref, 1.0, kg_ref.shape[-1])
    v_ref[...] = _dot(h, wv_ref[...]).reshape(bb, tt, d)
    kb_ref[...] = k_ref[...].astype(BF16)
    vt = _dot_nt(wvt_ref[...], h).astype(BF16)
    per, tile = vt_ref.shape[1], vt_ref.shape[3]
    for i in range(bb):
        for j in range(per):
            _store_values_ext(vt_ref, (i, j), vt, (i * per + j) * tile, tile, A_HEADS)


def _shared_kv(x, kv_g, w_k, w_v, k_g, *, rows, attn_tile):
    b, t, d = x.shape
    bb, tt = _row_blocks(b, t, rows)
    assert tt % attn_tile == 0
    per = tt // attn_tile
    d_ext = d + A_HEADS * ONES_ROWS
    tok = lambda i, j: (i, j, 0)
    blk = pl.BlockSpec((bb, tt, d), tok)
    return pl.pallas_call(
        _kv_kernel,
        grid=(b // bb, t // tt),
        in_specs=[blk, _resident((1, d)), _resident((d, d)), _resident((d, d)), _resident((d, d)),
                  _resident((1, k_g.shape[-1]))],
        out_specs=[blk, blk, blk, pl.BlockSpec((bb, per, d_ext, attn_tile), lambda i, j: (i, j, 0, 0))],
        out_shape=[jax.ShapeDtypeStruct((b, t, d), F32), jax.ShapeDtypeStruct((b, t, d), F32),
                   jax.ShapeDtypeStruct((b, t, d), BF16),
                   jax.ShapeDtypeStruct((b, t // attn_tile, d_ext, attn_tile), BF16)],
        scratch_shapes=[pltpu.VMEM((bb * tt, d), F32)],
        compiler_params=_params("parallel", "parallel"),
        name="shared_kv",
    )(x, kv_g.reshape(1, d), w_k, w_v, w_v.T, k_g.reshape(1, -1))


def _mixer_ffn_kernel(x_ref, a_ref, wp_ref, gate1_ref, g_ref, sh_ref, sc_ref, gate2_ref, wg_ref, wu_ref, wd_ref,
                      o_ref, acc_ref):
    bb, tt, d = x_ref.shape
    f = wg_ref.shape[1]
    a = a_ref[...].reshape(bb * tt, a_ref.shape[-1])
    o_ref[...] = x_ref[...] + gate1_ref[...] * _dot(a, wp_ref[...]).reshape(bb, tt, d)
    xm = _modulated(o_ref, g_ref, sh_ref, sc_ref)
    for c in range(f // FFN_CHUNK):
        sl = slice(c * FFN_CHUNK, (c + 1) * FFN_CHUNK)
        hg = _dot(xm, wg_ref[:, sl])
        hu = _dot(xm, wu_ref[:, sl])
        act = (hg * jax.nn.sigmoid(hg) * hu).astype(BF16)
        part = _dot(act, wd_ref[sl, :])
        if c == 0:
            acc_ref[...] = part
        else:
            acc_ref[...] += part
    o_ref[...] += gate2_ref[...] * acc_ref[...].reshape(bb, tt, d)


def _mixer_ffn(x, a, w_proj, gate1, gain, shift, scale, gate2, w_gate, w_up, w_down, *, rows):
    b, t, d = x.shape
    k = a.shape[-1]
    f = w_gate.shape[1]
    assert f % FFN_CHUNK == 0
    bb, tt = _row_blocks(b, t, rows)
    tok = lambda i, j: (i, j, 0)
    per_b = pl.BlockSpec((bb, 1, d), lambda i, j: (i, 0, 0))
    return pl.pallas_call(
        _mixer_ffn_kernel,
        grid=(b // bb, t // tt),
        in_specs=[pl.BlockSpec((bb, tt, d), tok), pl.BlockSpec((bb, tt, k), tok), _resident((k, d)), per_b,
                  _resident((1, d)), per_b, per_b, per_b,
                  _resident((d, f)), _resident((d, f)), _resident((f, d))],
        out_specs=pl.BlockSpec((bb, tt, d), tok),
        out_shape=jax.ShapeDtypeStruct((b, t, d), F32),
        scratch_shapes=[pltpu.VMEM((bb * tt, d), F32)],
        compiler_params=_params("parallel", "parallel"),
        name="mixer_ffn",
    )(x, a, w_proj, gate1, gain.reshape(1, d), shift, scale, gate2, w_gate, w_up, w_down)


def _mlstm_kernel(q_ref, k_ref, o_ref, vt_ref, g_ref, gt_ref, brow_ref, bcol_ref, hg_ref,
                  c0_ref, n0_ref, m0_ref, h_ref, c_out, n_out, m_out,
                  ct_s, m_s, mnew_s, acol_s, br_s, inter_s, wn_s, wold_s, qk_s, qc_s, sb_s, sv_s, stat_s):
    ci = pl.program_id(1)
    length = q_ref.shape[1]
    ext = ct_s.shape[1]
    dv = ext - ONES_ROWS
    dk = q_ref.shape[2] // M_HEADS
    heads = range(M_HEADS)

    @pl.when(ci == 0)
    def _():
        ct_s[...] = jnp.zeros(ct_s.shape, F32)
        for h in heads:
            c0 = jnp.concatenate([c0_ref[0, h], jnp.zeros((dv - dk, dv), F32)], axis=0)
            ct_s[h, :dv, :] = c0.T
            ct_s[h, dv:, :dk] = jnp.broadcast_to(n0_ref[0, h:h + 1, :], (ONES_ROWS, dk))
        m_s[...] = m0_ref[0]

    row = lax.broadcasted_iota(jnp.int32, (length, length), 0)
    col = lax.broadcasted_iota(jnp.int32, (length, length), 1)
    causal = row <= col
    hi = lax.Precision.HIGHEST
    g = g_ref[0] + brow_ref[...]
    b_cols = jnp.dot((col <= row).astype(F32), _log_sigmoid(g[:, GATE_PAD:]), precision=hi,
                     preferred_element_type=F32)
    acol_s[...] = g[:, :GATE_PAD] - b_cols
    gt = gt_ref[0] + bcol_ref[...]
    ig_rows = gt[:M_HEADS]
    br_s[...] = jnp.dot(_log_sigmoid(gt[M_HEADS:]), causal.astype(F32), precision=hi,
                        preferred_element_type=F32)
    b_rows = br_s[...]
    b_end = br_s[:, length - 1:length]
    m_prev = m_s[...]
    g_rows = b_end - b_rows + ig_rows
    m_new = jnp.maximum(b_end + m_prev, jnp.max(g_rows, axis=-1, keepdims=True))
    mnew_s[...] = m_new
    wold_s[...] = jnp.exp(b_end + m_prev - m_new)
    wn_s[...] = jnp.exp(g_rows - m_new)
    inter_s[...] = b_rows + m_prev

    def q_of(h):
        return q_ref[0, :, h * dk:(h + 1) * dk]

    def k_of(h):
        return k_ref[0, :, h * dk:(h + 1) * dk] * (dk ** -0.5)

    def vt_of(h):
        return vt_ref[0, h * ext:(h + 1) * ext, :]

    for h in heads:
        qk_s[h] = _dot_nt(k_of(h), q_of(h))
        qc_s[h] = _dot_nt(ct_s[h, :, :dk].astype(BF16), q_of(h))
    for h in heads:
        inter = inter_s[h:h + 1, :]
        dmat = jnp.where(causal, acol_s[:, h:h + 1] + br_s[h:h + 1, :], -jnp.inf)
        m_t = jnp.maximum(inter, jnp.max(dmat, axis=0, keepdims=True))
        sb_s[h] = (qk_s[h] * jnp.exp(dmat - m_t)).astype(BF16)
        stat_s[0, h] = jnp.exp(inter - m_t)
        stat_s[1, h] = jnp.exp(-m_t)
    for h in heads:
        sv_s[h] = _dot(vt_of(h), sb_s[h])
    for h in heads:
        w_inter = stat_s[0, h]
        tot = w_inter * qc_s[h] + sv_s[h]
        hh = tot[:dv] / jnp.maximum(jnp.abs(tot[dv:dv + 1]), stat_s[1, h])
        hh = hh * lax.rsqrt(jnp.mean(hh * hh, axis=0, keepdims=True) + EPS)
        og = o_ref[0, :, h * dv:(h + 1) * dv].astype(F32)
        h_ref[0, :, h * dv:(h + 1) * dv] = (_to_token_major(hh) * hg_ref[:, h * dv:(h + 1) * dv]
                                            * jax.nn.sigmoid(og)).astype(h_ref.dtype)
    for h in heads:
        vtw = (vt_of(h).astype(F32) * wn_s[h:h + 1, :]).astype(BF16)
        ct_s[h, :, :dk] = wold_s[h:h + 1, :] * ct_s[h, :, :dk] + _dot(vtw, k_of(h))
    m_s[...] = mnew_s[...]

    @pl.when(ci == pl.num_programs(1) - 1)
    def _():
        for h in heads:
            c_out[0, h] = ct_s[h, :dv, :].T[:dk]
            n_out[0, h:h + 1, :] = ct_s[h, dv:dv + 1, :dk]
        m_out[0] = m_s[...]


def _to_token_major(x):
    dv, length = x.shape
    if length % LANES:
        pad = LANES - length % LANES
        x = jnp.concatenate([x, jnp.zeros((dv, pad), x.dtype)], axis=1)
    return x.T[:length]


def _mlstm(qko, vt, g, gt, b_gates, head_g, c0, n0, m0, *, chunk):
    b, t, _ = qko.shape
    heads, dk, dv = c0.shape[1:]
    d = heads * dv
    qk = heads * dk
    ext = dv + ONES_ROWS
    assert qk * 2 == d and t % chunk == 0 and heads == M_HEADS and vt.shape[1] == heads * ext
    nc = t // chunk
    two_h = 2 * heads
    brow = (jnp.zeros((1, 2 * GATE_PAD), F32).at[0, :heads].set(b_gates[:heads])
            .at[0, GATE_PAD:GATE_PAD + heads].set(b_gates[heads:]))
    st4 = lambda i, c: (i, 0, 0, 0)
    st3 = lambda i, c: (i, 0, 0)
    outs = pl.pallas_call(
        _mlstm_kernel,
        grid=(b, nc),
        in_specs=[
            pl.BlockSpec((1, chunk, qk), lambda i, c: (i, c, 0)),
            pl.BlockSpec((1, chunk, qk), lambda i, c: (i, c, 1)),
            pl.BlockSpec((1, chunk, d), lambda i, c: (i, c, 1)),
            pl.BlockSpec((1, heads * ext, chunk), lambda i, c: (i, 0, c)),
            pl.BlockSpec((1, chunk, 2 * GATE_PAD), lambda i, c: (i, c, 0)),
            pl.BlockSpec((1, two_h, chunk), lambda i, c: (i, 0, c)),
            _resident((1, 2 * GATE_PAD)), _resident((two_h, 1)), _resident((1, d)),
            pl.BlockSpec((1, heads, dk, dv), st4),
            pl.BlockSpec((1, heads, dk), st3),
            pl.BlockSpec((1, heads, 1), st3),
        ],
        out_specs=[
            pl.BlockSpec((1, chunk, d), lambda i, c: (i, c, 0)),
            pl.BlockSpec((1, heads, dk, dv), st4),
            pl.BlockSpec((1, heads, dk), st3),
            pl.BlockSpec((1, heads, 1), st3),
        ],
        out_shape=[
            jax.ShapeDtypeStruct((b, t, d), BF16),
            jax.ShapeDtypeStruct((b, heads, dk, dv), F32),
            jax.ShapeDtypeStruct((b, heads, dk), F32),
            jax.ShapeDtypeStruct((b, heads, 1), F32),
        ],
        scratch_shapes=[pltpu.VMEM((heads, ext, dv), F32),
                        pltpu.VMEM((heads, 1), F32), pltpu.VMEM((heads, 1), F32),
                        pltpu.VMEM((chunk, GATE_PAD), F32),
                        pltpu.VMEM((heads, chunk), F32), pltpu.VMEM((heads, chunk), F32),
                        pltpu.VMEM((heads, chunk), F32), pltpu.VMEM((heads, 1), F32),
                        pltpu.VMEM((heads, chunk, chunk), F32),
                        pltpu.VMEM((heads, ext, chunk), F32),
                        pltpu.VMEM((heads, chunk, chunk), BF16),
                        pltpu.VMEM((heads, ext, chunk), F32),
                        pltpu.VMEM((2, heads, 1, chunk), F32)],
        compiler_params=_params("parallel", "arbitrary"),
        name="mlstm",
    )(qko, qko, qko, vt, g, gt, brow, b_gates.reshape(two_h, 1), head_g.reshape(1, d),
      c0, n0, m0.reshape(b, heads, 1))
    hn, c_new, n_new, m_new = outs
    return hn, c_new, n_new, m_new.reshape(b, heads)


def _rel_bucket(rel):
    nb = N_BUCKETS // 2
    max_exact = nb // 2
    ret = jnp.where(rel > 0, nb, 0)
    n = jnp.abs(rel)
    large = max_exact + (jnp.log(jnp.maximum(n, 1).astype(F32) / max_exact)
                         / math.log(MAX_DISTANCE / max_exact) * (nb - max_exact)).astype(jnp.int32)
    large = jnp.minimum(large, nb - 1)
    return ret + jnp.where(n < max_exact, n, large)


def _bias_tile(rel_bias, q0, k0, tq, tk):
    table = rel_bias.astype(F32)
    span = tq + tk
    r = jnp.arange(span)
    r = jnp.where(r < tk, r, r - span)
    vec = (table[_rel_bucket(r + (k0 - q0))] - table[N_BUCKETS // 2 - 1]).T
    bias = jnp.tile(vec, (1, tq))[:, :tq * (span - 1)].reshape(-1, tq, span - 1)[:, :, :tk]
    q_chunk = (q0 + lax.broadcasted_iota(jnp.int32, (tq, tk), 0)) // CHUNK
    k_chunk = (k0 + lax.broadcasted_iota(jnp.int32, (tq, tk), 1)) // CHUNK
    return jnp.where((k_chunk <= q_chunk)[None], bias * LOG2E, NEG_BIG)


def _attn_prompt_kernel(q_ref, k_ref, vt_ref, bias_ref, lam_ref, hg_ref, o_ref,
                        m_s, alpha_s, acc_s, s_scr, p_scr, *, lam_init):
    qi = pl.program_id(1)
    tile = q_ref.shape[1]
    ext = acc_s.shape[1]
    dv = o_ref.shape[2] // A_HEADS
    dh = dv // 2
    n_chain = 2 * A_HEADS
    m_s[...] = jnp.full(m_s.shape, NEG_BIG, F32)
    acc_s[...] = jnp.zeros(acc_s.shape, F32)

    def kv_tile(j, slot):
        koff = pl.multiple_of(j * tile, tile)
        for hc in range(n_chain):
            lo = hc * dh
            s_scr[hc] = _dot_nt(k_ref[0, pl.ds(koff, tile), lo:lo + dh], q_ref[0, :, lo:lo + dh])
        for hc in range(n_chain):
            m_prev = m_s[hc]
            m_part = None
            for r in range(0, tile, ROW_CHUNK):
                s = s_scr[hc, r:r + ROW_CHUNK]
                if slot is not None:
                    s = s + bias_ref[hc // 2, slot, r:r + ROW_CHUNK]
                    s_scr[hc, r:r + ROW_CHUNK] = s
                part = jnp.max(s.reshape(ROW_CHUNK // SUBLANES, SUBLANES, tile), axis=0)
                m_part = part if m_part is None else jnp.maximum(m_part, part)
            m_new = jnp.maximum(m_prev, jnp.max(m_part, axis=0, keepdims=True))
            alpha_s[hc] = jnp.exp2(m_prev - m_new)
            m_s[hc] = m_new
        for hc in range(n_chain):
            for r in range(0, tile, ROW_CHUNK):
                p_scr[hc, r:r + ROW_CHUNK] = jnp.exp2(s_scr[hc, r:r + ROW_CHUNK] - m_s[hc]).astype(BF16)
        for hc in range(n_chain):
            h = hc // 2
            acc_s[hc] = alpha_s[hc] * acc_s[hc] + _dot(vt_ref[0, j, h * ext:(h + 1) * ext, :], p_scr[hc])

    def far_body(j, carry):
        kv_tile(j, None)
        return carry

    lax.fori_loop(0, jnp.maximum(qi - 1, 0), far_body, 0)

    @pl.when(qi > 0)
    def _():
        kv_tile(qi - 1, 0)

    kv_tile(qi, 1)

    lp = lam_ref[...]
    lam = (jnp.exp(jnp.sum(lp[0:1] * lp[1:2], axis=-1, keepdims=True))
           - jnp.exp(jnp.sum(lp[2:3] * lp[3:4], axis=-1, keepdims=True)) + lam_init)
    for h in range(A_HEADS):
        o = (acc_s[2 * h, :dv] / acc_s[2 * h, dv:dv + 1]
             - lam * (acc_s[2 * h + 1, :dv] / acc_s[2 * h + 1, dv:dv + 1]))
        o = o * lax.rsqrt(jnp.mean(o * o, axis=0, keepdims=True) + EPS) * hg_ref[...] * (1.0 - lam_init)
        o_ref[0, :, h * dv:(h + 1) * dv] = o.T.astype(o_ref.dtype)


def _attn_prompt(q, k, vt, rel_bias, lam_p, head_g, lam_init):
    b, t, d = q.shape
    tile = ATTN_TILE
    assert t % tile == 0 and tile % CHUNK == 0 and tile >= MAX_DISTANCE and vt.shape[-1] == tile
    n = t // tile
    dv = d // A_HEADS
    ext = vt.shape[2] // A_HEADS
    n_chain = 2 * A_HEADS
    bias = jnp.stack([_bias_tile(rel_bias, tile, 0, tile, tile), _bias_tile(rel_bias, 0, 0, tile, tile)],
                     axis=1).swapaxes(-1, -2)
    q_spec = pl.BlockSpec((1, tile, d), lambda i, qi: (i, qi, 0))
    return pl.pallas_call(
        functools.partial(_attn_prompt_kernel, lam_init=lam_init),
        grid=(b, n),
        in_specs=[q_spec,
                  pl.BlockSpec((1, t, d), lambda i, qi: (i, 0, 0), pipeline_mode=pl.Buffered(1)),
                  pl.BlockSpec((1,) + vt.shape[1:], lambda i, qi: (i, 0, 0, 0), pipeline_mode=pl.Buffered(1)),
                  _resident(bias.shape), _resident(lam_p.shape), _resident((dv, 1))],
        out_specs=q_spec,
        out_shape=jax.ShapeDtypeStruct((b, t, d), BF16),
        scratch_shapes=[pltpu.VMEM((n_chain, 1, tile), F32), pltpu.VMEM((n_chain, 1, tile), F32),
                        pltpu.VMEM((n_chain, ext, tile), F32),
                        pltpu.VMEM((n_chain, tile, tile), F32), pltpu.VMEM((n_chain, tile, tile), BF16)],
        compiler_params=_params("parallel", "arbitrary"),
        name="attn_prompt",
    )(q, k, vt, bias, lam_p, head_g.reshape(dv, 1))


def _attn_sample_kernel(q_ref, kc_ref, vtc_ref, kn_ref, vtn_ref, bias_c_ref, bias_n_ref, lam_ref, hg_ref,
                        o_ref, qbd_s, m_s, alpha_s, acc_s, s_scr, p_scr, *, lam_init):
    tq = q_ref.shape[1]
    tile = vtc_ref.shape[3]
    n_cache = vtc_ref.shape[1]
    ext = acc_s.shape[1]
    dv = o_ref.shape[2] // A_HEADS
    gw = qbd_s.shape[1]
    n_group = qbd_s.shape[0]
    n_sub = gw // tq
    row_blk = lax.broadcasted_iota(jnp.int32, (gw, gw), 0) // tq
    col_blk = lax.broadcasted_iota(jnp.int32, (gw, gw), 1) // (gw // n_sub)
    for g in range(n_group):
        qg = q_ref[0, :, g * gw:(g + 1) * gw]
        qrep = jnp.concatenate([qg] * n_sub, axis=0)
        qbd_s[g] = jnp.where(row_blk == col_blk, qrep, jnp.zeros_like(qrep))
    m_s[...] = jnp.full(m_s.shape, NEG_BIG, F32)
    acc_s[...] = jnp.zeros(acc_s.shape, F32)

    def kv_tile(k_tile, vt_tile, bias_ref, rows):
        for g in range(n_group):
            s_scr[g, :rows] = _dot_nt(k_tile(g), qbd_s[g])
        for g in range(n_group):
            s = s_scr[g, :rows]
            if bias_ref is not None:
                s = s + bias_ref[g]
                s_scr[g, :rows] = s
            m_prev = m_s[g]
            m_new = jnp.maximum(m_prev, jnp.max(s, axis=0, keepdims=True))
            alpha_s[g] = jnp.exp2(m_prev - m_new)
            m_s[g] = m_new
        for g in range(n_group):
            for r in range(0, rows, ROW_CHUNK):
                p_scr[g, r:r + ROW_CHUNK] = jnp.exp2(s_scr[g, r:r + ROW_CHUNK] - m_s[g]).astype(BF16)
        for h in range(A_HEADS):
            g, half = divmod(h, 2)
            lanes = slice(half * 2 * tq, (half + 1) * 2 * tq)
            acc_s[h] = alpha_s[g, :, lanes] * acc_s[h] + _dot(vt_tile(h), p_scr[g, :rows, lanes])

    ones = jnp.ones((ext - dv, tile), BF16)

    def cache_tile(j, bias_ref):
        koff = pl.multiple_of(j * tile, tile)
        kv_tile(lambda g: kc_ref[0, pl.ds(koff, tile), g * gw:(g + 1) * gw],
                lambda h: jnp.concatenate([vtc_ref[0, j, h * dv:(h + 1) * dv, :], ones], axis=0),
                bias_ref, tile)

    def far_body(j, carry):
        cache_tile(j, None)
        return carry

    lax.fori_loop(0, n_cache - 1, far_body, 0)
    cache_tile(n_cache - 1, bias_c_ref)
    kv_tile(lambda g: kn_ref[0, :, g * gw:(g + 1) * gw],
            lambda h: vtn_ref[0, 0, h * ext:(h + 1) * ext, :], bias_n_ref, tq)

    lp = lam_ref[...]
    lam = (jnp.exp(jnp.sum(lp[0:1] * lp[1:2], axis=-1, keepdims=True))
           - jnp.exp(jnp.sum(lp[2:3] * lp[3:4], axis=-1, keepdims=True)) + lam_init)
    for h in range(A_HEADS):
        both = (acc_s[h, :dv] / acc_s[h, dv:dv + 1]).T
        o = both[:tq] - lam * both[tq:]
        o = _rms(o, hg_ref[...]) * (1.0 - lam_init)
        o_ref[0, :, h * dv:(h + 1) * dv] = o.astype(o_ref.dtype)


def _attn_sample(q, cache_k, cache_vt, new_k, new_vt, rel_bias, lam_p, head_g, lam_init):
    b, t, d = q.shape
    past = cache_k.shape[1]
    n_cache, _, tile = cache_vt.shape[1:]
    d_ext = new_vt.shape[2]
    assert past == n_cache * tile and past % CHUNK == 0 and tile >= MAX_DISTANCE
    dv = d // A_HEADS
    ext = d_ext // A_HEADS
    gw = 4 * t
    assert gw == 2 * dv and A_HEADS % 2 == 0
    n_group = A_HEADS // 2

    def packed(bias):
        bt = bias.swapaxes(-1, -2).reshape(n_group, 2, -1, t)
        return jnp.concatenate([bt[:, 0], bt[:, 0], bt[:, 1], bt[:, 1]], axis=-1)

    bias_c = packed(_bias_tile(rel_bias, past, past - tile, t, tile))
    bias_n = packed(_bias_tile(rel_bias, past, past, t, t))
    tok_spec = pl.BlockSpec((1, t, d), lambda i: (i, 0, 0))
    return pl.pallas_call(
        functools.partial(_attn_sample_kernel, lam_init=lam_init),
        grid=(b,),
        in_specs=[tok_spec,
                  pl.BlockSpec((1, past, d), lambda i: (i, 0, 0)),
                  pl.BlockSpec((1, n_cache, d, tile), lambda i: (i, 0, 0, 0)),
                  tok_spec,
                  pl.BlockSpec((1, 1, d_ext, t), lambda i: (i, 0, 0, 0)),
                  _resident(bias_c.shape), _resident(bias_n.shape), _resident(lam_p.shape),
                  _resident((1, dv))],
        out_specs=tok_spec,
        out_shape=jax.ShapeDtypeStruct((b, t, d), BF16),
        scratch_shapes=[pltpu.VMEM((n_group, gw, gw), BF16),
                        pltpu.VMEM((n_group, 1, gw), F32), pltpu.VMEM((n_group, 1, gw), F32),
                        pltpu.VMEM((A_HEADS, ext, 2 * t), F32),
                        pltpu.VMEM((n_group, tile, gw), F32), pltpu.VMEM((n_group, tile, gw), BF16)],
        compiler_params=_params("parallel"),
        name="attn_sample",
    )(q, cache_k, cache_vt, new_k, new_vt, bias_c, bias_n, lam_p, head_g.reshape(1, dv))


def _cache_layouts(cache_k, cache_v, tile):
    b, past, heads, two, dh = cache_k.shape
    dv = two * dh
    n = past // tile
    k = cache_k.reshape(b, past, heads * dv).astype(BF16)
    vt = cache_v.reshape(b, n, tile, heads * dv).swapaxes(-1, -2).astype(BF16)
    return k, vt


def _trunk(x, mod, mstate, past, chunk, rows, p):
    b, t, d = x.shape
    depth = p["norm_g"].shape[0]
    n_a = p["mlstm_w_in"].shape[0]
    qk = M_HEADS * (d // M_HEADS // 2)
    new_c, new_n, new_m = [], [], []
    prompt = past is None
    k_f32 = v_f32 = k_bf = vt_bf = None
    for l in range(depth):
        sh1, sc1, g1, sh2, sc2, g2 = [mod[l, :, i * d:(i + 1) * d].reshape(b, 1, d) for i in range(6)]
        if l < n_a:
            w_in = p["mlstm_w_in"][l]
            w_qko = jnp.concatenate([w_in[:, :2 * qk], w_in[:, 2 * qk + d:2 * qk + 2 * d]], axis=1)
            qko, vt, g, gt = _norm_proj(x, p["norm_g"][l, 0], sh1, sc1, w_qko, rows=rows,
                                        w_feat=w_in[:, 2 * qk:2 * qk + d], w_gate=w_in[:, 2 * qk + 2 * d:])
            hn, c_l, n_l, m_l = _mlstm(qko, vt, g, gt, p["mlstm_b_gates"][l], p["mlstm_head_g"][l],
                                       mstate[0][l], mstate[1][l], mstate[2][l], chunk=chunk)
            new_c.append(c_l)
            new_n.append(n_l)
            new_m.append(m_l)
            mixed, w_proj = hn, p["mlstm_w_out"][l]
        else:
            j = l - n_a
            lam_init = 0.8 - 0.6 * math.exp(-0.3 * l)
            dh = p["q_norm_g"].shape[-1]
            q = _norm_proj(x, p["norm_g"][l, 0], sh1, sc1, p["attn_w_q"][j], rows=rows,
                           qk_gain=p["q_norm_g"][j], qk_scale=dh ** -0.5 * LOG2E)
            if prompt:
                o = _attn_prompt(q, k_bf, vt_bf, p["rel_bias"], p["attn_lambda"][j], p["attn_head_g"][j],
                                 lam_init)
            else:
                o = _attn_sample(q, past[0], past[1], k_bf, vt_bf, p["rel_bias"], p["attn_lambda"][j],
                                 p["attn_head_g"][j], lam_init)
            mixed, w_proj = o, p["attn_w_o"][j]
        x = _mixer_ffn(x, mixed, w_proj, g1, p["norm_g"][l, 1], sh2, sc2, g2, p["ffn_w_gate"][l],
                       p["ffn_w_up"][l], p["ffn_w_down"][l], rows=rows)
        if l == n_a - 1:
            k_f32, v_f32, k_bf, vt_bf = _shared_kv(x, p["kv_norm_g"], p["w_k"], p["w_v"], p["k_norm_g"],
                                                   rows=rows, attn_tile=ATTN_TILE if prompt else t)
    dh = p["k_norm_g"].shape[-1]
    return (x, jnp.stack(new_c), jnp.stack(new_n), jnp.stack(new_m),
            k_f32.reshape(b, t, A_HEADS, 2, dh), v_f32.reshape(b, t, A_HEADS, 2 * dh))


def kernel(x_prompt, x_sample, c_prompt, c_sample, state_mlstm_C, state_mlstm_n, state_mlstm_m, cache_k, cache_v, ada_w, ada_b, norm_g, mlstm_w_in, mlstm_b_gates, mlstm_head_g, mlstm_w_out, kv_norm_g, w_k, w_v, k_norm_g, attn_w_q, q_norm_g, attn_lambda, attn_head_g, attn_w_o, rel_bias, ffn_w_gate, ffn_w_up, ffn_w_down):
    bf = lambda w: w.astype(BF16)
    p = dict(norm_g=norm_g, mlstm_w_in=bf(mlstm_w_in), mlstm_b_gates=mlstm_b_gates,
             mlstm_head_g=mlstm_head_g, mlstm_w_out=bf(mlstm_w_out), kv_norm_g=kv_norm_g, w_k=bf(w_k),
             w_v=bf(w_v), k_norm_g=k_norm_g, attn_w_q=bf(attn_w_q), q_norm_g=q_norm_g,
             attn_lambda=attn_lambda, attn_head_g=attn_head_g, attn_w_o=bf(attn_w_o), rel_bias=rel_bias,
             ffn_w_gate=bf(ffn_w_gate), ffn_w_up=bf(ffn_w_up), ffn_w_down=bf(ffn_w_down))
    bp = x_prompt.shape[0]
    n_a, _, heads, dk, dv = state_mlstm_C.shape
    mod = _adaln(jnp.concatenate([c_prompt, c_sample], axis=0), ada_w, ada_b)
    zero_state = (jnp.zeros((n_a, bp, heads, dk, dv), F32), jnp.zeros((n_a, bp, heads, dk), F32),
                  jnp.zeros((n_a, bp, heads), F32))
    y_p, c_p, n_p, m_p, k_p, v_p = _trunk(x_prompt, mod[:, :bp], zero_state, None, MLSTM_PROMPT_CHUNK, 512, p)
    y_s, c_s, n_s, m_s, k_s, v_s = _trunk(x_sample, mod[:, bp:], (state_mlstm_C, state_mlstm_n, state_mlstm_m),
                                          _cache_layouts(cache_k, cache_v, ATTN_TILE), x_sample.shape[1], 512, p)
    return (y_p, y_s, c_p, n_p, m_p, k_p, v_p, c_s, n_s, m_s, k_s, v_s)
```

```python
import functools
import math

import jax
import jax.numpy as jnp
from jax import lax
from jax.experimental import pallas as pl
from jax.experimental.pallas import tpu as pltpu

F32 = jnp.float32
BF16 = jnp.bfloat16

EPS = 1e-6
NEG_BIG = -1e30
CHUNK = 64
N_BUCKETS = 32
MAX_DISTANCE = 128
M_HEADS = 8
A_HEADS = 8
GATE_PAD = 128
LANES = 128
SUBLANES = 8
ONES_ROWS = 16
LOG2E = math.log2(math.e)

V7X_VMEM_BYTES = 64 * 1024 * 1024
VMEM_LIMIT = V7X_VMEM_BYTES - 8 * 1024 * 1024

ATTN_TILE = 256
MLSTM_PROMPT_CHUNK = 256
FFN_CHUNK = 256
ROW_CHUNK = 64


def _params(*sem, flags=None):
    return pltpu.CompilerParams(dimension_semantics=sem, vmem_limit_bytes=VMEM_LIMIT, flags=flags)


def _rms(x, g):
    return x * lax.rsqrt(jnp.mean(x * x, axis=-1, keepdims=True) + EPS) * g


def _log_sigmoid(x):
    return jnp.minimum(x, 0.0) - jnp.log1p(jnp.exp(-jnp.abs(x)))


def _dot(a, b):
    return jnp.dot(a, b, preferred_element_type=F32)


def _dot_nt(a, b):
    return lax.dot_general(a, b, (((1,), (1,)), ((), ())), preferred_element_type=F32)


def _dot_tn(a, b):
    return lax.dot_general(a, b, (((0,), (0,)), ((), ())), preferred_element_type=F32)


def _resident(shape):
    nd = len(shape)
    return pl.BlockSpec(shape, lambda *_: (0,) * nd, pipeline_mode=pl.Buffered(1))


def _adaln_kernel(c_ref, w_ref, b_ref, o_ref):
    c = c_ref[...]
    a = (c * jax.nn.sigmoid(c)).astype(BF16)
    o_ref[0] = _dot(a, w_ref[0].astype(BF16)) + b_ref[0]


def _adaln(c, ada_w, ada_b):
    depth, d, n = ada_w.shape
    rows = c.shape[0]
    tn = 1536
    assert n % tn == 0
    return pl.pallas_call(
        _adaln_kernel,
        grid=(depth, n // tn),
        in_specs=[
            pl.BlockSpec((rows, d), lambda l, j: (0, 0)),
            pl.BlockSpec((1, d, tn), lambda l, j: (l, 0, j)),
            pl.BlockSpec((1, 1, tn), lambda l, j: (l, 0, j)),
        ],
        out_specs=pl.BlockSpec((1, rows, tn), lambda l, j: (l, 0, j)),
        out_shape=jax.ShapeDtypeStruct((depth, rows, n), F32),
        compiler_params=_params("parallel", "parallel"),
        name="adaln",
    )(c, ada_w, ada_b.reshape(depth, 1, n))


def _modulated(x_ref, g_ref, sh_ref, sc_ref):
    x = x_ref[...]
    bb, tt, d = x.shape
    xm = _rms(x, g_ref[...]) * (1.0 + sc_ref[...]) + sh_ref[...]
    return xm.reshape(bb * tt, d).astype(BF16)


def _group_rms_feature_major(yt, gain_col):
    width = gain_col.shape[0]
    groups = []
    for g in range(yt.shape[0] // width):
        seg = yt[g * width:(g + 1) * width]
        groups.append(seg * lax.rsqrt(jnp.mean(seg * seg, axis=0, keepdims=True) + EPS) * gain_col)
    return groups


def _store_values_ext(dst_ref, lead, vt, col, width, heads):
    dv = vt.shape[0] // heads
    ext = dv + ONES_ROWS
    ones = jnp.ones((ONES_ROWS, width), BF16)
    for hh in range(heads):
        dst_ref[lead + (slice(hh * ext, hh * ext + dv), slice(None))] = vt[hh * dv:(hh + 1) * dv, col:col + width]
        dst_ref[lead + (slice(hh * ext + dv, (hh + 1) * ext), slice(None))] = ones


def _mlstm_proj_kernel(x_ref, g_ref, sh_ref, sc_ref, w_ref, wf_ref, wg_ref, wgt_ref,
                       o_ref, of_ref, og_ref, ogt_ref, *, n_chunk):
    bb, tt, n = o_ref.shape
    xm = _modulated(x_ref, g_ref, sh_ref, sc_ref)
    for c in range(n // n_chunk):
        y = _dot(xm, w_ref[:, c * n_chunk:(c + 1) * n_chunk])
        o_ref[:, :, c * n_chunk:(c + 1) * n_chunk] = y.reshape(bb, tt, n_chunk).astype(o_ref.dtype)
    vt = _dot_nt(wf_ref[...], xm).astype(BF16)
    for i in range(bb):
        _store_values_ext(of_ref, (i,), vt, i * tt, tt, M_HEADS)
        ogt_ref[i] = _dot_nt(wgt_ref[...], xm[i * tt:(i + 1) * tt])
    og_ref[...] = _dot(xm, wg_ref[...]).reshape(bb, tt, 2 * GATE_PAD)


def _q_proj_kernel(x_ref, g_ref, sh_ref, sc_ref, wt_ref, qg_ref, o_ref, *, qk_scale):
    bb, n, tt = o_ref.shape
    xm = _modulated(x_ref, g_ref, sh_ref, sc_ref)
    groups = _group_rms_feature_major(_dot_nt(wt_ref[...], xm), qg_ref[...])
    width = qg_ref.shape[0]
    for g, seg in enumerate(groups):
        seg = (seg * qk_scale).astype(o_ref.dtype)
        for i in range(bb):
            o_ref[i, g * width:(g + 1) * width, :] = seg[:, i * tt:(i + 1) * tt]


def _row_blocks(b, t, rows):
    if t >= rows:
        assert t % rows == 0
        return 1, rows
    assert rows % t == 0 and b % (rows // t) == 0
    return rows // t, t


def _token_specs(x, rows):
    b, t, d = x.shape
    bb, tt = _row_blocks(b, t, rows)
    per_b = pl.BlockSpec((bb, 1, d), lambda i, j: (i, 0, 0))
    return bb, tt, (b // bb, t // tt), pl.BlockSpec((bb, tt, d), lambda i, j: (i, j, 0)), per_b


def _mlstm_proj(x, gain, shift, scale, w, w_feat, w_gate, *, rows):
    b, t, d = x.shape
    n = w.shape[1]
    bb, tt, grid, x_spec, per_b = _token_specs(x, rows)
    tok = lambda i, j: (i, j, 0)
    feat = lambda i, j: (i, 0, j)
    ng = w_gate.shape[1]
    half = ng // 2
    wg = (jnp.zeros((d, 2 * GATE_PAD), BF16).at[:, :half].set(w_gate[:, :half])
          .at[:, GATE_PAD:GATE_PAD + half].set(w_gate[:, half:]))
    n_feat = w_feat.shape[1]
    d_ext = n_feat + M_HEADS * ONES_ROWS
    return pl.pallas_call(
        functools.partial(_mlstm_proj_kernel, n_chunk=min(n, 1024)),
        grid=grid,
        in_specs=[x_spec, _resident((1, d)), per_b, per_b, _resident((d, n)),
                  _resident((n_feat, d)), _resident((d, 2 * GATE_PAD)), _resident((ng, d))],
        out_specs=[pl.BlockSpec((bb, tt, n), tok), pl.BlockSpec((bb, d_ext, tt), feat),
                   pl.BlockSpec((bb, tt, 2 * GATE_PAD), tok), pl.BlockSpec((bb, ng, tt), feat)],
        out_shape=[jax.ShapeDtypeStruct((b, t, n), BF16), jax.ShapeDtypeStruct((b, d_ext, t), BF16),
                   jax.ShapeDtypeStruct((b, t, 2 * GATE_PAD), F32), jax.ShapeDtypeStruct((b, ng, t), F32)],
        compiler_params=_params("parallel", "parallel"), name="mlstm_proj",
    )(x, gain.reshape(1, d), shift, scale, w, w_feat.T, wg, w_gate.T)


def _q_proj(x, gain, shift, scale, w, qk_gain, qk_scale, *, rows):
    b, t, d = x.shape
    n = w.shape[1]
    bb, tt, grid, x_spec, per_b = _token_specs(x, rows)
    return pl.pallas_call(
        functools.partial(_q_proj_kernel, qk_scale=qk_scale),
        grid=grid,
        in_specs=[x_spec, _resident((1, d)), per_b, per_b, _resident((n, d)), _resident((qk_gain.shape[-1], 1))],
        out_specs=pl.BlockSpec((bb, n, tt), lambda i, j: (i, 0, j)),
        out_shape=jax.ShapeDtypeStruct((b, n, t), BF16),
        compiler_params=_params("parallel", "parallel"), name="q_proj",
    )(x, gain.reshape(1, d), shift, scale, w.T, qk_gain.reshape(-1, 1))


def _kv_kernel(x_ref, g_ref, wkt_ref, wv_ref, wvt_ref, kg_ref, k_ref, v_ref, kb_ref, vt_ref):
    x = x_ref[...]
    bb, tt, d = x.shape
    h = _rms(x, g_ref[...]).reshape(bb * tt, d).astype(BF16)
    groups = _group_rms_feature_major(_dot_nt(wkt_ref[...], h), kg_ref[...])
    per_tile = LANES // kg_ref.shape[0]
    for c in range(d // LANES):
        k_tok = jnp.concatenate(groups[c * per_tile:(c + 1) * per_tile], axis=0).T.reshape(bb, tt, LANES)
        k_ref[:, :, c * LANES:(c + 1) * LANES] = k_tok
        kb_ref[:, :, c * LANES:(c + 1) * LANES] = k_tok.astype(BF16)
    v_ref[...] = _dot(h, wv_ref[...]).reshape(bb, tt, d)
    vt = _dot_nt(wvt_ref[...], h).astype(BF16)
    per, tile = vt_ref.shape[1], vt_ref.shape[3]
    for i in range(bb):
        for j in range(per):
            _store_values_ext(vt_ref, (i, j), vt, (i * per + j) * tile, tile, A_HEADS)


def _shared_kv(x, kv_g, w_k, w_v, k_g, *, rows, attn_tile):
    b, t, d = x.shape
    bb, tt = _row_blocks(b, t, rows)
    assert tt % attn_tile == 0
    per = tt // attn_tile
    d_ext = d + A_HEADS * ONES_ROWS
    tok = lambda i, j: (i, j, 0)
    blk = pl.BlockSpec((bb, tt, d), tok)
    return pl.pallas_call(
        _kv_kernel,
        grid=(b // bb, t // tt),
        in_specs=[blk, _resident((1, d)), _resident((d, d)), _resident((d, d)), _resident((d, d)),
                  _resident((k_g.shape[-1], 1))],
        out_specs=[blk, blk, blk, pl.BlockSpec((bb, per, d_ext, attn_tile), lambda i, j: (i, j, 0, 0))],
        out_shape=[jax.ShapeDtypeStruct((b, t, d), F32), jax.ShapeDtypeStruct((b, t, d), F32),
                   jax.ShapeDtypeStruct((b, t, d), BF16),
                   jax.ShapeDtypeStruct((b, t // attn_tile, d_ext, attn_tile), BF16)],
        compiler_params=_params("parallel", "parallel"),
        name="shared_kv",
    )(x, kv_g.reshape(1, d), w_k.T, w_v, w_v.T, k_g.reshape(-1, 1))


def _mixer_ffn_kernel(x_ref, a_ref, wp_ref, gate1_ref, g_ref, sh_ref, sc_ref, gate2_ref, wg_ref, wu_ref, wd_ref,
                      o_ref, acc_ref):
    bb, tt, d = x_ref.shape
    f = wg_ref.shape[1]
    a = a_ref[...].reshape(bb * tt, a_ref.shape[-1])
    o_ref[...] = x_ref[...] + gate1_ref[...] * _dot(a, wp_ref[...]).reshape(bb, tt, d)
    xm = _modulated(o_ref, g_ref, sh_ref, sc_ref)
    for c in range(f // FFN_CHUNK):
        sl = slice(c * FFN_CHUNK, (c + 1) * FFN_CHUNK)
        hg = _dot(xm, wg_ref[:, sl])
        hu = _dot(xm, wu_ref[:, sl])
        act = (hg * jax.nn.sigmoid(hg) * hu).astype(BF16)
        part = _dot(act, wd_ref[sl, :])
        if c == 0:
            acc_ref[...] = part
        else:
            acc_ref[...] += part
    o_ref[...] += gate2_ref[...] * acc_ref[...].reshape(bb, tt, d)


def _mixer_ffn(x, a, w_proj, gate1, gain, shift, scale, gate2, w_gate, w_up, w_down, *, rows):
    b, t, d = x.shape
    k = a.shape[-1]
    f = w_gate.shape[1]
    assert f % FFN_CHUNK == 0
    bb, tt = _row_blocks(b, t, rows)
    tok = lambda i, j: (i, j, 0)
    per_b = pl.BlockSpec((bb, 1, d), lambda i, j: (i, 0, 0))
    return pl.pallas_call(
        _mixer_ffn_kernel,
        grid=(b // bb, t // tt),
        in_specs=[pl.BlockSpec((bb, tt, d), tok), pl.BlockSpec((bb, tt, k), tok), _resident((k, d)), per_b,
                  _resident((1, d)), per_b, per_b, per_b,
                  _resident((d, f)), _resident((d, f)), _resident((f, d))],
        out_specs=pl.BlockSpec((bb, tt, d), tok),
        out_shape=jax.ShapeDtypeStruct((b, t, d), F32),
        scratch_shapes=[pltpu.VMEM((bb * tt, d), F32)],
        compiler_params=_params("parallel", "parallel"),
        name="mixer_ffn",
    )(x, a, w_proj, gate1, gain.reshape(1, d), shift, scale, gate2, w_gate, w_up, w_down)


def _mlstm_kernel(q_ref, k_ref, o_ref, vt_ref, g_ref, gt_ref, brow_ref, bcol_ref, hg_ref,
                  c0_ref, n0_ref, m0_ref, h_ref, c_out, n_out, m_out,
                  ct_s, m_s, mnew_s, acol_s, br_s, inter_s, wn_s, wold_s, qk_s, qc_s, sb_s, sv_s, stat_s):
    ci = pl.program_id(1)
    length = q_ref.shape[1]
    ext = ct_s.shape[1]
    dv = ext - ONES_ROWS
    dk = q_ref.shape[2] // M_HEADS
    heads = range(M_HEADS)

    @pl.when(ci == 0)
    def _():
        ct_s[...] = jnp.zeros(ct_s.shape, F32)
        for h in heads:
            c0 = jnp.concatenate([c0_ref[0, h], jnp.zeros((dv - dk, dv), F32)], axis=0)
            ct_s[h, :dv, :] = c0.T
            ct_s[h, dv:, :dk] = jnp.broadcast_to(n0_ref[0, h:h + 1, :], (ONES_ROWS, dk))
        m_s[...] = m0_ref[0]

    row = lax.broadcasted_iota(jnp.int32, (length, length), 0)
    col = lax.broadcasted_iota(jnp.int32, (length, length), 1)
    causal = row <= col
    hi = lax.Precision.HIGHEST
    g = g_ref[0] + brow_ref[...]
    b_cols = jnp.dot((col <= row).astype(F32), _log_sigmoid(g[:, GATE_PAD:]), precision=hi,
                     preferred_element_type=F32)
    acol_s[...] = g[:, :GATE_PAD] - b_cols
    gt = gt_ref[0] + bcol_ref[...]
    ig_rows = gt[:M_HEADS]
    br_s[...] = jnp.dot(_log_sigmoid(gt[M_HEADS:]), causal.astype(F32), precision=hi,
                        preferred_element_type=F32)
    b_rows = br_s[...]
    b_end = br_s[:, length - 1:length]
    m_prev = m_s[...]
    g_rows = b_end - b_rows + ig_rows
    m_new = jnp.maximum(b_end + m_prev, jnp.max(g_rows, axis=-1, keepdims=True))
    mnew_s[...] = m_new
    wold_s[...] = jnp.exp(b_end + m_prev - m_new)
    wn_s[...] = jnp.exp(g_rows - m_new)
    inter_s[...] = b_rows + m_prev

    def q_of(h):
        return q_ref[0, :, h * dk:(h + 1) * dk]

    def k_of(h):
        return k_ref[0, :, h * dk:(h + 1) * dk] * (dk ** -0.5)

    def vt_of(h):
        return vt_ref[0, h * ext:(h + 1) * ext, :]

    for h in heads:
        qk_s[h] = _dot_nt(k_of(h), q_of(h))
        qc_s[h] = _dot_nt(ct_s[h, :, :dk].astype(BF16), q_of(h))
    for h in heads:
        inter = inter_s[h:h + 1, :]
        dmat = jnp.where(causal, acol_s[:, h:h + 1] + br_s[h:h + 1, :], -jnp.inf)
        m_t = jnp.maximum(inter, jnp.max(dmat, axis=0, keepdims=True))
        sb_s[h] = (qk_s[h] * jnp.exp(dmat - m_t)).astype(BF16)
        stat_s[0, h] = jnp.exp(inter - m_t)
        stat_s[1, h] = jnp.exp(-m_t)
    for h in heads:
        sv_s[h] = _dot(vt_of(h), sb_s[h])
    for h in heads:
        w_inter = stat_s[0, h]
        tot = w_inter * qc_s[h] + sv_s[h]
        hh = tot[:dv] / jnp.maximum(jnp.abs(tot[dv:dv + 1]), stat_s[1, h])
        hh = hh * lax.rsqrt(jnp.mean(hh * hh, axis=0, keepdims=True) + EPS)
        og = o_ref[0, :, h * dv:(h + 1) * dv].astype(F32)
        h_ref[0, :, h * dv:(h + 1) * dv] = (_to_token_major(hh) * hg_ref[:, h * dv:(h + 1) * dv]
                                            * jax.nn.sigmoid(og)).astype(h_ref.dtype)
    for h in heads:
        vtw = (vt_of(h).astype(F32) * wn_s[h:h + 1, :]).astype(BF16)
        ct_s[h, :, :dk] = wold_s[h:h + 1, :] * ct_s[h, :, :dk] + _dot(vtw, k_of(h))
    m_s[...] = mnew_s[...]

    @pl.when(ci == pl.num_programs(1) - 1)
    def _():
        for h in heads:
            c_out[0, h] = ct_s[h, :dv, :].T[:dk]
            n_out[0, h:h + 1, :] = ct_s[h, dv:dv + 1, :dk]
        m_out[0] = m_s[...]


def _to_token_major(x):
    dv, length = x.shape
    if length % LANES:
        pad = LANES - length % LANES
        x = jnp.concatenate([x, jnp.zeros((dv, pad), x.dtype)], axis=1)
    return x.T[:length]


def _mlstm(qko, vt, g, gt, b_gates, head_g, c0, n0, m0, *, chunk):
    b, t, _ = qko.shape
    heads, dk, dv = c0.shape[1:]
    d = heads * dv
    qk = heads * dk
    ext = dv + ONES_ROWS
    assert qk * 2 == d and t % chunk == 0 and heads == M_HEADS and vt.shape[1] == heads * ext
    nc = t // chunk
    two_h = 2 * heads
    brow = (jnp.zeros((1, 2 * GATE_PAD), F32).at[0, :heads].set(b_gates[:heads])
            .at[0, GATE_PAD:GATE_PAD + heads].set(b_gates[heads:]))
    st4 = lambda i, c: (i, 0, 0, 0)
    st3 = lambda i, c: (i, 0, 0)
    outs = pl.pallas_call(
        _mlstm_kernel,
        grid=(b, nc),
        in_specs=[
            pl.BlockSpec((1, chunk, qk), lambda i, c: (i, c, 0)),
            pl.BlockSpec((1, chunk, qk), lambda i, c: (i, c, 1)),
            pl.BlockSpec((1, chunk, d), lambda i, c: (i, c, 1)),
            pl.BlockSpec((1, heads * ext, chunk), lambda i, c: (i, 0, c)),
            pl.BlockSpec((1, chunk, 2 * GATE_PAD), lambda i, c: (i, c, 0)),
            pl.BlockSpec((1, two_h, chunk), lambda i, c: (i, 0, c)),
            _resident((1, 2 * GATE_PAD)), _resident((two_h, 1)), _resident((1, d)),
            pl.BlockSpec((1, heads, dk, dv), st4),
            pl.BlockSpec((1, heads, dk), st3),
            pl.BlockSpec((1, heads, 1), st3),
        ],
        out_specs=[
            pl.BlockSpec((1, chunk, d), lambda i, c: (i, c, 0)),
            pl.BlockSpec((1, heads, dk, dv), st4),
            pl.BlockSpec((1, heads, dk), st3),
            pl.BlockSpec((1, heads, 1), st3),
        ],
        out_shape=[
            jax.ShapeDtypeStruct((b, t, d), BF16),
            jax.ShapeDtypeStruct((b, heads, dk, dv), F32),
            jax.ShapeDtypeStruct((b, heads, dk), F32),
            jax.ShapeDtypeStruct((b, heads, 1), F32),
        ],
        scratch_shapes=[pltpu.VMEM((heads, ext, dv), F32),
                        pltpu.VMEM((heads, 1), F32), pltpu.VMEM((heads, 1), F32),
                        pltpu.VMEM((chunk, GATE_PAD), F32),
                        pltpu.VMEM((heads, chunk), F32), pltpu.VMEM((heads, chunk), F32),
                        pltpu.VMEM((heads, chunk), F32), pltpu.VMEM((heads, 1), F32),
                        pltpu.VMEM((heads, chunk, chunk), F32),
                        pltpu.VMEM((heads, ext, chunk), F32),
                        pltpu.VMEM((heads, chunk, chunk), BF16),
                        pltpu.VMEM((heads, ext, chunk), F32),
                        pltpu.VMEM((2, heads, 1, chunk), F32)],
        compiler_params=_params("parallel", "arbitrary"),
        name="mlstm",
    )(qko, qko, qko, vt, g, gt, brow, b_gates.reshape(two_h, 1), head_g.reshape(1, d),
      c0, n0, m0.reshape(b, heads, 1))
    hn, c_new, n_new, m_new = outs
    return hn, c_new, n_new, m_new.reshape(b, heads)


def _rel_bucket(rel):
    nb = N_BUCKETS // 2
    max_exact = nb // 2
    ret = jnp.where(rel > 0, nb, 0)
    n = jnp.abs(rel)
    large = max_exact + (jnp.log(jnp.maximum(n, 1).astype(F32) / max_exact)
                         / math.log(MAX_DISTANCE / max_exact) * (nb - max_exact)).astype(jnp.int32)
    large = jnp.minimum(large, nb - 1)
    return ret + jnp.where(n < max_exact, n, large)


def _bias_tile(rel_bias, q0, k0, tq, tk):
    table = rel_bias.astype(F32)
    span = tq + tk
    r = jnp.arange(span)
    r = jnp.where(r < tk, r, r - span)
    vec = (table[_rel_bucket(r + (k0 - q0))] - table[N_BUCKETS // 2 - 1]).T
    bias = jnp.tile(vec, (1, tq))[:, :tq * (span - 1)].reshape(-1, tq, span - 1)[:, :, :tk]
    q_chunk = (q0 + lax.broadcasted_iota(jnp.int32, (tq, tk), 0)) // CHUNK
    k_chunk = (k0 + lax.broadcasted_iota(jnp.int32, (tq, tk), 1)) // CHUNK
    return jnp.where((k_chunk <= q_chunk)[None], bias * LOG2E, NEG_BIG)


def _attn_prompt_kernel(q_ref, k_ref, vt_ref, bias_ref, lam_ref, hg_ref, o_ref,
                        m_s, alpha_s, acc_s, s_scr, p_scr, *, lam_init):
    qi = pl.program_id(1)
    tile = q_ref.shape[2]
    ext = acc_s.shape[1]
    dv = o_ref.shape[2] // A_HEADS
    dh = dv // 2
    n_chain = 2 * A_HEADS
    m_s[...] = jnp.full(m_s.shape, NEG_BIG, F32)
    acc_s[...] = jnp.zeros(acc_s.shape, F32)

    def kv_tile(j, slot):
        koff = pl.multiple_of(j * tile, tile)
        for hc in range(n_chain):
            lo = hc * dh
            s_scr[hc] = _dot(k_ref[0, pl.ds(koff, tile), lo:lo + dh], q_ref[0, lo:lo + dh, :])
        for hc in range(n_chain):
            m_prev = m_s[hc]
            m_part = None
            for r in range(0, tile, ROW_CHUNK):
                s = s_scr[hc, r:r + ROW_CHUNK]
                if slot is not None:
                    s = s + bias_ref[hc // 2, slot, r:r + ROW_CHUNK]
                    s_scr[hc, r:r + ROW_CHUNK] = s
                part = jnp.max(s.reshape(ROW_CHUNK // SUBLANES, SUBLANES, tile), axis=0)
                m_part = part if m_part is None else jnp.maximum(m_part, part)
            m_new = jnp.maximum(m_prev, jnp.max(m_part, axis=0, keepdims=True))
            alpha_s[hc] = jnp.exp2(m_prev - m_new)
            m_s[hc] = m_new
        for hc in range(n_chain):
            for r in range(0, tile, ROW_CHUNK):
                p_scr[hc, r:r + ROW_CHUNK] = jnp.exp2(s_scr[hc, r:r + ROW_CHUNK] - m_s[hc]).astype(BF16)
        for hc in range(n_chain):
            h = hc // 2
            acc_s[hc] = alpha_s[hc] * acc_s[hc] + _dot(vt_ref[0, j, h * ext:(h + 1) * ext, :], p_scr[hc])

    def far_body(j, carry):
        kv_tile(j, None)
        return carry

    lax.fori_loop(0, jnp.maximum(qi - 1, 0), far_body, 0)

    @pl.when(qi > 0)
    def _():
        kv_tile(qi - 1, 0)

    kv_tile(qi, 1)

    lp = lam_ref[...]
    lam = (jnp.exp(jnp.sum(lp[0:1] * lp[1:2], axis=-1, keepdims=True))
           - jnp.exp(jnp.sum(lp[2:3] * lp[3:4], axis=-1, keepdims=True)) + lam_init)
    for h in range(A_HEADS):
        o = (acc_s[2 * h, :dv] / acc_s[2 * h, dv:dv + 1]
             - lam * (acc_s[2 * h + 1, :dv] / acc_s[2 * h + 1, dv:dv + 1]))
        o = o * lax.rsqrt(jnp.mean(o * o, axis=0, keepdims=True) + EPS) * hg_ref[...] * (1.0 - lam_init)
        o_ref[0, :, h * dv:(h + 1) * dv] = o.T.astype(o_ref.dtype)


def _attn_prompt(q, k, vt, rel_bias, lam_p, head_g, lam_init):
    b, d, t = q.shape
    tile = ATTN_TILE
    assert t % tile == 0 and tile % CHUNK == 0 and tile >= MAX_DISTANCE and vt.shape[-1] == tile
    n = t // tile
    dv = d // A_HEADS
    ext = vt.shape[2] // A_HEADS
    n_chain = 2 * A_HEADS
    bias = jnp.stack([_bias_tile(rel_bias, tile, 0, tile, tile), _bias_tile(rel_bias, 0, 0, tile, tile)],
                     axis=1).swapaxes(-1, -2)
    return pl.pallas_call(
        functools.partial(_attn_prompt_kernel, lam_init=lam_init),
        grid=(b, n),
        in_specs=[pl.BlockSpec((1, d, tile), lambda i, qi: (i, 0, qi)),
                  pl.BlockSpec((1, t, d), lambda i, qi: (i, 0, 0), pipeline_mode=pl.Buffered(1)),
                  pl.BlockSpec((1,) + vt.shape[1:], lambda i, qi: (i, 0, 0, 0), pipeline_mode=pl.Buffered(1)),
                  _resident(bias.shape), _resident(lam_p.shape), _resident((dv, 1))],
        out_specs=pl.BlockSpec((1, tile, d), lambda i, qi: (i, qi, 0)),
        out_shape=jax.ShapeDtypeStruct((b, t, d), BF16),
        scratch_shapes=[pltpu.VMEM((n_chain, 1, tile), F32), pltpu.VMEM((n_chain, 1, tile), F32),
                        pltpu.VMEM((n_chain, ext, tile), F32),
                        pltpu.VMEM((n_chain, tile, tile), F32), pltpu.VMEM((n_chain, tile, tile), BF16)],
        compiler_params=_params("parallel", "arbitrary"),
        name="attn_prompt",
    )(q, k, vt, bias, lam_p, head_g.reshape(dv, 1))


def _attn_sample_kernel(q_ref, kc_ref, vtc_ref, kn_ref, vtn_ref, bias_c_ref, bias_n_ref, lam_ref, hg_ref,
                        o_ref, qbd_s, m_s, alpha_s, acc_s, s_scr, p_scr, *, lam_init):
    tq = q_ref.shape[2]
    tile = vtc_ref.shape[3]
    n_cache = vtc_ref.shape[1]
    ext = acc_s.shape[1]
    dv = o_ref.shape[2] // A_HEADS
    gw = qbd_s.shape[1]
    n_group = qbd_s.shape[0]
    n_sub = gw // tq
    row_blk = lax.broadcasted_iota(jnp.int32, (gw, gw), 0) // (gw // n_sub)
    col_blk = lax.broadcasted_iota(jnp.int32, (gw, gw), 1) // tq
    for g in range(n_group):
        qg = q_ref[0, g * gw:(g + 1) * gw, :]
        qrep = jnp.concatenate([qg] * n_sub, axis=1)
        qbd_s[g] = jnp.where(row_blk == col_blk, qrep, jnp.zeros_like(qrep))
    m_s[...] = jnp.full(m_s.shape, NEG_BIG, F32)
    acc_s[...] = jnp.zeros(acc_s.shape, F32)

    def kv_tile(k_tile, vt_tile, bias_ref, rows):
        for g in range(n_group):
            s_scr[g, :rows] = _dot(k_tile(g), qbd_s[g])
        for g in range(n_group):
            s = s_scr[g, :rows]
            if bias_ref is not None:
                s = s + bias_ref[g]
                s_scr[g, :rows] = s
            m_prev = m_s[g]
            m_new = jnp.maximum(m_prev, jnp.max(s, axis=0, keepdims=True))
            alpha_s[g] = jnp.exp2(m_prev - m_new)
            m_s[g] = m_new
        for g in range(n_group):
            for r in range(0, rows, ROW_CHUNK):
                p_scr[g, r:r + ROW_CHUNK] = jnp.exp2(s_scr[g, r:r + ROW_CHUNK] - m_s[g]).astype(BF16)
        for h in range(A_HEADS):
            g, half = divmod(h, 2)
            lanes = slice(half * 2 * tq, (half + 1) * 2 * tq)
            acc_s[h] = alpha_s[g, :, lanes] * acc_s[h] + _dot(vt_tile(h), p_scr[g, :rows, lanes])

    ones = jnp.ones((ext - dv, tile), BF16)

    def cache_tile(j, bias_ref):
        koff = pl.multiple_of(j * tile, tile)
        kv_tile(lambda g: kc_ref[0, pl.ds(koff, tile), g * gw:(g + 1) * gw],
                lambda h: jnp.concatenate([vtc_ref[0, j, h * dv:(h + 1) * dv, :], ones], axis=0),
                bias_ref, tile)

    def far_body(j, carry):
        cache_tile(j, None)
        return carry

    lax.fori_loop(0, n_cache - 1, far_body, 0)
    cache_tile(n_cache - 1, bias_c_ref)
    kv_tile(lambda g: kn_ref[0, :, g * gw:(g + 1) * gw],
            lambda h: vtn_ref[0, 0, h * ext:(h + 1) * ext, :], bias_n_ref, tq)

    lp = lam_ref[...]
    lam = (jnp.exp(jnp.sum(lp[0:1] * lp[1:2], axis=-1, keepdims=True))
           - jnp.exp(jnp.sum(lp[2:3] * lp[3:4], axis=-1, keepdims=True)) + lam_init)
    for h in range(A_HEADS):
        both = (acc_s[h, :dv] / acc_s[h, dv:dv + 1]).T
        o = both[:tq] - lam * both[tq:]
        o = _rms(o, hg_ref[...]) * (1.0 - lam_init)
        o_ref[0, :, h * dv:(h + 1) * dv] = o.astype(o_ref.dtype)


def _attn_sample(q, cache_k, cache_vt, new_k, new_vt, rel_bias, lam_p, head_g, lam_init):
    b, d, t = q.shape
    past = cache_k.shape[1]
    n_cache, _, tile = cache_vt.shape[1:]
    d_ext = new_vt.shape[2]
    assert past == n_cache * tile and past % CHUNK == 0 and tile >= MAX_DISTANCE
    dv = d // A_HEADS
    ext = d_ext // A_HEADS
    gw = 4 * t
    assert gw == 2 * dv and A_HEADS % 2 == 0
    n_group = A_HEADS // 2

    def packed(bias):
        bt = bias.swapaxes(-1, -2).reshape(n_group, 2, -1, t)
        return jnp.concatenate([bt[:, 0], bt[:, 0], bt[:, 1], bt[:, 1]], axis=-1)

    bias_c = packed(_bias_tile(rel_bias, past, past - tile, t, tile))
    bias_n = packed(_bias_tile(rel_bias, past, past, t, t))
    tok_spec = pl.BlockSpec((1, t, d), lambda i: (i, 0, 0))
    return pl.pallas_call(
        functools.partial(_attn_sample_kernel, lam_init=lam_init),
        grid=(b,),
        in_specs=[pl.BlockSpec((1, d, t), lambda i: (i, 0, 0)),
                  pl.BlockSpec((1, past, d), lambda i: (i, 0, 0)),
                  pl.BlockSpec((1, n_cache, d, tile), lambda i: (i, 0, 0, 0)),
                  tok_spec,
                  pl.BlockSpec((1, 1, d_ext, t), lambda i: (i, 0, 0, 0)),
                  _resident(bias_c.shape), _resident(bias_n.shape), _resident(lam_p.shape),
                  _resident((1, dv))],
        out_specs=tok_spec,
        out_shape=jax.ShapeDtypeStruct((b, t, d), BF16),
        scratch_shapes=[pltpu.VMEM((n_group, gw, gw), BF16),
                        pltpu.VMEM((n_group, 1, gw), F32), pltpu.VMEM((n_group, 1, gw), F32),
                        pltpu.VMEM((A_HEADS, ext, 2 * t), F32),
                        pltpu.VMEM((n_group, tile, gw), F32), pltpu.VMEM((n_group, tile, gw), BF16)],
        compiler_params=_params("parallel"),
        name="attn_sample",
    )(q, cache_k, cache_vt, new_k, new_vt, bias_c, bias_n, lam_p, head_g.reshape(1, dv))


def _cache_layouts(cache_k, cache_v, tile):
    b, past, heads, two, dh = cache_k.shape
    dv = two * dh
    n = past // tile
    k = cache_k.reshape(b, past, heads * dv).astype(BF16)
    per = math.gcd(n, 4)
    vt = pl.pallas_call(
        functools.partial(_values_to_feature_major_kernel, heads=heads),
        grid=(b, n // per),
        in_specs=[pl.BlockSpec((1, per * tile * heads, dv), lambda i, j: (i, j, 0))],
        out_specs=pl.BlockSpec((1, per, heads * dv, tile), lambda i, j: (i, j, 0, 0)),
        out_shape=jax.ShapeDtypeStruct((b, n, heads * dv, tile), BF16),
        compiler_params=_params("parallel", "parallel"),
        name="cache_values_t",
    )(cache_v.reshape(b, past * heads, dv))
    return k, vt


def _values_to_feature_major_kernel(v_ref, o_ref, *, heads):
    per, tile = o_ref.shape[1], o_ref.shape[3]
    dv = v_ref.shape[2]
    for j in range(per):
        for h in range(heads):
            rows = v_ref[0, pl.ds(j * tile * heads + h, tile, stride=heads), :]
            o_ref[0, j, h * dv:(h + 1) * dv, :] = rows.T.astype(BF16)


def _trunk(x, mod, mstate, past, chunk, rows, p):
    b, t, d = x.shape
    depth = p["norm_g"].shape[0]
    n_a = p["mlstm_w_in"].shape[0]
    qk = M_HEADS * (d // M_HEADS // 2)
    new_c, new_n, new_m = [], [], []
    prompt = past is None
    k_f32 = v_f32 = k_bf = vt_bf = None
    for l in range(depth):
        sh1, sc1, g1, sh2, sc2, g2 = [mod[l, :, i * d:(i + 1) * d].reshape(b, 1, d) for i in range(6)]
        if l < n_a:
            w_in = p["mlstm_w_in"][l]
            w_qko = jnp.concatenate([w_in[:, :2 * qk], w_in[:, 2 * qk + d:2 * qk + 2 * d]], axis=1)
            qko, vt, g, gt = _mlstm_proj(x, p["norm_g"][l, 0], sh1, sc1, w_qko, w_in[:, 2 * qk:2 * qk + d],
                                         w_in[:, 2 * qk + 2 * d:], rows=rows)
            hn, c_l, n_l, m_l = _mlstm(qko, vt, g, gt, p["mlstm_b_gates"][l], p["mlstm_head_g"][l],
                                       mstate[0][l], mstate[1][l], mstate[2][l], chunk=chunk)
            new_c.append(c_l)
            new_n.append(n_l)
            new_m.append(m_l)
            mixed, w_proj = hn, p["mlstm_w_out"][l]
        else:
            j = l - n_a
            lam_init = 0.8 - 0.6 * math.exp(-0.3 * l)
            dh = p["q_norm_g"].shape[-1]
            q = _q_proj(x, p["norm_g"][l, 0], sh1, sc1, p["attn_w_q"][j], p["q_norm_g"][j],
                        dh ** -0.5 * LOG2E, rows=rows)
            if prompt:
                o = _attn_prompt(q, k_bf, vt_bf, p["rel_bias"], p["attn_lambda"][j], p["attn_head_g"][j],
                                 lam_init)
            else:
                o = _attn_sample(q, past[0], past[1], k_bf, vt_bf, p["rel_bias"], p["attn_lambda"][j],
                                 p["attn_head_g"][j], lam_init)
            mixed, w_proj = o, p["attn_w_o"][j]
        x = _mixer_ffn(x, mixed, w_proj, g1, p["norm_g"][l, 1], sh2, sc2, g2, p["ffn_w_gate"][l],
                       p["ffn_w_up"][l], p["ffn_w_down"][l], rows=rows)
        if l == n_a - 1:
            k_f32, v_f32, k_bf, vt_bf = _shared_kv(x, p["kv_norm_g"], p["w_k"], p["w_v"], p["k_norm_g"],
                                                   rows=rows, attn_tile=ATTN_TILE if prompt else t)
    dh = p["k_norm_g"].shape[-1]
    return (x, jnp.stack(new_c), jnp.stack(new_n), jnp.stack(new_m),
            k_f32.reshape(b, t, A_HEADS, 2, dh), v_f32.reshape(b, t, A_HEADS, 2 * dh))


def kernel(x_prompt, x_sample, c_prompt, c_sample, state_mlstm_C, state_mlstm_n, state_mlstm_m, cache_k, cache_v, ada_w, ada_b, norm_g, mlstm_w_in, mlstm_b_gates, mlstm_head_g, mlstm_w_out, kv_norm_g, w_k, w_v, k_norm_g, attn_w_q, q_norm_g, attn_lambda, attn_head_g, attn_w_o, rel_bias, ffn_w_gate, ffn_w_up, ffn_w_down):
    bf = lambda w: w.astype(BF16)
    p = dict(norm_g=norm_g, mlstm_w_in=bf(mlstm_w_in), mlstm_b_gates=mlstm_b_gates,
             mlstm_head_g=mlstm_head_g, mlstm_w_out=bf(mlstm_w_out), kv_norm_g=kv_norm_g, w_k=bf(w_k),
             w_v=bf(w_v), k_norm_g=k_norm_g, attn_w_q=bf(attn_w_q), q_norm_g=q_norm_g,
             attn_lambda=attn_lambda, attn_head_g=attn_head_g, attn_w_o=bf(attn_w_o), rel_bias=rel_bias,
             ffn_w_gate=bf(ffn_w_gate), ffn_w_up=bf(ffn_w_up), ffn_w_down=bf(ffn_w_down))
    bp = x_prompt.shape[0]
    n_a, _, heads, dk, dv = state_mlstm_C.shape
    mod = _adaln(jnp.concatenate([c_prompt, c_sample], axis=0), ada_w, ada_b)
    zero_state = (jnp.zeros((n_a, bp, heads, dk, dv), F32), jnp.zeros((n_a, bp, heads, dk), F32),
                  jnp.zeros((n_a, bp, heads), F32))
    y_p, c_p, n_p, m_p, k_p, v_p = _trunk(x_prompt, mod[:, :bp], zero_state, None, MLSTM_PROMPT_CHUNK, 512, p)
    y_s, c_s, n_s, m_s, k_s, v_s = _trunk(x_sample, mod[:, bp:], (state_mlstm_C, state_mlstm_n, state_mlstm_m),
                                          _cache_layouts(cache_k, cache_v, ATTN_TILE), x_sample.shape[1], 512, p)
    return (y_p, y_s, c_p, n_p, m_p, k_p, v_p, c_s, n_s, m_s, k_s, v_s)
```

```python
import functools
import math

import jax
import jax.numpy as jnp
from jax import lax
from jax.experimental import pallas as pl
from jax.experimental.pallas import tpu as pltpu

F32 = jnp.float32
BF16 = jnp.bfloat16

EPS = 1e-6
NEG_BIG = -1e30
CHUNK = 64
N_BUCKETS = 32
MAX_DISTANCE = 128
M_HEADS = 8
A_HEADS = 8
GATE_PAD = 128
LANES = 128
SUBLANES = 8
ONES_ROWS = 16
LOG2E = math.log2(math.e)

V7X_VMEM_BYTES = 64 * 1024 * 1024
VMEM_LIMIT = V7X_VMEM_BYTES - 8 * 1024 * 1024

ATTN_TILE = 256
MLSTM_PROMPT_CHUNK = 256
FFN_CHUNK = 256
ROW_CHUNK = 64


def _params(*sem, flags=None):
    return pltpu.CompilerParams(dimension_semantics=sem, vmem_limit_bytes=VMEM_LIMIT, flags=flags)


def _rms(x, g):
    return x * lax.rsqrt(jnp.mean(x * x, axis=-1, keepdims=True) + EPS) * g


def _log_sigmoid(x):
    return jnp.minimum(x, 0.0) - jnp.log1p(jnp.exp(-jnp.abs(x)))


def _dot(a, b):
    return jnp.dot(a, b, preferred_element_type=F32)


def _dot_nt(a, b):
    return lax.dot_general(a, b, (((1,), (1,)), ((), ())), preferred_element_type=F32)


def _dot_tn(a, b):
    return lax.dot_general(a, b, (((0,), (0,)), ((), ())), preferred_element_type=F32)


def _resident(shape):
    nd = len(shape)
    return pl.BlockSpec(shape, lambda *_: (0,) * nd, pipeline_mode=pl.Buffered(1))


def _adaln_kernel(c_ref, w_ref, b_ref, o_ref):
    c = c_ref[...]
    a = (c * jax.nn.sigmoid(c)).astype(BF16)
    o_ref[0] = _dot(a, w_ref[0].astype(BF16)) + b_ref[0]


def _adaln(c, ada_w, ada_b):
    depth, d, n = ada_w.shape
    rows = c.shape[0]
    tn = 1536
    assert n % tn == 0
    return pl.pallas_call(
        _adaln_kernel,
        grid=(depth, n // tn),
        in_specs=[
            pl.BlockSpec((rows, d), lambda l, j: (0, 0)),
            pl.BlockSpec((1, d, tn), lambda l, j: (l, 0, j)),
            pl.BlockSpec((1, 1, tn), lambda l, j: (l, 0, j)),
        ],
        out_specs=pl.BlockSpec((1, rows, tn), lambda l, j: (l, 0, j)),
        out_shape=jax.ShapeDtypeStruct((depth, rows, n), F32),
        compiler_params=_params("parallel", "parallel"),
        name="adaln",
    )(c, ada_w, ada_b.reshape(depth, 1, n))


def _modulated(x_ref, g_ref, sh_ref, sc_ref):
    x = x_ref[...]
    bb, tt, d = x.shape
    xm = _rms(x, g_ref[...]) * (1.0 + sc_ref[...]) + sh_ref[...]
    return xm.reshape(bb * tt, d).astype(BF16)


def _group_rms_feature_major(yt, gain_col):
    width = gain_col.shape[0]
    groups = []
    for g in range(yt.shape[0] // width):
        seg = yt[g * width:(g + 1) * width]
        groups.append(seg * lax.rsqrt(jnp.mean(seg * seg, axis=0, keepdims=True) + EPS) * gain_col)
    return groups


def _store_values_ext(dst_ref, lead, vt, col, width, heads):
    dv = vt.shape[0] // heads
    ext = dv + ONES_ROWS
    ones = jnp.ones((ONES_ROWS, width), BF16)
    for hh in range(heads):
        dst_ref[lead + (slice(hh * ext, hh * ext + dv), slice(None))] = vt[hh * dv:(hh + 1) * dv, col:col + width]
        dst_ref[lead + (slice(hh * ext + dv, (hh + 1) * ext), slice(None))] = ones


def _mlstm_proj_kernel(x_ref, g_ref, sh_ref, sc_ref, w_ref, wf_ref, wg_ref, wgt_ref,
                       o_ref, of_ref, og_ref, ogt_ref, *, n_chunk):
    bb, tt, n = o_ref.shape
    xm = _modulated(x_ref, g_ref, sh_ref, sc_ref)
    for c in range(n // n_chunk):
        y = _dot(xm, w_ref[:, c * n_chunk:(c + 1) * n_chunk])
        o_ref[:, :, c * n_chunk:(c + 1) * n_chunk] = y.reshape(bb, tt, n_chunk).astype(o_ref.dtype)
    vt = _dot_nt(wf_ref[...], xm).astype(BF16)
    for i in range(bb):
        _store_values_ext(of_ref, (i,), vt, i * tt, tt, M_HEADS)
        ogt_ref[i] = _dot_nt(wgt_ref[...], xm[i * tt:(i + 1) * tt])
    og_ref[...] = _dot(xm, wg_ref[...]).reshape(bb, tt, 2 * GATE_PAD)


def _q_proj_kernel(x_ref, g_ref, sh_ref, sc_ref, wt_ref, qg_ref, o_ref, *, qk_scale):
    bb, n, tt = o_ref.shape
    xm = _modulated(x_ref, g_ref, sh_ref, sc_ref)
    groups = _group_rms_feature_major(_dot_nt(wt_ref[...], xm), qg_ref[...])
    width = qg_ref.shape[0]
    for g, seg in enumerate(groups):
        seg = (seg * qk_scale).astype(o_ref.dtype)
        for i in range(bb):
            o_ref[i, g * width:(g + 1) * width, :] = seg[:, i * tt:(i + 1) * tt]


def _row_blocks(b, t, rows):
    if t >= rows:
        assert t % rows == 0
        return 1, rows
    assert rows % t == 0 and b % (rows // t) == 0
    return rows // t, t


def _token_specs(x, rows):
    b, t, d = x.shape
    bb, tt = _row_blocks(b, t, rows)
    per_b = pl.BlockSpec((bb, 1, d), lambda i, j: (i, 0, 0))
    return bb, tt, (b // bb, t // tt), pl.BlockSpec((bb, tt, d), lambda i, j: (i, j, 0)), per_b


def _mlstm_proj(x, gain, shift, scale, w, w_feat, w_gate, *, rows):
    b, t, d = x.shape
    n = w.shape[1]
    bb, tt, grid, x_spec, per_b = _token_specs(x, rows)
    tok = lambda i, j: (i, j, 0)
    feat = lambda i, j: (i, 0, j)
    ng = w_gate.shape[1]
    half = ng // 2
    wg = (jnp.zeros((d, 2 * GATE_PAD), BF16).at[:, :half].set(w_gate[:, :half])
          .at[:, GATE_PAD:GATE_PAD + half].set(w_gate[:, half:]))
    n_feat = w_feat.shape[1]
    d_ext = n_feat + M_HEADS * ONES_ROWS
    return pl.pallas_call(
        functools.partial(_mlstm_proj_kernel, n_chunk=min(n, 1024)),
        grid=grid,
        in_specs=[x_spec, _resident((1, d)), per_b, per_b, _resident((d, n)),
                  _resident((n_feat, d)), _resident((d, 2 * GATE_PAD)), _resident((ng, d))],
        out_specs=[pl.BlockSpec((bb, tt, n), tok), pl.BlockSpec((bb, d_ext, tt), feat),
                   pl.BlockSpec((bb, tt, 2 * GATE_PAD), tok), pl.BlockSpec((bb, ng, tt), feat)],
        out_shape=[jax.ShapeDtypeStruct((b, t, n), BF16), jax.ShapeDtypeStruct((b, d_ext, t), BF16),
                   jax.ShapeDtypeStruct((b, t, 2 * GATE_PAD), F32), jax.ShapeDtypeStruct((b, ng, t), F32)],
        compiler_params=_params("parallel", "parallel"), name="mlstm_proj",
    )(x, gain.reshape(1, d), shift, scale, w, w_feat.T, wg, w_gate.T)


def _q_proj(x, gain, shift, scale, w, qk_gain, qk_scale, *, rows):
    b, t, d = x.shape
    n = w.shape[1]
    bb, tt, grid, x_spec, per_b = _token_specs(x, rows)
    return pl.pallas_call(
        functools.partial(_q_proj_kernel, qk_scale=qk_scale),
        grid=grid,
        in_specs=[x_spec, _resident((1, d)), per_b, per_b, _resident((n, d)), _resident((qk_gain.shape[-1], 1))],
        out_specs=pl.BlockSpec((bb, n, tt), lambda i, j: (i, 0, j)),
        out_shape=jax.ShapeDtypeStruct((b, n, t), BF16),
        compiler_params=_params("parallel", "parallel"), name="q_proj",
    )(x, gain.reshape(1, d), shift, scale, w.T, qk_gain.reshape(-1, 1))


def _kv_kernel(x_ref, g_ref, wkt_ref, wv_ref, wvt_ref, kg_ref, k_ref, v_ref, kb_ref, vt_ref):
    x = x_ref[...]
    bb, tt, d = x.shape
    h = _rms(x, g_ref[...]).reshape(bb * tt, d).astype(BF16)
    groups = _group_rms_feature_major(_dot_nt(wkt_ref[...], h), kg_ref[...])
    per_tile = LANES // kg_ref.shape[0]
    for c in range(d // LANES):
        k_tok = jnp.concatenate(groups[c * per_tile:(c + 1) * per_tile], axis=0).T.reshape(bb, tt, LANES)
        k_ref[:, :, c * LANES:(c + 1) * LANES] = k_tok
        kb_ref[:, :, c * LANES:(c + 1) * LANES] = k_tok.astype(BF16)
    v_ref[...] = _dot(h, wv_ref[...]).reshape(bb, tt, d)
    vt = _dot_nt(wvt_ref[...], h).astype(BF16)
    per, tile = vt_ref.shape[1], vt_ref.shape[3]
    for i in range(bb):
        for j in range(per):
            _store_values_ext(vt_ref, (i, j), vt, (i * per + j) * tile, tile, A_HEADS)


def _shared_kv(x, kv_g, w_k, w_v, k_g, *, rows, attn_tile):
    b, t, d = x.shape
    bb, tt = _row_blocks(b, t, rows)
    assert tt % attn_tile == 0
    per = tt // attn_tile
    d_ext = d + A_HEADS * ONES_ROWS
    tok = lambda i, j: (i, j, 0)
    blk = pl.BlockSpec((bb, tt, d), tok)
    return pl.pallas_call(
        _kv_kernel,
        grid=(b // bb, t // tt),
        in_specs=[blk, _resident((1, d)), _resident((d, d)), _resident((d, d)), _resident((d, d)),
                  _resident((k_g.shape[-1], 1))],
        out_specs=[blk, blk, blk, pl.BlockSpec((bb, per, d_ext, attn_tile), lambda i, j: (i, j, 0, 0))],
        out_shape=[jax.ShapeDtypeStruct((b, t, d), F32), jax.ShapeDtypeStruct((b, t, d), F32),
                   jax.ShapeDtypeStruct((b, t, d), BF16),
                   jax.ShapeDtypeStruct((b, t // attn_tile, d_ext, attn_tile), BF16)],
        compiler_params=_params("parallel", "parallel"),
        name="shared_kv",
    )(x, kv_g.reshape(1, d), w_k.T, w_v, w_v.T, k_g.reshape(-1, 1))


def _mixer_ffn_kernel(x_ref, a_ref, wp_ref, gate1_ref, g_ref, sh_ref, sc_ref, gate2_ref, wg_ref, wu_ref, wd_ref,
                      o_ref, acc_ref):
    bb, tt, d = x_ref.shape
    f = wg_ref.shape[1]
    a = a_ref[...].reshape(bb * tt, a_ref.shape[-1])
    o_ref[...] = x_ref[...] + gate1_ref[...] * _dot(a, wp_ref[...]).reshape(bb, tt, d)
    xm = _modulated(o_ref, g_ref, sh_ref, sc_ref)
    for c in range(f // FFN_CHUNK):
        sl = slice(c * FFN_CHUNK, (c + 1) * FFN_CHUNK)
        hg = _dot(xm, wg_ref[:, sl])
        hu = _dot(xm, wu_ref[:, sl])
        act = (hg * jax.nn.sigmoid(hg) * hu).astype(BF16)
        part = _dot(act, wd_ref[sl, :])
        if c == 0:
            acc_ref[...] = part
        else:
            acc_ref[...] += part
    o_ref[...] += gate2_ref[...] * acc_ref[...].reshape(bb, tt, d)


def _mixer_ffn(x, a, w_proj, gate1, gain, shift, scale, gate2, w_gate, w_up, w_down, *, rows):
    b, t, d = x.shape
    k = a.shape[-1]
    f = w_gate.shape[1]
    assert f % FFN_CHUNK == 0
    bb, tt = _row_blocks(b, t, rows)
    tok = lambda i, j: (i, j, 0)
    per_b = pl.BlockSpec((bb, 1, d), lambda i, j: (i, 0, 0))
    return pl.pallas_call(
        _mixer_ffn_kernel,
        grid=(b // bb, t // tt),
        in_specs=[pl.BlockSpec((bb, tt, d), tok), pl.BlockSpec((bb, tt, k), tok), _resident((k, d)), per_b,
                  _resident((1, d)), per_b, per_b, per_b,
                  _resident((d, f)), _resident((d, f)), _resident((f, d))],
        out_specs=pl.BlockSpec((bb, tt, d), tok),
        out_shape=jax.ShapeDtypeStruct((b, t, d), F32),
        scratch_shapes=[pltpu.VMEM((bb * tt, d), F32)],
        compiler_params=_params("parallel", "parallel"),
        name="mixer_ffn",
    )(x, a, w_proj, gate1, gain.reshape(1, d), shift, scale, gate2, w_gate, w_up, w_down)


def _mlstm_kernel(q_ref, k_ref, o_ref, vt_ref, g_ref, gt_ref, brow_ref, bcol_ref, hg_ref,
                  c0_ref, n0_ref, m0_ref, h_ref, c_out, n_out, m_out,
                  ct_s, m_s, mnew_s, acol_s, br_s, inter_s, wn_s, wold_s, qk_s, qc_s, sb_s, sv_s, stat_s):
    ci = pl.program_id(1)
    length = q_ref.shape[1]
    ext = ct_s.shape[1]
    dv = ext - ONES_ROWS
    dk = q_ref.shape[2] // M_HEADS
    heads = range(M_HEADS)

    @pl.when(ci == 0)
    def _():
        ct_s[...] = jnp.zeros(ct_s.shape, F32)
        for h in heads:
            c0 = jnp.concatenate([c0_ref[0, h], jnp.zeros((dv - dk, dv), F32)], axis=0)
            ct_s[h, :dv, :] = c0.T
            ct_s[h, dv:, :dk] = jnp.broadcast_to(n0_ref[0, h:h + 1, :], (ONES_ROWS, dk))
        m_s[...] = m0_ref[0]

    row = lax.broadcasted_iota(jnp.int32, (length, length), 0)
    col = lax.broadcasted_iota(jnp.int32, (length, length), 1)
    causal = row <= col
    hi = lax.Precision.HIGHEST
    g = g_ref[0] + brow_ref[...]
    b_cols = jnp.dot((col <= row).astype(F32), _log_sigmoid(g[:, GATE_PAD:]), precision=hi,
                     preferred_element_type=F32)
    acol_s[...] = g[:, :GATE_PAD] - b_cols
    gt = gt_ref[0] + bcol_ref[...]
    ig_rows = gt[:M_HEADS]
    br_s[...] = jnp.dot(_log_sigmoid(gt[M_HEADS:]), causal.astype(F32), precision=hi,
                        preferred_element_type=F32)
    b_rows = br_s[...]
    b_end = br_s[:, length - 1:length]
    m_prev = m_s[...]
    g_rows = b_end - b_rows + ig_rows
    m_new = jnp.maximum(b_end + m_prev, jnp.max(g_rows, axis=-1, keepdims=True))
    mnew_s[...] = m_new
    wold_s[...] = jnp.exp(b_end + m_prev - m_new)
    wn_s[...] = jnp.exp(g_rows - m_new)
    inter_s[...] = b_rows + m_prev

    def q_of(h):
        return q_ref[0, :, h * dk:(h + 1) * dk]

    def k_of(h):
        return k_ref[0, :, h * dk:(h + 1) * dk] * (dk ** -0.5)

    def vt_of(h):
        return vt_ref[0, h * ext:(h + 1) * ext, :]

    for h in heads:
        qk_s[h] = _dot_nt(k_of(h), q_of(h))
        qc_s[h] = _dot_nt(ct_s[h, :, :dk].astype(BF16), q_of(h))
    for h in heads:
        inter = inter_s[h:h + 1, :]
        dmat = jnp.where(causal, acol_s[:, h:h + 1] + br_s[h:h + 1, :], -jnp.inf)
        m_t = jnp.maximum(inter, jnp.max(dmat, axis=0, keepdims=True))
        sb_s[h] = (qk_s[h] * jnp.exp(dmat - m_t)).astype(BF16)
        stat_s[0, h] = jnp.exp(inter - m_t)
        stat_s[1, h] = jnp.exp(-m_t)
    for h in heads:
        sv_s[h] = _dot(vt_of(h), sb_s[h])
    for h in heads:
        w_inter = stat_s[0, h]
        tot = w_inter * qc_s[h] + sv_s[h]
        hh = tot[:dv] / jnp.maximum(jnp.abs(tot[dv:dv + 1]), stat_s[1, h])
        hh = hh * lax.rsqrt(jnp.mean(hh * hh, axis=0, keepdims=True) + EPS)
        og = o_ref[0, :, h * dv:(h + 1) * dv].astype(F32)
        h_ref[0, :, h * dv:(h + 1) * dv] = (_to_token_major(hh) * hg_ref[:, h * dv:(h + 1) * dv]
                                            * jax.nn.sigmoid(og)).astype(h_ref.dtype)
    for h in heads:
        vtw = (vt_of(h).astype(F32) * wn_s[h:h + 1, :]).astype(BF16)
        ct_s[h, :, :dk] = wold_s[h:h + 1, :] * ct_s[h, :, :dk] + _dot(vtw, k_of(h))
    m_s[...] = mnew_s[...]

    @pl.when(ci == pl.num_programs(1) - 1)
    def _():
        for h in heads:
            c_out[0, h] = ct_s[h, :dv, :].T[:dk]
            n_out[0, h:h + 1, :] = ct_s[h, dv:dv + 1, :dk]
        m_out[0] = m_s[...]


def _to_token_major(x):
    dv, length = x.shape
    if length % LANES:
        pad = LANES - length % LANES
        x = jnp.concatenate([x, jnp.zeros((dv, pad), x.dtype)], axis=1)
    return x.T[:length]


def _mlstm(qko, vt, g, gt, b_gates, head_g, c0, n0, m0, *, chunk):
    b, t, _ = qko.shape
    heads, dk, dv = c0.shape[1:]
    d = heads * dv
    qk = heads * dk
    ext = dv + ONES_ROWS
    assert qk * 2 == d and t % chunk == 0 and heads == M_HEADS and vt.shape[1] == heads * ext
    nc = t // chunk
    two_h = 2 * heads
    brow = (jnp.zeros((1, 2 * GATE_PAD), F32).at[0, :heads].set(b_gates[:heads])
            .at[0, GATE_PAD:GATE_PAD + heads].set(b_gates[heads:]))
    st4 = lambda i, c: (i, 0, 0, 0)
    st3 = lambda i, c: (i, 0, 0)
    outs = pl.pallas_call(
        _mlstm_kernel,
        grid=(b, nc),
        in_specs=[
            pl.BlockSpec((1, chunk, qk), lambda i, c: (i, c, 0)),
            pl.BlockSpec((1, chunk, qk), lambda i, c: (i, c, 1)),
            pl.BlockSpec((1, chunk, d), lambda i, c: (i, c, 1)),
            pl.BlockSpec((1, heads * ext, chunk), lambda i, c: (i, 0, c)),
            pl.BlockSpec((1, chunk, 2 * GATE_PAD), lambda i, c: (i, c, 0)),
            pl.BlockSpec((1, two_h, chunk), lambda i, c: (i, 0, c)),
            _resident((1, 2 * GATE_PAD)), _resident((two_h, 1)), _resident((1, d)),
            pl.BlockSpec((1, heads, dk, dv), st4),
            pl.BlockSpec((1, heads, dk), st3),
            pl.BlockSpec((1, heads, 1), st3),
        ],
        out_specs=[
            pl.BlockSpec((1, chunk, d), lambda i, c: (i, c, 0)),
            pl.BlockSpec((1, heads, dk, dv), st4),
            pl.BlockSpec((1, heads, dk), st3),
            pl.BlockSpec((1, heads, 1), st3),
        ],
        out_shape=[
            jax.ShapeDtypeStruct((b, t, d), BF16),
            jax.ShapeDtypeStruct((b, heads, dk, dv), F32),
            jax.ShapeDtypeStruct((b, heads, dk), F32),
            jax.ShapeDtypeStruct((b, heads, 1), F32),
        ],
        scratch_shapes=[pltpu.VMEM((heads, ext, dv), F32),
                        pltpu.VMEM((heads, 1), F32), pltpu.VMEM((heads, 1), F32),
                        pltpu.VMEM((chunk, GATE_PAD), F32),
                        pltpu.VMEM((heads, chunk), F32), pltpu.VMEM((heads, chunk), F32),
                        pltpu.VMEM((heads, chunk), F32), pltpu.VMEM((heads, 1), F32),
                        pltpu.VMEM((heads, chunk, chunk), F32),
                        pltpu.VMEM((heads, ext, chunk), F32),
                        pltpu.VMEM((heads, chunk, chunk), BF16),
                        pltpu.VMEM((heads, ext, chunk), F32),
                        pltpu.VMEM((2, heads, 1, chunk), F32)],
        compiler_params=_params("parallel", "arbitrary"),
        name="mlstm",
    )(qko, qko, qko, vt, g, gt, brow, b_gates.reshape(two_h, 1), head_g.reshape(1, d),
      c0, n0, m0.reshape(b, heads, 1))
    hn, c_new, n_new, m_new = outs
    return hn, c_new, n_new, m_new.reshape(b, heads)


def _rel_bucket(rel):
    nb = N_BUCKETS // 2
    max_exact = nb // 2
    ret = jnp.where(rel > 0, nb, 0)
    n = jnp.abs(rel)
    large = max_exact + (jnp.log(jnp.maximum(n, 1).astype(F32) / max_exact)
                         / math.log(MAX_DISTANCE / max_exact) * (nb - max_exact)).astype(jnp.int32)
    large = jnp.minimum(large, nb - 1)
    return ret + jnp.where(n < max_exact, n, large)


def _bias_tile(rel_bias, q0, k0, tq, tk):
    table = rel_bias.astype(F32)
    span = tq + tk
    r = jnp.arange(span)
    r = jnp.where(r < tk, r, r - span)
    vec = (table[_rel_bucket(r + (k0 - q0))] - table[N_BUCKETS // 2 - 1]).T
    bias = jnp.tile(vec, (1, tq))[:, :tq * (span - 1)].reshape(-1, tq, span - 1)[:, :, :tk]
    q_chunk = (q0 + lax.broadcasted_iota(jnp.int32, (tq, tk), 0)) // CHUNK
    k_chunk = (k0 + lax.broadcasted_iota(jnp.int32, (tq, tk), 1)) // CHUNK
    return jnp.where((k_chunk <= q_chunk)[None], bias * LOG2E, NEG_BIG)


def _attn_prompt_kernel(q_ref, k_ref, vt_ref, bias_ref, lam_ref, hg_ref, o_ref,
                        m_s, alpha_s, acc_s, s_scr, p_scr, *, lam_init):
    qi = pl.program_id(1)
    tile = q_ref.shape[2]
    ext = acc_s.shape[1]
    dv = o_ref.shape[2] // A_HEADS
    dh = dv // 2
    n_chain = 2 * A_HEADS
    m_s[...] = jnp.full(m_s.shape, NEG_BIG, F32)
    acc_s[...] = jnp.zeros(acc_s.shape, F32)

    def kv_tile(j, slot):
        koff = pl.multiple_of(j * tile, tile)
        for hc in range(n_chain):
            lo = hc * dh
            s_scr[hc] = _dot(k_ref[0, pl.ds(koff, tile), lo:lo + dh], q_ref[0, lo:lo + dh, :])
        for hc in range(n_chain):
            m_prev = m_s[hc]
            m_part = None
            for r in range(0, tile, ROW_CHUNK):
                s = s_scr[hc, r:r + ROW_CHUNK]
                if slot is not None:
                    s = s + bias_ref[hc // 2, slot, r:r + ROW_CHUNK]
                    s_scr[hc, r:r + ROW_CHUNK] = s
                part = jnp.max(s.reshape(ROW_CHUNK // SUBLANES, SUBLANES, tile), axis=0)
                m_part = part if m_part is None else jnp.maximum(m_part, part)
            m_new = jnp.maximum(m_prev, jnp.max(m_part, axis=0, keepdims=True))
            alpha_s[hc] = jnp.exp2(m_prev - m_new)
            m_s[hc] = m_new
        for hc in range(n_chain):
            for r in range(0, tile, ROW_CHUNK):
                p_scr[hc, r:r + ROW_CHUNK] = jnp.exp2(s_scr[hc, r:r + ROW_CHUNK] - m_s[hc]).astype(BF16)
        for hc in range(n_chain):
            h = hc // 2
            acc_s[hc] = alpha_s[hc] * acc_s[hc] + _dot(vt_ref[0, j, h * ext:(h + 1) * ext, :], p_scr[hc])

    def far_body(j, carry):
        kv_tile(j, None)
        return carry

    lax.fori_loop(0, jnp.maximum(qi - 1, 0), far_body, 0)

    @pl.when(qi > 0)
    def _():
        kv_tile(qi - 1, 0)

    kv_tile(qi, 1)


    lp = lam_ref[...]
    lam = (jnp.exp(jnp.sum(lp[0:1] * lp[1:2], axis=-1, keepdims=True))
           - jnp.exp(jnp.sum(lp[2:3] * lp[3:4], axis=-1, keepdims=True)) + lam_init)
    for h in range(A_HEADS):
        o = (acc_s[2 * h, :dv] / acc_s[2 * h, dv:dv + 1]
             - lam * (acc_s[2 * h + 1, :dv] / acc_s[2 * h + 1, dv:dv + 1]))
        o = o * lax.rsqrt(jnp.mean(o * o, axis=0, keepdims=True) + EPS) * hg_ref[...] * (1.0 - lam_init)
        o_ref[0, :, h * dv:(h + 1) * dv] = o.T.astype(o_ref.dtype)


def _attn_prompt(q, k, vt, rel_bias, lam_p, head_g, lam_init):
    b, d, t = q.shape
    tile = ATTN_TILE
    assert t % tile == 0 and tile % CHUNK == 0 and tile >= MAX_DISTANCE and vt.shape[-1] == tile
    n = t // tile
    dv = d // A_HEADS
    ext = vt.shape[2] // A_HEADS
    n_chain = 2 * A_HEADS
    bias = jnp.stack([_bias_tile(rel_bias, tile, 0, tile, tile), _bias_tile(rel_bias, 0, 0, tile, tile)],
                     axis=1).swapaxes(-1, -2)
    return pl.pallas_call(
        functools.partial(_attn_prompt_kernel, lam_init=lam_init),
        grid=(b, n),
        in_specs=[pl.BlockSpec((1, d, tile), lambda i, qi: (i, 0, qi)),
                  pl.BlockSpec((1, t, d), lambda i, qi: (i, 0, 0), pipeline_mode=pl.Buffered(1)),
                  pl.BlockSpec((1,) + vt.shape[1:], lambda i, qi: (i, 0, 0, 0), pipeline_mode=pl.Buffered(1)),
                  _resident(bias.shape), _resident(lam_p.shape), _resident((dv, 1))],
        out_specs=pl.BlockSpec((1, tile, d), lambda i, qi: (i, qi, 0)),
        out_shape=jax.ShapeDtypeStruct((b, t, d), BF16),
        scratch_shapes=[pltpu.VMEM((n_chain, 1, tile), F32), pltpu.VMEM((n_chain, 1, tile), F32),
                        pltpu.VMEM((n_chain, ext, tile), F32),
                        pltpu.VMEM((n_chain, tile, tile), F32), pltpu.VMEM((n_chain, tile, tile), BF16)],
        compiler_params=_params("parallel", "arbitrary"),
        name="attn_prompt",
    )(q, k, vt, bias, lam_p, head_g.reshape(dv, 1))


def _attn_sample_kernel(q_ref, kc_ref, vtc_ref, kn_ref, vtn_ref, bias_c_ref, bias_n_ref, lam_ref, hg_ref,
                        o_ref, qbd_s, m_s, alpha_s, acc_s, s_scr, p_scr, *, lam_init):
    tq = q_ref.shape[2]
    tile = vtc_ref.shape[3]
    n_cache = vtc_ref.shape[1]
    ext = acc_s.shape[1]
    dv = o_ref.shape[2] // A_HEADS
    gw = qbd_s.shape[1]
    n_group = qbd_s.shape[0]
    n_sub = gw // tq
    row_blk = lax.broadcasted_iota(jnp.int32, (gw, gw), 0) // (gw // n_sub)
    col_blk = lax.broadcasted_iota(jnp.int32, (gw, gw), 1) // tq
    for g in range(n_group):
        qg = q_ref[0, g * gw:(g + 1) * gw, :]
        qrep = jnp.concatenate([qg] * n_sub, axis=1)
        qbd_s[g] = jnp.where(row_blk == col_blk, qrep, jnp.zeros_like(qrep))
    m_s[...] = jnp.full(m_s.shape, NEG_BIG, F32)
    acc_s[...] = jnp.zeros(acc_s.shape, F32)

    def kv_tile(k_tile, vt_tile, bias_ref, rows):
        for g in range(n_group):
            s_scr[g, :rows] = _dot(k_tile(g), qbd_s[g])
        for g in range(n_group):
            s = s_scr[g, :rows]
            if bias_ref is not None:
                s = s + bias_ref[g]
                s_scr[g, :rows] = s
            m_prev = m_s[g]
            m_new = jnp.maximum(m_prev, jnp.max(s, axis=0, keepdims=True))
            alpha_s[g] = jnp.exp2(m_prev - m_new)
            m_s[g] = m_new
        for g in range(n_group):
            for r in range(0, rows, ROW_CHUNK):
                p_scr[g, r:r + ROW_CHUNK] = jnp.exp2(s_scr[g, r:r + ROW_CHUNK] - m_s[g]).astype(BF16)
        for h in range(A_HEADS):
            g, half = divmod(h, 2)
            lanes = slice(half * 2 * tq, (half + 1) * 2 * tq)
            acc_s[h] = alpha_s[g, :, lanes] * acc_s[h] + _dot(vt_tile(h), p_scr[g, :rows, lanes])

    ones = jnp.ones((ext - dv, tile), BF16)

    def cache_tile(j, bias_ref):
        koff = pl.multiple_of(j * tile, tile)
        kv_tile(lambda g: kc_ref[0, pl.ds(koff, tile), g * gw:(g + 1) * gw],
                lambda h: jnp.concatenate([vtc_ref[0, j, h * dv:(h + 1) * dv, :], ones], axis=0),
                bias_ref, tile)

    def far_body(j, carry):
        cache_tile(j, None)
        return carry

    lax.fori_loop(0, n_cache - 1, far_body, 0)
    cache_tile(n_cache - 1, bias_c_ref)
    kv_tile(lambda g: kn_ref[0, :, g * gw:(g + 1) * gw],
            lambda h: vtn_ref[0, 0, h * ext:(h + 1) * ext, :], bias_n_ref, tq)

    lp = lam_ref[...]
    lam = (jnp.exp(jnp.sum(lp[0:1] * lp[1:2], axis=-1, keepdims=True))
           - jnp.exp(jnp.sum(lp[2:3] * lp[3:4], axis=-1, keepdims=True)) + lam_init)
    for h in range(A_HEADS):
        both = (acc_s[h, :dv] / acc_s[h, dv:dv + 1]).T
        o = both[:tq] - lam * both[tq:]
        o = _rms(o, hg_ref[...]) * (1.0 - lam_init)
        o_ref[0, :, h * dv:(h + 1) * dv] = o.astype(o_ref.dtype)


def _attn_sample(q, cache_k, cache_vt, new_k, new_vt, rel_bias, lam_p, head_g, lam_init):
    b, d, t = q.shape
    past = cache_k.shape[1]
    n_cache, _, tile = cache_vt.shape[1:]
    d_ext = new_vt.shape[2]
    assert past == n_cache * tile and past % CHUNK == 0 and tile >= MAX_DISTANCE
    dv = d // A_HEADS
    ext = d_ext // A_HEADS
    gw = 4 * t
    assert gw == 2 * dv and A_HEADS % 2 == 0
    n_group = A_HEADS // 2

    def packed(bias):
        bt = bias.swapaxes(-1, -2).reshape(n_group, 2, -1, t)
        return jnp.concatenate([bt[:, 0], bt[:, 0], bt[:, 1], bt[:, 1]], axis=-1)

    bias_c = packed(_bias_tile(rel_bias, past, past - tile, t, tile))
    bias_n = packed(_bias_tile(rel_bias, past, past, t, t))
    tok_spec = pl.BlockSpec((1, t, d), lambda i: (i, 0, 0))
    return pl.pallas_call(
        functools.partial(_attn_sample_kernel, lam_init=lam_init),
        grid=(b,),
        in_specs=[pl.BlockSpec((1, d, t), lambda i: (i, 0, 0)),
                  pl.BlockSpec((1, past, d), lambda i: (i, 0, 0)),
                  pl.BlockSpec((1, n_cache, d, tile), lambda i: (i, 0, 0, 0)),
                  tok_spec,
                  pl.BlockSpec((1, 1, d_ext, t), lambda i: (i, 0, 0, 0)),
                  _resident(bias_c.shape), _resident(bias_n.shape), _resident(lam_p.shape),
                  _resident((1, dv))],
        out_specs=tok_spec,
        out_shape=jax.ShapeDtypeStruct((b, t, d), BF16),
        scratch_shapes=[pltpu.VMEM((n_group, gw, gw), BF16),
                        pltpu.VMEM((n_group, 1, gw), F32), pltpu.VMEM((n_group, 1, gw), F32),
                        pltpu.VMEM((A_HEADS, ext, 2 * t), F32),
                        pltpu.VMEM((n_group, tile, gw), F32), pltpu.VMEM((n_group, tile, gw), BF16)],
        compiler_params=_params("parallel"),
        name="attn_sample",
    )(q, cache_k, cache_vt, new_k, new_vt, bias_c, bias_n, lam_p, head_g.reshape(1, dv))


def _cache_layouts(cache_k, cache_v, tile):
    b, past, heads, two, dh = cache_k.shape
    dv = two * dh
    n = past // tile
    k = cache_k.reshape(b, past, heads * dv).astype(BF16)
    per = math.gcd(n, 4)
    vt = pl.pallas_call(
        _values_to_feature_major_kernel,
        grid=(b, n // per),
        in_specs=[pl.BlockSpec((1, per * tile, heads, dv), lambda i, j: (i, j, 0, 0))],
        out_specs=pl.BlockSpec((1, per, heads * dv, tile), lambda i, j: (i, j, 0, 0)),
        out_shape=jax.ShapeDtypeStruct((b, n, heads * dv, tile), BF16),
        compiler_params=_params("parallel", "parallel"),
        name="cache_values_t",
    )(cache_v)
    return k, vt


def _values_to_feature_major_kernel(v_ref, o_ref):
    per, tile = o_ref.shape[1], o_ref.shape[3]
    heads, dv = v_ref.shape[2], v_ref.shape[3]
    for j in range(per):
        for h in range(heads):
            rows = v_ref[0, j * tile:(j + 1) * tile, h, :]
            o_ref[0, j, h * dv:(h + 1) * dv, :] = rows.T.astype(BF16)


def _trunk(x, mod, mstate, past, chunk, rows, p):
    b, t, d = x.shape
    depth = p["norm_g"].shape[0]
    n_a = p["mlstm_w_in"].shape[0]
    qk = M_HEADS * (d // M_HEADS // 2)
    new_c, new_n, new_m = [], [], []
    prompt = past is None
    k_f32 = v_f32 = k_bf = vt_bf = None
    for l in range(depth):
        sh1, sc1, g1, sh2, sc2, g2 = [mod[l, :, i * d:(i + 1) * d].reshape(b, 1, d) for i in range(6)]
        if l < n_a:
            w_in = p["mlstm_w_in"][l]
            w_qko = jnp.concatenate([w_in[:, :2 * qk], w_in[:, 2 * qk + d:2 * qk + 2 * d]], axis=1)
            qko, vt, g, gt = _mlstm_proj(x, p["norm_g"][l, 0], sh1, sc1, w_qko, w_in[:, 2 * qk:2 * qk + d],
                                         w_in[:, 2 * qk + 2 * d:], rows=rows)
            hn, c_l, n_l, m_l = _mlstm(qko, vt, g, gt, p["mlstm_b_gates"][l], p["mlstm_head_g"][l],
                                       mstate[0][l], mstate[1][l], mstate[2][l], chunk=chunk)
            new_c.append(c_l)
            new_n.append(n_l)
            new_m.append(m_l)
            mixed, w_proj = hn, p["mlstm_w_out"][l]
        else:
            j = l - n_a
            lam_init = 0.8 - 0.6 * math.exp(-0.3 * l)
            dh = p["q_norm_g"].shape[-1]
            q = _q_proj(x, p["norm_g"][l, 0], sh1, sc1, p["attn_w_q"][j], p["q_norm_g"][j],
                        dh ** -0.5 * LOG2E, rows=rows)
            if prompt:
                o = _attn_prompt(q, k_bf, vt_bf, p["rel_bias"], p["attn_lambda"][j], p["attn_head_g"][j],
                                 lam_init)
            else:
                o = _attn_sample(q, past[0], past[1], k_bf, vt_bf, p["rel_bias"], p["attn_lambda"][j],
                                 p["attn_head_g"][j], lam_init)
            mixed, w_proj = o, p["attn_w_o"][j]
        x = _mixer_ffn(x, mixed, w_proj, g1, p["norm_g"][l, 1], sh2, sc2, g2, p["ffn_w_gate"][l],
                       p["ffn_w_up"][l], p["ffn_w_down"][l], rows=rows)
        if l == n_a - 1:
            k_f32, v_f32, k_bf, vt_bf = _shared_kv(x, p["kv_norm_g"], p["w_k"], p["w_v"], p["k_norm_g"],
                                                   rows=rows, attn_tile=ATTN_TILE if prompt else t)
    dh = p["k_norm_g"].shape[-1]
    return (x, jnp.stack(new_c), jnp.stack(new_n), jnp.stack(new_m),
            k_f32.reshape(b, t, A_HEADS, 2, dh), v_f32.reshape(b, t, A_HEADS, 2 * dh))


def kernel(x_prompt, x_sample, c_prompt, c_sample, state_mlstm_C, state_mlstm_n, state_mlstm_m, cache_k, cache_v, ada_w, ada_b, norm_g, mlstm_w_in, mlstm_b_gates, mlstm_head_g, mlstm_w_out, kv_norm_g, w_k, w_v, k_norm_g, attn_w_q, q_norm_g, attn_lambda, attn_head_g, attn_w_o, rel_bias, ffn_w_gate, ffn_w_up, ffn_w_down):
    bf = lambda w: w.astype(BF16)
    p = dict(norm_g=norm_g, mlstm_w_in=bf(mlstm_w_in), mlstm_b_gates=mlstm_b_gates,
             mlstm_head_g=mlstm_head_g, mlstm_w_out=bf(mlstm_w_out), kv_norm_g=kv_norm_g, w_k=bf(w_k),
             w_v=bf(w_v), k_norm_g=k_norm_g, attn_w_q=bf(attn_w_q), q_norm_g=q_norm_g,
             attn_lambda=attn_lambda, attn_head_g=attn_head_g, attn_w_o=bf(attn_w_o), rel_bias=rel_bias,
             ffn_w_gate=bf(ffn_w_gate), ffn_w_up=bf(ffn_w_up), ffn_w_down=bf(ffn_w_down))
    bp = x_prompt.shape[0]
    n_a, _, heads, dk, dv = state_mlstm_C.shape
    mod = _adaln(jnp.concatenate([c_prompt, c_sample], axis=0), ada_w, ada_b)
    zero_state = (jnp.zeros((n_a, bp, heads, dk, dv), F32), jnp.zeros((n_a, bp, heads, dk), F32),
                  jnp.zeros((n_a, bp, heads), F32))
    y_p, c_p, n_p, m_p, k_p, v_p = _trunk(x_prompt, mod[:, :bp], zero_state, None, MLSTM_PROMPT_CHUNK, 512, p)
    y_s, c_s, n_s, m_s, k_s, v_s = _trunk(x_sample, mod[:, bp:], (state_mlstm_C, state_mlstm_n, state_mlstm_m),
                                          _cache_layouts(cache_k, cache_v, ATTN_TILE), x_sample.shape[1], 512, p)
    return (y_p, y_s, c_p, n_p, m_p, k_p, v_p, c_s, n_s, m_s, k_s, v_s)
```

```python
import functools
import math

import jax
import jax.numpy as jnp
from jax import lax
from jax.experimental import pallas as pl
from jax.experimental.pallas import tpu as pltpu

F32 = jnp.float32
BF16 = jnp.bfloat16

EPS = 1e-6
NEG_BIG = -1e30
CHUNK = 64
N_BUCKETS = 32
MAX_DISTANCE = 128
M_HEADS = 8
A_HEADS = 8
GATE_PAD = 128
LANES = 128
SUBLANES = 8
ONES_ROWS = 16
LOG2E = math.log2(math.e)

V7X_VMEM_BYTES = 64 * 1024 * 1024
VMEM_LIMIT = V7X_VMEM_BYTES - 8 * 1024 * 1024

ATTN_TILE = 256
MLSTM_PROMPT_CHUNK = 256
FFN_CHUNK = 256
ROW_CHUNK = 64


def _params(*sem, flags=None):
    return pltpu.CompilerParams(dimension_semantics=sem, vmem_limit_bytes=VMEM_LIMIT, flags=flags)


def _rms(x, g):
    return x * lax.rsqrt(jnp.mean(x * x, axis=-1, keepdims=True) + EPS) * g


def _log_sigmoid(x):
    return jnp.minimum(x, 0.0) - jnp.log1p(jnp.exp(-jnp.abs(x)))


def _dot(a, b):
    return jnp.dot(a, b, preferred_element_type=F32)


def _dot_nt(a, b):
    return lax.dot_general(a, b, (((1,), (1,)), ((), ())), preferred_element_type=F32)


def _dot_tn(a, b):
    return lax.dot_general(a, b, (((0,), (0,)), ((), ())), preferred_element_type=F32)


def _resident(shape):
    nd = len(shape)
    return pl.BlockSpec(shape, lambda *_: (0,) * nd, pipeline_mode=pl.Buffered(1))


def _adaln_kernel(c_ref, w_ref, b_ref, o_ref):
    c = c_ref[...]
    a = (c * jax.nn.sigmoid(c)).astype(BF16)
    o_ref[0] = _dot(a, w_ref[0].astype(BF16)) + b_ref[0]


def _adaln(c, ada_w, ada_b):
    depth, d, n = ada_w.shape
    rows = c.shape[0]
    tn = 1536
    assert n % tn == 0
    return pl.pallas_call(
        _adaln_kernel,
        grid=(depth, n // tn),
        in_specs=[
            pl.BlockSpec((rows, d), lambda l, j: (0, 0)),
            pl.BlockSpec((1, d, tn), lambda l, j: (l, 0, j)),
            pl.BlockSpec((1, 1, tn), lambda l, j: (l, 0, j)),
        ],
        out_specs=pl.BlockSpec((1, rows, tn), lambda l, j: (l, 0, j)),
        out_shape=jax.ShapeDtypeStruct((depth, rows, n), F32),
        compiler_params=_params("parallel", "parallel"),
        name="adaln",
    )(c, ada_w, ada_b.reshape(depth, 1, n))


def _modulated(x_ref, g_ref, sh_ref, sc_ref):
    x = x_ref[...]
    bb, tt, d = x.shape
    xm = _rms(x, g_ref[...]) * (1.0 + sc_ref[...]) + sh_ref[...]
    return xm.reshape(bb * tt, d).astype(BF16)


def _group_rms_feature_major(yt, gain_col):
    width = gain_col.shape[0]
    groups = []
    for g in range(yt.shape[0] // width):
        seg = yt[g * width:(g + 1) * width]
        groups.append(seg * lax.rsqrt(jnp.mean(seg * seg, axis=0, keepdims=True) + EPS) * gain_col)
    return groups


def _store_values_ext(dst_ref, lead, vt, col, width, heads):
    dv = vt.shape[0] // heads
    ext = dv + ONES_ROWS
    ones = jnp.ones((ONES_ROWS, width), BF16)
    for hh in range(heads):
        dst_ref[lead + (slice(hh * ext, hh * ext + dv), slice(None))] = vt[hh * dv:(hh + 1) * dv, col:col + width]
        dst_ref[lead + (slice(hh * ext + dv, (hh + 1) * ext), slice(None))] = ones


def _mlstm_proj_kernel(x_ref, g_ref, sh_ref, sc_ref, w_ref, wf_ref, wg_ref, wgt_ref,
                       o_ref, of_ref, og_ref, ogt_ref, *, n_chunk):
    bb, tt, n = o_ref.shape
    xm = _modulated(x_ref, g_ref, sh_ref, sc_ref)
    for c in range(n // n_chunk):
        y = _dot(xm, w_ref[:, c * n_chunk:(c + 1) * n_chunk])
        o_ref[:, :, c * n_chunk:(c + 1) * n_chunk] = y.reshape(bb, tt, n_chunk).astype(o_ref.dtype)
    vt = _dot_nt(wf_ref[...], xm).astype(BF16)
    for i in range(bb):
        _store_values_ext(of_ref, (i,), vt, i * tt, tt, M_HEADS)
        ogt_ref[i] = _dot_nt(wgt_ref[...], xm[i * tt:(i + 1) * tt])
    og_ref[...] = _dot(xm, wg_ref[...]).reshape(bb, tt, 2 * GATE_PAD)


def _q_proj_kernel(x_ref, g_ref, sh_ref, sc_ref, wt_ref, qg_ref, o_ref, *, qk_scale):
    bb, n, tt = o_ref.shape
    xm = _modulated(x_ref, g_ref, sh_ref, sc_ref)
    groups = _group_rms_feature_major(_dot_nt(wt_ref[...], xm), qg_ref[...])
    width = qg_ref.shape[0]
    for g, seg in enumerate(groups):
        seg = (seg * qk_scale).astype(o_ref.dtype)
        for i in range(bb):
            o_ref[i, g * width:(g + 1) * width, :] = seg[:, i * tt:(i + 1) * tt]


def _row_blocks(b, t, rows):
    if t >= rows:
        assert t % rows == 0
        return 1, rows
    assert rows % t == 0 and b % (rows // t) == 0
    return rows // t, t


def _token_specs(x, rows):
    b, t, d = x.shape
    bb, tt = _row_blocks(b, t, rows)
    per_b = pl.BlockSpec((bb, 1, d), lambda i, j: (i, 0, 0))
    return bb, tt, (b // bb, t // tt), pl.BlockSpec((bb, tt, d), lambda i, j: (i, j, 0)), per_b


def _mlstm_proj(x, gain, shift, scale, w, w_feat, w_gate, *, rows):
    b, t, d = x.shape
    n = w.shape[1]
    bb, tt, grid, x_spec, per_b = _token_specs(x, rows)
    tok = lambda i, j: (i, j, 0)
    feat = lambda i, j: (i, 0, j)
    ng = w_gate.shape[1]
    half = ng // 2
    wg = (jnp.zeros((d, 2 * GATE_PAD), BF16).at[:, :half].set(w_gate[:, :half])
          .at[:, GATE_PAD:GATE_PAD + half].set(w_gate[:, half:]))
    n_feat = w_feat.shape[1]
    d_ext = n_feat + M_HEADS * ONES_ROWS
    return pl.pallas_call(
        functools.partial(_mlstm_proj_kernel, n_chunk=min(n, 1024)),
        grid=grid,
        in_specs=[x_spec, _resident((1, d)), per_b, per_b, _resident((d, n)),
                  _resident((n_feat, d)), _resident((d, 2 * GATE_PAD)), _resident((ng, d))],
        out_specs=[pl.BlockSpec((bb, tt, n), tok), pl.BlockSpec((bb, d_ext, tt), feat),
                   pl.BlockSpec((bb, tt, 2 * GATE_PAD), tok), pl.BlockSpec((bb, ng, tt), feat)],
        out_shape=[jax.ShapeDtypeStruct((b, t, n), BF16), jax.ShapeDtypeStruct((b, d_ext, t), BF16),
                   jax.ShapeDtypeStruct((b, t, 2 * GATE_PAD), F32), jax.ShapeDtypeStruct((b, ng, t), F32)],
        compiler_params=_params("parallel", "parallel"), name="mlstm_proj",
    )(x, gain.reshape(1, d), shift, scale, w, w_feat.T, wg, w_gate.T)


def _q_proj(x, gain, shift, scale, w, qk_gain, qk_scale, *, rows):
    b, t, d = x.shape
    n = w.shape[1]
    bb, tt, grid, x_spec, per_b = _token_specs(x, rows)
    return pl.pallas_call(
        functools.partial(_q_proj_kernel, qk_scale=qk_scale),
        grid=grid,
        in_specs=[x_spec, _resident((1, d)), per_b, per_b, _resident((n, d)), _resident((qk_gain.shape[-1], 1))],
        out_specs=pl.BlockSpec((bb, n, tt), lambda i, j: (i, 0, j)),
        out_shape=jax.ShapeDtypeStruct((b, n, t), BF16),
        compiler_params=_params("parallel", "parallel"), name="q_proj",
    )(x, gain.reshape(1, d), shift, scale, w.T, qk_gain.reshape(-1, 1))


def _kv_kernel(x_ref, g_ref, wkt_ref, wv_ref, wvt_ref, kg_ref, k_ref, v_ref, kb_ref, vt_ref):
    x = x_ref[...]
    bb, tt, d = x.shape
    h = _rms(x, g_ref[...]).reshape(bb * tt, d).astype(BF16)
    groups = _group_rms_feature_major(_dot_nt(wkt_ref[...], h), kg_ref[...])
    per_tile = LANES // kg_ref.shape[0]
    for c in range(d // LANES):
        k_tok = jnp.concatenate(groups[c * per_tile:(c + 1) * per_tile], axis=0).T.reshape(bb, tt, LANES)
        k_ref[:, :, c * LANES:(c + 1) * LANES] = k_tok
        kb_ref[:, :, c * LANES:(c + 1) * LANES] = k_tok.astype(BF16)
    v_ref[...] = _dot(h, wv_ref[...]).reshape(bb, tt, d)
    vt = _dot_nt(wvt_ref[...], h).astype(BF16)
    per, tile = vt_ref.shape[1], vt_ref.shape[3]
    for i in range(bb):
        for j in range(per):
            _store_values_ext(vt_ref, (i, j), vt, (i * per + j) * tile, tile, A_HEADS)


def _shared_kv(x, kv_g, w_k, w_v, k_g, *, rows, attn_tile):
    b, t, d = x.shape
    bb, tt = _row_blocks(b, t, rows)
    assert tt % attn_tile == 0
    per = tt // attn_tile
    d_ext = d + A_HEADS * ONES_ROWS
    tok = lambda i, j: (i, j, 0)
    blk = pl.BlockSpec((bb, tt, d), tok)
    return pl.pallas_call(
        _kv_kernel,
        grid=(b // bb, t // tt),
        in_specs=[blk, _resident((1, d)), _resident((d, d)), _resident((d, d)), _resident((d, d)),
                  _resident((k_g.shape[-1], 1))],
        out_specs=[blk, blk, blk, pl.BlockSpec((bb, per, d_ext, attn_tile), lambda i, j: (i, j, 0, 0))],
        out_shape=[jax.ShapeDtypeStruct((b, t, d), F32), jax.ShapeDtypeStruct((b, t, d), F32),
                   jax.ShapeDtypeStruct((b, t, d), BF16),
                   jax.ShapeDtypeStruct((b, t // attn_tile, d_ext, attn_tile), BF16)],
        compiler_params=_params("parallel", "parallel"),
        name="shared_kv",
    )(x, kv_g.reshape(1, d), w_k.T, w_v, w_v.T, k_g.reshape(-1, 1))


def _mixer_ffn_kernel(x_ref, a_ref, wp_ref, gate1_ref, g_ref, sh_ref, sc_ref, gate2_ref, wg_ref, wu_ref, wd_ref,
                      o_ref, acc_ref):
    bb, tt, d = x_ref.shape
    f = wg_ref.shape[1]
    a = a_ref[...].reshape(bb * tt, a_ref.shape[-1])
    o_ref[...] = x_ref[...] + gate1_ref[...] * _dot(a, wp_ref[...]).reshape(bb, tt, d)
    xm = _modulated(o_ref, g_ref, sh_ref, sc_ref)
    for c in range(f // FFN_CHUNK):
        sl = slice(c * FFN_CHUNK, (c + 1) * FFN_CHUNK)
        hg = _dot(xm, wg_ref[:, sl])
        hu = _dot(xm, wu_ref[:, sl])
        act = (hg * jax.nn.sigmoid(hg) * hu).astype(BF16)
        part = _dot(act, wd_ref[sl, :])
        if c == 0:
            acc_ref[...] = part
        else:
            acc_ref[...] += part
    o_ref[...] += gate2_ref[...] * acc_ref[...].reshape(bb, tt, d)


def _mixer_ffn(x, a, w_proj, gate1, gain, shift, scale, gate2, w_gate, w_up, w_down, *, rows):
    b, t, d = x.shape
    k = a.shape[-1]
    f = w_gate.shape[1]
    assert f % FFN_CHUNK == 0
    bb, tt = _row_blocks(b, t, rows)
    tok = lambda i, j: (i, j, 0)
    per_b = pl.BlockSpec((bb, 1, d), lambda i, j: (i, 0, 0))
    return pl.pallas_call(
        _mixer_ffn_kernel,
        grid=(b // bb, t // tt),
        in_specs=[pl.BlockSpec((bb, tt, d), tok), pl.BlockSpec((bb, tt, k), tok), _resident((k, d)), per_b,
                  _resident((1, d)), per_b, per_b, per_b,
                  _resident((d, f)), _resident((d, f)), _resident((f, d))],
        out_specs=pl.BlockSpec((bb, tt, d), tok),
        out_shape=jax.ShapeDtypeStruct((b, t, d), F32),
        scratch_shapes=[pltpu.VMEM((bb * tt, d), F32)],
        compiler_params=_params("parallel", "parallel"),
        name="mixer_ffn",
    )(x, a, w_proj, gate1, gain.reshape(1, d), shift, scale, gate2, w_gate, w_up, w_down)


def _mlstm_kernel(q_ref, k_ref, o_ref, vt_ref, g_ref, gt_ref, brow_ref, bcol_ref, hg_ref,
                  c0_ref, n0_ref, m0_ref, h_ref, c_out, n_out, m_out,
                  ct_s, m_s, mnew_s, acol_s, br_s, inter_s, wn_s, wold_s, qk_s, qc_s, sb_s, sv_s, stat_s):
    ci = pl.program_id(1)
    length = q_ref.shape[1]
    ext = ct_s.shape[1]
    dv = ext - ONES_ROWS
    dk = q_ref.shape[2] // M_HEADS
    heads = range(M_HEADS)

    @pl.when(ci == 0)
    def _():
        ct_s[...] = jnp.zeros(ct_s.shape, F32)
        for h in heads:
            c0 = jnp.concatenate([c0_ref[0, h], jnp.zeros((dv - dk, dv), F32)], axis=0)
            ct_s[h, :dv, :] = c0.T
            ct_s[h, dv:, :dk] = jnp.broadcast_to(n0_ref[0, h:h + 1, :], (ONES_ROWS, dk))
        m_s[...] = m0_ref[0]

    row = lax.broadcasted_iota(jnp.int32, (length, length), 0)
    col = lax.broadcasted_iota(jnp.int32, (length, length), 1)
    causal = row <= col
    hi = lax.Precision.HIGHEST
    g = g_ref[0] + brow_ref[...]
    b_cols = jnp.dot((col <= row).astype(F32), _log_sigmoid(g[:, GATE_PAD:]), precision=hi,
                     preferred_element_type=F32)
    acol_s[...] = g[:, :GATE_PAD] - b_cols
    gt = gt_ref[0] + bcol_ref[...]
    ig_rows = gt[:M_HEADS]
    br_s[...] = jnp.dot(_log_sigmoid(gt[M_HEADS:]), causal.astype(F32), precision=hi,
                        preferred_element_type=F32)
    b_rows = br_s[...]
    b_end = br_s[:, length - 1:length]
    m_prev = m_s[...]
    g_rows = b_end - b_rows + ig_rows
    m_new = jnp.maximum(b_end + m_prev, jnp.max(g_rows, axis=-1, keepdims=True))
    mnew_s[...] = m_new
    wold_s[...] = jnp.exp(b_end + m_prev - m_new)
    wn_s[...] = jnp.exp(g_rows - m_new)
    inter_s[...] = b_rows + m_prev

    def q_of(h):
        return q_ref[0, :, h * dk:(h + 1) * dk]

    def k_of(h):
        return k_ref[0, :, h * dk:(h + 1) * dk] * (dk ** -0.5)

    def vt_of(h):
        return vt_ref[0, h * ext:(h + 1) * ext, :]

    for h in heads:
        qk_s[h] = _dot_nt(k_of(h), q_of(h))
        qc_s[h] = _dot_nt(ct_s[h, :, :dk].astype(BF16), q_of(h))
    for h in heads:
        inter = inter_s[h:h + 1, :]
        dmat = jnp.where(causal, acol_s[:, h:h + 1] + br_s[h:h + 1, :], -jnp.inf)
        m_t = jnp.maximum(inter, jnp.max(dmat, axis=0, keepdims=True))
        sb_s[h] = (qk_s[h] * jnp.exp(dmat - m_t)).astype(BF16)
        stat_s[0, h] = jnp.exp(inter - m_t)
        stat_s[1, h] = jnp.exp(-m_t)
    for h in heads:
        sv_s[h] = _dot(vt_of(h), sb_s[h])
    for h in heads:
        w_inter = stat_s[0, h]
        tot = w_inter * qc_s[h] + sv_s[h]
        hh = tot[:dv] / jnp.maximum(jnp.abs(tot[dv:dv + 1]), stat_s[1, h])
        hh = hh * lax.rsqrt(jnp.mean(hh * hh, axis=0, keepdims=True) + EPS)
        og = o_ref[0, :, h * dv:(h + 1) * dv].astype(F32)
        h_ref[0, :, h * dv:(h + 1) * dv] = (_to_token_major(hh) * hg_ref[:, h * dv:(h + 1) * dv]
                                            * jax.nn.sigmoid(og)).astype(h_ref.dtype)
    for h in heads:
        vtw = (vt_of(h).astype(F32) * wn_s[h:h + 1, :]).astype(BF16)
        ct_s[h, :, :dk] = wold_s[h:h + 1, :] * ct_s[h, :, :dk] + _dot(vtw, k_of(h))
    m_s[...] = mnew_s[...]

    @pl.when(ci == pl.num_programs(1) - 1)
    def _():
        for h in heads:
            c_out[0, h] = ct_s[h, :dv, :].T[:dk]
            n_out[0, h:h + 1, :] = ct_s[h, dv:dv + 1, :dk]
        m_out[0] = m_s[...]


def _to_token_major(x):
    dv, length = x.shape
    if length % LANES:
        pad = LANES - length % LANES
        x = jnp.concatenate([x, jnp.zeros((dv, pad), x.dtype)], axis=1)
    return x.T[:length]


def _mlstm(qko, vt, g, gt, b_gates, head_g, c0, n0, m0, *, chunk):
    b, t, _ = qko.shape
    heads, dk, dv = c0.shape[1:]
    d = heads * dv
    qk = heads * dk
    ext = dv + ONES_ROWS
    assert qk * 2 == d and t % chunk == 0 and heads == M_HEADS and vt.shape[1] == heads * ext
    nc = t // chunk
    two_h = 2 * heads
    brow = (jnp.zeros((1, 2 * GATE_PAD), F32).at[0, :heads].set(b_gates[:heads])
            .at[0, GATE_PAD:GATE_PAD + heads].set(b_gates[heads:]))
    st4 = lambda i, c: (i, 0, 0, 0)
    st3 = lambda i, c: (i, 0, 0)
    outs = pl.pallas_call(
        _mlstm_kernel,
        grid=(b, nc),
        in_specs=[
            pl.BlockSpec((1, chunk, qk), lambda i, c: (i, c, 0)),
            pl.BlockSpec((1, chunk, qk), lambda i, c: (i, c, 1)),
            pl.BlockSpec((1, chunk, d), lambda i, c: (i, c, 1)),
            pl.BlockSpec((1, heads * ext, chunk), lambda i, c: (i, 0, c)),
            pl.BlockSpec((1, chunk, 2 * GATE_PAD), lambda i, c: (i, c, 0)),
            pl.BlockSpec((1, two_h, chunk), lambda i, c: (i, 0, c)),
            _resident((1, 2 * GATE_PAD)), _resident((two_h, 1)), _resident((1, d)),
            pl.BlockSpec((1, heads, dk, dv), st4),
            pl.BlockSpec((1, heads, dk), st3),
            pl.BlockSpec((1, heads, 1), st3),
        ],
        out_specs=[
            pl.BlockSpec((1, chunk, d), lambda i, c: (i, c, 0)),
            pl.BlockSpec((1, heads, dk, dv), st4),
            pl.BlockSpec((1, heads, dk), st3),
            pl.BlockSpec((1, heads, 1), st3),
        ],
        out_shape=[
            jax.ShapeDtypeStruct((b, t, d), BF16),
            jax.ShapeDtypeStruct((b, heads, dk, dv), F32),
            jax.ShapeDtypeStruct((b, heads, dk), F32),
            jax.ShapeDtypeStruct((b, heads, 1), F32),
        ],
        scratch_shapes=[pltpu.VMEM((heads, ext, dv), F32),
                        pltpu.VMEM((heads, 1), F32), pltpu.VMEM((heads, 1), F32),
                        pltpu.VMEM((chunk, GATE_PAD), F32),
                        pltpu.VMEM((heads, chunk), F32), pltpu.VMEM((heads, chunk), F32),
                        pltpu.VMEM((heads, chunk), F32), pltpu.VMEM((heads, 1), F32),
                        pltpu.VMEM((heads, chunk, chunk), F32),
                        pltpu.VMEM((heads, ext, chunk), F32),
                        pltpu.VMEM((heads, chunk, chunk), BF16),
                        pltpu.VMEM((heads, ext, chunk), F32),
                        pltpu.VMEM((2, heads, 1, chunk), F32)],
        compiler_params=_params("parallel", "arbitrary"),
        name="mlstm",
    )(qko, qko, qko, vt, g, gt, brow, b_gates.reshape(two_h, 1), head_g.reshape(1, d),
      c0, n0, m0.reshape(b, heads, 1))
    hn, c_new, n_new, m_new = outs
    return hn, c_new, n_new, m_new.reshape(b, heads)


def _rel_bucket(rel):
    nb = N_BUCKETS // 2
    max_exact = nb // 2
    ret = jnp.where(rel > 0, nb, 0)
    n = jnp.abs(rel)
    large = max_exact + (jnp.log(jnp.maximum(n, 1).astype(F32) / max_exact)
                         / math.log(MAX_DISTANCE / max_exact) * (nb - max_exact)).astype(jnp.int32)
    large = jnp.minimum(large, nb - 1)
    return ret + jnp.where(n < max_exact, n, large)


def _bias_tile(rel_bias, q0, k0, tq, tk):
    table = rel_bias.astype(F32)
    span = tq + tk
    r = jnp.arange(span)
    r = jnp.where(r < tk, r, r - span)
    vec = (table[_rel_bucket(r + (k0 - q0))] - table[N_BUCKETS // 2 - 1]).T
    bias = jnp.tile(vec, (1, tq))[:, :tq * (span - 1)].reshape(-1, tq, span - 1)[:, :, :tk]
    q_chunk = (q0 + lax.broadcasted_iota(jnp.int32, (tq, tk), 0)) // CHUNK
    k_chunk = (k0 + lax.broadcasted_iota(jnp.int32, (tq, tk), 1)) // CHUNK
    return jnp.where((k_chunk <= q_chunk)[None], bias * LOG2E, NEG_BIG)


def _attn_prompt_kernel(q_ref, k_ref, vt_ref, bias_ref, lam_ref, hg_ref, o_ref,
                        m_s, alpha_s, acc_s, s_scr, p_scr, *, lam_init):
    qi = pl.program_id(1)
    tile = q_ref.shape[2]
    ext = acc_s.shape[1]
    dv = o_ref.shape[2] // A_HEADS
    dh = dv // 2
    n_chain = 2 * A_HEADS
    m_s[...] = jnp.full(m_s.shape, NEG_BIG, F32)
    acc_s[...] = jnp.zeros(acc_s.shape, F32)

    def kv_tile(j, slot):
        koff = pl.multiple_of(j * tile, tile)
        for hc in range(n_chain):
            lo = hc * dh
            s_scr[hc] = _dot(k_ref[0, pl.ds(koff, tile), lo:lo + dh], q_ref[0, lo:lo + dh, :])
        for hc in range(n_chain):
            m_prev = m_s[hc]
            m_part = None
            for r in range(0, tile, ROW_CHUNK):
                s = s_scr[hc, r:r + ROW_CHUNK]
                if slot is not None:
                    s = s + bias_ref[hc // 2, slot, r:r + ROW_CHUNK]
                    s_scr[hc, r:r + ROW_CHUNK] = s
                part = jnp.max(s.reshape(ROW_CHUNK // SUBLANES, SUBLANES, tile), axis=0)
                m_part = part if m_part is None else jnp.maximum(m_part, part)
            m_new = jnp.maximum(m_prev, jnp.max(m_part, axis=0, keepdims=True))
            alpha_s[hc] = jnp.exp2(m_prev - m_new)
            m_s[hc] = m_new
        for hc in range(n_chain):
            for r in range(0, tile, ROW_CHUNK):
                p_scr[hc, r:r + ROW_CHUNK] = jnp.exp2(s_scr[hc, r:r + ROW_CHUNK] - m_s[hc]).astype(BF16)
        for hc in range(n_chain):
            h = hc // 2
            acc_s[hc] = alpha_s[hc] * acc_s[hc] + _dot(vt_ref[0, j, h * ext:(h + 1) * ext, :], p_scr[hc])

    def far_body(j, carry):
        kv_tile(j, None)
        return carry

    lax.fori_loop(0, jnp.maximum(qi - 1, 0), far_body, 0)

    @pl.when(qi > 0)
    def _():
        kv_tile(qi - 1, 0)

    kv_tile(qi, 1)


    lp = lam_ref[...]
    lam = (jnp.exp(jnp.sum(lp[0:1] * lp[1:2], axis=-1, keepdims=True))
           - jnp.exp(jnp.sum(lp[2:3] * lp[3:4], axis=-1, keepdims=True)) + lam_init)
    for h in range(A_HEADS):
        o = (acc_s[2 * h, :dv] / acc_s[2 * h, dv:dv + 1]
             - lam * (acc_s[2 * h + 1, :dv] / acc_s[2 * h + 1, dv:dv + 1]))
        o = o * lax.rsqrt(jnp.mean(o * o, axis=0, keepdims=True) + EPS) * hg_ref[...] * (1.0 - lam_init)
        o_ref[0, :, h * dv:(h + 1) * dv] = o.T.astype(o_ref.dtype)


def _attn_prompt(q, k, vt, rel_bias, lam_p, head_g, lam_init):
    b, d, t = q.shape
    tile = ATTN_TILE
    assert t % tile == 0 and tile % CHUNK == 0 and tile >= MAX_DISTANCE and vt.shape[-1] == tile
    n = t // tile
    dv = d // A_HEADS
    ext = vt.shape[2] // A_HEADS
    n_chain = 2 * A_HEADS
    bias = jnp.stack([_bias_tile(rel_bias, tile, 0, tile, tile), _bias_tile(rel_bias, 0, 0, tile, tile)],
                     axis=1).swapaxes(-1, -2)
    return pl.pallas_call(
        functools.partial(_attn_prompt_kernel, lam_init=lam_init),
        grid=(b, n),
        in_specs=[pl.BlockSpec((1, d, tile), lambda i, qi: (i, 0, qi)),
                  pl.BlockSpec((1, t, d), lambda i, qi: (i, 0, 0), pipeline_mode=pl.Buffered(1)),
                  pl.BlockSpec((1,) + vt.shape[1:], lambda i, qi: (i, 0, 0, 0), pipeline_mode=pl.Buffered(1)),
                  _resident(bias.shape), _resident(lam_p.shape), _resident((dv, 1))],
        out_specs=pl.BlockSpec((1, tile, d), lambda i, qi: (i, qi, 0)),
        out_shape=jax.ShapeDtypeStruct((b, t, d), BF16),
        scratch_shapes=[pltpu.VMEM((n_chain, 1, tile), F32), pltpu.VMEM((n_chain, 1, tile), F32),
                        pltpu.VMEM((n_chain, ext, tile), F32),
                        pltpu.VMEM((n_chain, tile, tile), F32), pltpu.VMEM((n_chain, tile, tile), BF16)],
        compiler_params=_params("parallel", "arbitrary"),
        name="attn_prompt",
    )(q, k, vt, bias, lam_p, head_g.reshape(dv, 1))


def _attn_sample_kernel(q_ref, kc_ref, vtc_ref, kn_ref, vtn_ref, bias_c_ref, bias_n_ref, lam_ref, hg_ref,
                        o_ref, qbd_s, m_s, alpha_s, acc_s, s_scr, p_scr, *, lam_init):
    tq = q_ref.shape[2]
    tile = vtc_ref.shape[3]
    n_cache = vtc_ref.shape[1]
    ext = acc_s.shape[1]
    dv = o_ref.shape[2] // A_HEADS
    gw = qbd_s.shape[1]
    n_group = qbd_s.shape[0]
    n_sub = gw // tq
    row_blk = lax.broadcasted_iota(jnp.int32, (gw, gw), 0) // (gw // n_sub)
    col_blk = lax.broadcasted_iota(jnp.int32, (gw, gw), 1) // tq
    for g in range(n_group):
        qg = q_ref[0, g * gw:(g + 1) * gw, :]
        qrep = jnp.concatenate([qg] * n_sub, axis=1)
        qbd_s[g] = jnp.where(row_blk == col_blk, qrep, jnp.zeros_like(qrep))
    m_s[...] = jnp.full(m_s.shape, NEG_BIG, F32)
    acc_s[...] = jnp.zeros(acc_s.shape, F32)

    def kv_tile(k_tile, vt_tile, bias_ref, rows):
        for g in range(n_group):
            s_scr[g, :rows] = _dot(k_tile(g), qbd_s[g])
        for g in range(n_group):
            s = s_scr[g, :rows]
            if bias_ref is not None:
                s = s + bias_ref[g]
                s_scr[g, :rows] = s
            m_prev = m_s[g]
            m_new = jnp.maximum(m_prev, jnp.max(s, axis=0, keepdims=True))
            alpha_s[g] = jnp.exp2(m_prev - m_new)
            m_s[g] = m_new
        for g in range(n_group):
            for r in range(0, rows, ROW_CHUNK):
                p_scr[g, r:r + ROW_CHUNK] = jnp.exp2(s_scr[g, r:r + ROW_CHUNK] - m_s[g]).astype(BF16)
        for h in range(A_HEADS):
            g, half = divmod(h, 2)
            lanes = slice(half * 2 * tq, (half + 1) * 2 * tq)
            acc_s[h] = alpha_s[g, :, lanes] * acc_s[h] + _dot(vt_tile(h), p_scr[g, :rows, lanes])

    ones = jnp.ones((ext - dv, tile), BF16)

    def cache_tile(j, bias_ref):
        koff = pl.multiple_of(j * tile, tile)
        kv_tile(lambda g: kc_ref[0, pl.ds(koff, tile), g * gw:(g + 1) * gw],
                lambda h: jnp.concatenate([vtc_ref[0, j, h * dv:(h + 1) * dv, :], ones], axis=0),
                bias_ref, tile)

    def far_body(j, carry):
        cache_tile(j, None)
        return carry

    lax.fori_loop(0, n_cache - 1, far_body, 0)
    cache_tile(n_cache - 1, bias_c_ref)
    kv_tile(lambda g: kn_ref[0, :, g * gw:(g + 1) * gw],
            lambda h: vtn_ref[0, 0, h * ext:(h + 1) * ext, :], bias_n_ref, tq)

    lp = lam_ref[...]
    lam = (jnp.exp(jnp.sum(lp[0:1] * lp[1:2], axis=-1, keepdims=True))
           - jnp.exp(jnp.sum(lp[2:3] * lp[3:4], axis=-1, keepdims=True)) + lam_init)
    for h in range(A_HEADS):
        both = (acc_s[h, :dv] / acc_s[h, dv:dv + 1]).T
        o = both[:tq] - lam * both[tq:]
        o = _rms(o, hg_ref[...]) * (1.0 - lam_init)
        o_ref[0, :, h * dv:(h + 1) * dv] = o.astype(o_ref.dtype)


def _attn_sample(q, cache_k, cache_vt, new_k, new_vt, rel_bias, lam_p, head_g, lam_init):
    b, d, t = q.shape
    past = cache_k.shape[1]
    n_cache, _, tile = cache_vt.shape[1:]
    d_ext = new_vt.shape[2]
    assert past == n_cache * tile and past % CHUNK == 0 and tile >= MAX_DISTANCE
    dv = d // A_HEADS
    ext = d_ext // A_HEADS
    gw = 4 * t
    assert gw == 2 * dv and A_HEADS % 2 == 0
    n_group = A_HEADS // 2

    def packed(bias):
        bt = bias.swapaxes(-1, -2).reshape(n_group, 2, -1, t)
        return jnp.concatenate([bt[:, 0], bt[:, 0], bt[:, 1], bt[:, 1]], axis=-1)

    bias_c = packed(_bias_tile(rel_bias, past, past - tile, t, tile))
    bias_n = packed(_bias_tile(rel_bias, past, past, t, t))
    tok_spec = pl.BlockSpec((1, t, d), lambda i: (i, 0, 0))
    return pl.pallas_call(
        functools.partial(_attn_sample_kernel, lam_init=lam_init),
        grid=(b,),
        in_specs=[pl.BlockSpec((1, d, t), lambda i: (i, 0, 0)),
                  pl.BlockSpec((1, past, d), lambda i: (i, 0, 0)),
                  pl.BlockSpec((1, n_cache, d, tile), lambda i: (i, 0, 0, 0)),
                  tok_spec,
                  pl.BlockSpec((1, 1, d_ext, t), lambda i: (i, 0, 0, 0)),
                  _resident(bias_c.shape), _resident(bias_n.shape), _resident(lam_p.shape),
                  _resident((1, dv))],
        out_specs=tok_spec,
        out_shape=jax.ShapeDtypeStruct((b, t, d), BF16),
        scratch_shapes=[pltpu.VMEM((n_group, gw, gw), BF16),
                        pltpu.VMEM((n_group, 1, gw), F32), pltpu.VMEM((n_group, 1, gw), F32),
                        pltpu.VMEM((A_HEADS, ext, 2 * t), F32),
                        pltpu.VMEM((n_group, tile, gw), F32), pltpu.VMEM((n_group, tile, gw), BF16)],
        compiler_params=_params("parallel"),
        name="attn_sample",
    )(q, cache_k, cache_vt, new_k, new_vt, bias_c, bias_n, lam_p, head_g.reshape(1, dv))


def _cache_layouts(cache_k, cache_v, tile):
    b, past, heads, two, dh = cache_k.shape
    dv = two * dh
    d = heads * dv
    n = past // tile
    per = math.gcd(n, 4)
    span = per * tile
    k = pl.pallas_call(
        _keys_to_token_major_kernel,
        grid=(b, past // span),
        in_specs=[pl.BlockSpec((1, d, span), lambda i, j: (i, 0, j))],
        out_specs=pl.BlockSpec((1, span, d), lambda i, j: (i, j, 0)),
        out_shape=jax.ShapeDtypeStruct((b, past, d), BF16),
        compiler_params=_params("parallel", "parallel"),
        name="cache_keys_t",
    )(cache_k.transpose(0, 2, 3, 4, 1).reshape(b, d, past))
    vt = pl.pallas_call(
        functools.partial(_values_to_feature_major_kernel, heads=heads),
        grid=(b, n // per),
        in_specs=[pl.BlockSpec((1, span * heads, dv), lambda i, j: (i, j, 0))],
        out_specs=pl.BlockSpec((1, per, d, tile), lambda i, j: (i, j, 0, 0)),
        out_shape=jax.ShapeDtypeStruct((b, n, d, tile), BF16),
        compiler_params=_params("parallel", "parallel"),
        name="cache_values_t",
    )(cache_v.reshape(b, past * heads, dv))
    return k, vt


def _keys_to_token_major_kernel(k_ref, o_ref):
    for c in range(k_ref.shape[1] // LANES):
        o_ref[0, :, c * LANES:(c + 1) * LANES] = k_ref[0, c * LANES:(c + 1) * LANES, :].T.astype(BF16)


def _values_to_feature_major_kernel(v_ref, o_ref, *, heads):
    per, tile = o_ref.shape[1], o_ref.shape[3]
    dv = v_ref.shape[2]
    for j in range(per):
        for h in range(heads):
            rows = v_ref[0, pl.ds(j * tile * heads + h, tile, stride=heads), :]
            o_ref[0, j, h * dv:(h + 1) * dv, :] = rows.T.astype(BF16)


def _trunk(x, mod, mstate, past, chunk, rows, p):
    b, t, d = x.shape
    depth = p["norm_g"].shape[0]
    n_a = p["mlstm_w_in"].shape[0]
    qk = M_HEADS * (d // M_HEADS // 2)
    new_c, new_n, new_m = [], [], []
    prompt = past is None
    k_f32 = v_f32 = k_bf = vt_bf = None
    for l in range(depth):
        sh1, sc1, g1, sh2, sc2, g2 = [mod[l, :, i * d:(i + 1) * d].reshape(b, 1, d) for i in range(6)]
        if l < n_a:
            w_in = p["mlstm_w_in"][l]
            w_qko = jnp.concatenate([w_in[:, :2 * qk], w_in[:, 2 * qk + d:2 * qk + 2 * d]], axis=1)
            qko, vt, g, gt = _mlstm_proj(x, p["norm_g"][l, 0], sh1, sc1, w_qko, w_in[:, 2 * qk:2 * qk + d],
                                         w_in[:, 2 * qk + 2 * d:], rows=rows)
            hn, c_l, n_l, m_l = _mlstm(qko, vt, g, gt, p["mlstm_b_gates"][l], p["mlstm_head_g"][l],
                                       mstate[0][l], mstate[1][l], mstate[2][l], chunk=chunk)
            new_c.append(c_l)
            new_n.append(n_l)
            new_m.append(m_l)
            mixed, w_proj = hn, p["mlstm_w_out"][l]
        else:
            j = l - n_a
            lam_init = 0.8 - 0.6 * math.exp(-0.3 * l)
            dh = p["q_norm_g"].shape[-1]
            q = _q_proj(x, p["norm_g"][l, 0], sh1, sc1, p["attn_w_q"][j], p["q_norm_g"][j],
                        dh ** -0.5 * LOG2E, rows=rows)
            if prompt:
                o = _attn_prompt(q, k_bf, vt_bf, p["rel_bias"], p["attn_lambda"][j], p["attn_head_g"][j],
                                 lam_init)
            else:
                o = _attn_sample(q, past[0], past[1], k_bf, vt_bf, p["rel_bias"], p["attn_lambda"][j],
                                 p["attn_head_g"][j], lam_init)
            mixed, w_proj = o, p["attn_w_o"][j]
        x = _mixer_ffn(x, mixed, w_proj, g1, p["norm_g"][l, 1], sh2, sc2, g2, p["ffn_w_gate"][l],
                       p["ffn_w_up"][l], p["ffn_w_down"][l], rows=rows)
        if l == n_a - 1:
            k_f32, v_f32, k_bf, vt_bf = _shared_kv(x, p["kv_norm_g"], p["w_k"], p["w_v"], p["k_norm_g"],
                                                   rows=rows, attn_tile=ATTN_TILE if prompt else t)
    dh = p["k_norm_g"].shape[-1]
    return (x, jnp.stack(new_c), jnp.stack(new_n), jnp.stack(new_m),
            k_f32.reshape(b, t, A_HEADS, 2, dh), v_f32.reshape(b, t, A_HEADS, 2 * dh))


def kernel(x_prompt, x_sample, c_prompt, c_sample, state_mlstm_C, state_mlstm_n, state_mlstm_m, cache_k, cache_v, ada_w, ada_b, norm_g, mlstm_w_in, mlstm_b_gates, mlstm_head_g, mlstm_w_out, kv_norm_g, w_k, w_v, k_norm_g, attn_w_q, q_norm_g, attn_lambda, attn_head_g, attn_w_o, rel_bias, ffn_w_gate, ffn_w_up, ffn_w_down):
    bf = lambda w: w.astype(BF16)
    p = dict(norm_g=norm_g, mlstm_w_in=bf(mlstm_w_in), mlstm_b_gates=mlstm_b_gates,
             mlstm_head_g=mlstm_head_g, mlstm_w_out=bf(mlstm_w_out), kv_norm_g=kv_norm_g, w_k=bf(w_k),
             w_v=bf(w_v), k_norm_g=k_norm_g, attn_w_q=bf(attn_w_q), q_norm_g=q_norm_g,
             attn_lambda=attn_lambda, attn_head_g=attn_head_g, attn_w_o=bf(attn_w_o), rel_bias=rel_bias,
             ffn_w_gate=bf(ffn_w_gate), ffn_w_up=bf(ffn_w_up), ffn_w_down=bf(ffn_w_down))
    bp = x_prompt.shape[0]
    n_a, _, heads, dk, dv = state_mlstm_C.shape
    mod = _adaln(jnp.concatenate([c_prompt, c_sample], axis=0), ada_w, ada_b)
    zero_state = (jnp.zeros((n_a, bp, heads, dk, dv), F32), jnp.zeros((n_a, bp, heads, dk), F32),
                  jnp.zeros((n_a, bp, heads), F32))
    y_p, c_p, n_p, m_p, k_p, v_p = _trunk(x_prompt, mod[:, :bp], zero_state, None, MLSTM_PROMPT_CHUNK, 512, p)
    y_s, c_s, n_s, m_s, k_s, v_s = _trunk(x_sample, mod[:, bp:], (state_mlstm_C, state_mlstm_n, state_mlstm_m),
                                          _cache_layouts(cache_k, cache_v, ATTN_TILE), x_sample.shape[1], 512, p)
    return (y_p, y_s, c_p, n_p, m_p, k_p, v_p, c_s, n_s, m_s, k_s, v_s)
```

```python
import functools
import math

import jax
import jax.numpy as jnp
from jax import lax
from jax.experimental import pallas as pl
from jax.experimental.pallas import tpu as pltpu

F32 = jnp.float32
BF16 = jnp.bfloat16

EPS = 1e-6
NEG_BIG = -1e30
CHUNK = 64
N_BUCKETS = 32
MAX_DISTANCE = 128
M_HEADS = 8
A_HEADS = 8
GATE_PAD = 128
LANES = 128
SUBLANES = 8
ONES_ROWS = 16
LOG2E = math.log2(math.e)

V7X_VMEM_BYTES = 64 * 1024 * 1024
VMEM_LIMIT = V7X_VMEM_BYTES - 8 * 1024 * 1024

ATTN_TILE = 256
FAR_TILES = 2
CACHE_TILE = 512
MLSTM_PROMPT_CHUNK = 256
FFN_CHUNK = 256
ROW_CHUNK = 64


def _params(*sem, flags=None):
    return pltpu.CompilerParams(dimension_semantics=sem, vmem_limit_bytes=VMEM_LIMIT, flags=flags)


def _rms(x, g):
    return x * lax.rsqrt(jnp.mean(x * x, axis=-1, keepdims=True) + EPS) * g


def _log_sigmoid(x):
    return jnp.minimum(x, 0.0) - jnp.log1p(jnp.exp(-jnp.abs(x)))


def _dot(a, b):
    return jnp.dot(a, b, preferred_element_type=F32)


def _dot_nt(a, b):
    return lax.dot_general(a, b, (((1,), (1,)), ((), ())), preferred_element_type=F32)


def _dot_tn(a, b):
    return lax.dot_general(a, b, (((0,), (0,)), ((), ())), preferred_element_type=F32)


def _resident(shape):
    nd = len(shape)
    return pl.BlockSpec(shape, lambda *_: (0,) * nd, pipeline_mode=pl.Buffered(1))


def _adaln_kernel(c_ref, w_ref, b_ref, o_ref):
    c = c_ref[...]
    a = (c * jax.nn.sigmoid(c)).astype(BF16)
    o_ref[0] = _dot(a, w_ref[0].astype(BF16)) + b_ref[0]


def _adaln(c, ada_w, ada_b):
    depth, d, n = ada_w.shape
    rows = c.shape[0]
    tn = 1536
    assert n % tn == 0
    return pl.pallas_call(
        _adaln_kernel,
        grid=(depth, n // tn),
        in_specs=[
            pl.BlockSpec((rows, d), lambda l, j: (0, 0)),
            pl.BlockSpec((1, d, tn), lambda l, j: (l, 0, j)),
            pl.BlockSpec((1, 1, tn), lambda l, j: (l, 0, j)),
        ],
        out_specs=pl.BlockSpec((1, rows, tn), lambda l, j: (l, 0, j)),
        out_shape=jax.ShapeDtypeStruct((depth, rows, n), F32),
        compiler_params=_params("parallel", "parallel"),
        name="adaln",
    )(c, ada_w, ada_b.reshape(depth, 1, n))


def _modulated(x_ref, g_ref, sh_ref, sc_ref):
    x = x_ref[...]
    bb, tt, d = x.shape
    xm = _rms(x, g_ref[...]) * (1.0 + sc_ref[...]) + sh_ref[...]
    return xm.reshape(bb * tt, d).astype(BF16)


def _group_rms_feature_major(yt, gain_col):
    width = gain_col.shape[0]
    groups = []
    for g in range(yt.shape[0] // width):
        seg = yt[g * width:(g + 1) * width]
        groups.append(seg * lax.rsqrt(jnp.mean(seg * seg, axis=0, keepdims=True) + EPS) * gain_col)
    return groups


def _store_values_ext(dst_ref, lead, vt, col, width, heads):
    dv = vt.shape[0] // heads
    ext = dv + ONES_ROWS
    ones = jnp.ones((ONES_ROWS, width), BF16)
    for hh in range(heads):
        dst_ref[lead + (slice(hh * ext, hh * ext + dv), slice(None))] = vt[hh * dv:(hh + 1) * dv, col:col + width]
        dst_ref[lead + (slice(hh * ext + dv, (hh + 1) * ext), slice(None))] = ones


def _mlstm_proj_kernel(x_ref, g_ref, sh_ref, sc_ref, w_ref, wf_ref, wg_ref, wgt_ref,
                       o_ref, of_ref, og_ref, ogt_ref, *, n_chunk):
    bb, tt, n = o_ref.shape
    xm = _modulated(x_ref, g_ref, sh_ref, sc_ref)
    for c in range(n // n_chunk):
        y = _dot(xm, w_ref[:, c * n_chunk:(c + 1) * n_chunk])
        o_ref[:, :, c * n_chunk:(c + 1) * n_chunk] = y.reshape(bb, tt, n_chunk).astype(o_ref.dtype)
    vt = _dot_nt(wf_ref[...], xm).astype(BF16)
    for i in range(bb):
        _store_values_ext(of_ref, (i,), vt, i * tt, tt, M_HEADS)
        ogt_ref[i] = _dot_nt(wgt_ref[...], xm[i * tt:(i + 1) * tt])
    og_ref[...] = _dot(xm, wg_ref[...]).reshape(bb, tt, 2 * GATE_PAD)


def _q_proj_kernel(x_ref, g_ref, sh_ref, sc_ref, wt_ref, qg_ref, o_ref, *, qk_scale):
    bb, n, tt = o_ref.shape
    xm = _modulated(x_ref, g_ref, sh_ref, sc_ref)
    groups = _group_rms_feature_major(_dot_nt(wt_ref[...], xm), qg_ref[...])
    width = qg_ref.shape[0]
    for g, seg in enumerate(groups):
        seg = (seg * qk_scale).astype(o_ref.dtype)
        for i in range(bb):
            o_ref[i, g * width:(g + 1) * width, :] = seg[:, i * tt:(i + 1) * tt]


def _row_blocks(b, t, rows):
    if t >= rows:
        assert t % rows == 0
        return 1, rows
    assert rows % t == 0 and b % (rows // t) == 0
    return rows // t, t


def _token_specs(x, rows):
    b, t, d = x.shape
    bb, tt = _row_blocks(b, t, rows)
    per_b = pl.BlockSpec((bb, 1, d), lambda i, j: (i, 0, 0))
    return bb, tt, (b // bb, t // tt), pl.BlockSpec((bb, tt, d), lambda i, j: (i, j, 0)), per_b


def _mlstm_proj(x, gain, shift, scale, w, w_feat, w_gate, *, rows):
    b, t, d = x.shape
    n = w.shape[1]
    bb, tt, grid, x_spec, per_b = _token_specs(x, rows)
    tok = lambda i, j: (i, j, 0)
    feat = lambda i, j: (i, 0, j)
    ng = w_gate.shape[1]
    half = ng // 2
    wg = (jnp.zeros((d, 2 * GATE_PAD), BF16).at[:, :half].set(w_gate[:, :half])
          .at[:, GATE_PAD:GATE_PAD + half].set(w_gate[:, half:]))
    n_feat = w_feat.shape[1]
    d_ext = n_feat + M_HEADS * ONES_ROWS
    return pl.pallas_call(
        functools.partial(_mlstm_proj_kernel, n_chunk=min(n, 1024)),
        grid=grid,
        in_specs=[x_spec, _resident((1, d)), per_b, per_b, _resident((d, n)),
                  _resident((n_feat, d)), _resident((d, 2 * GATE_PAD)), _resident((ng, d))],
        out_specs=[pl.BlockSpec((bb, tt, n), tok), pl.BlockSpec((bb, d_ext, tt), feat),
                   pl.BlockSpec((bb, tt, 2 * GATE_PAD), tok), pl.BlockSpec((bb, ng, tt), feat)],
        out_shape=[jax.ShapeDtypeStruct((b, t, n), BF16), jax.ShapeDtypeStruct((b, d_ext, t), BF16),
                   jax.ShapeDtypeStruct((b, t, 2 * GATE_PAD), F32), jax.ShapeDtypeStruct((b, ng, t), F32)],
        compiler_params=_params("parallel", "parallel"), name="mlstm_proj",
    )(x, gain.reshape(1, d), shift, scale, w, w_feat.T, wg, w_gate.T)


def _q_proj(x, gain, shift, scale, w, qk_gain, qk_scale, *, rows):
    b, t, d = x.shape
    n = w.shape[1]
    bb, tt, grid, x_spec, per_b = _token_specs(x, rows)
    return pl.pallas_call(
        functools.partial(_q_proj_kernel, qk_scale=qk_scale),
        grid=grid,
        in_specs=[x_spec, _resident((1, d)), per_b, per_b, _resident((n, d)), _resident((qk_gain.shape[-1], 1))],
        out_specs=pl.BlockSpec((bb, n, tt), lambda i, j: (i, 0, j)),
        out_shape=jax.ShapeDtypeStruct((b, n, t), BF16),
        compiler_params=_params("parallel", "parallel"), name="q_proj",
    )(x, gain.reshape(1, d), shift, scale, w.T, qk_gain.reshape(-1, 1))


def _kv_kernel(x_ref, g_ref, wkt_ref, wv_ref, wvt_ref, kg_ref, k_ref, v_ref, kb_ref, vt_ref):
    x = x_ref[...]
    bb, tt, d = x.shape
    h = _rms(x, g_ref[...]).reshape(bb * tt, d).astype(BF16)
    groups = _group_rms_feature_major(_dot_nt(wkt_ref[...], h), kg_ref[...])
    per_tile = LANES // kg_ref.shape[0]
    for c in range(d // LANES):
        k_tok = jnp.concatenate(groups[c * per_tile:(c + 1) * per_tile], axis=0).T.reshape(bb, tt, LANES)
        k_ref[:, :, c * LANES:(c + 1) * LANES] = k_tok
        kb_ref[:, :, c * LANES:(c + 1) * LANES] = k_tok.astype(BF16)
    v_ref[...] = _dot(h, wv_ref[...]).reshape(bb, tt, d)
    vt = _dot_nt(wvt_ref[...], h).astype(BF16)
    per, tile = vt_ref.shape[1], vt_ref.shape[3]
    for i in range(bb):
        for j in range(per):
            _store_values_ext(vt_ref, (i, j), vt, (i * per + j) * tile, tile, A_HEADS)


def _shared_kv(x, kv_g, w_k, w_v, k_g, *, rows, attn_tile):
    b, t, d = x.shape
    bb, tt = _row_blocks(b, t, rows)
    assert tt % attn_tile == 0
    per = tt // attn_tile
    d_ext = d + A_HEADS * ONES_ROWS
    tok = lambda i, j: (i, j, 0)
    blk = pl.BlockSpec((bb, tt, d), tok)
    return pl.pallas_call(
        _kv_kernel,
        grid=(b // bb, t // tt),
        in_specs=[blk, _resident((1, d)), _resident((d, d)), _resident((d, d)), _resident((d, d)),
                  _resident((k_g.shape[-1], 1))],
        out_specs=[blk, blk, blk, pl.BlockSpec((bb, per, d_ext, attn_tile), lambda i, j: (i, j, 0, 0))],
        out_shape=[jax.ShapeDtypeStruct((b, t, d), F32), jax.ShapeDtypeStruct((b, t, d), F32),
                   jax.ShapeDtypeStruct((b, t, d), BF16),
                   jax.ShapeDtypeStruct((b, t // attn_tile, d_ext, attn_tile), BF16)],
        compiler_params=_params("parallel", "parallel"),
        name="shared_kv",
    )(x, kv_g.reshape(1, d), w_k.T, w_v, w_v.T, k_g.reshape(-1, 1))


def _mixer_ffn_kernel(x_ref, a_ref, wp_ref, gate1_ref, g_ref, sh_ref, sc_ref, gate2_ref, wg_ref, wu_ref, wd_ref,
                      o_ref, acc_ref):
    bb, tt, d = x_ref.shape
    f = wg_ref.shape[1]
    a = a_ref[...].reshape(bb * tt, a_ref.shape[-1])
    o_ref[...] = x_ref[...] + gate1_ref[...] * _dot(a, wp_ref[...]).reshape(bb, tt, d)
    xm = _modulated(o_ref, g_ref, sh_ref, sc_ref)
    for c in range(f // FFN_CHUNK):
        sl = slice(c * FFN_CHUNK, (c + 1) * FFN_CHUNK)
        hg = _dot(xm, wg_ref[:, sl])
        hu = _dot(xm, wu_ref[:, sl])
        act = (hg * jax.nn.sigmoid(hg) * hu).astype(BF16)
        part = _dot(act, wd_ref[sl, :])
        if c == 0:
            acc_ref[...] = part
        else:
            acc_ref[...] += part
    o_ref[...] += gate2_ref[...] * acc_ref[...].reshape(bb, tt, d)


def _mixer_ffn(x, a, w_proj, gate1, gain, shift, scale, gate2, w_gate, w_up, w_down, *, rows):
    b, t, d = x.shape
    k = a.shape[-1]
    f = w_gate.shape[1]
    assert f % FFN_CHUNK == 0
    bb, tt = _row_blocks(b, t, rows)
    tok = lambda i, j: (i, j, 0)
    per_b = pl.BlockSpec((bb, 1, d), lambda i, j: (i, 0, 0))
    return pl.pallas_call(
        _mixer_ffn_kernel,
        grid=(b // bb, t // tt),
        in_specs=[pl.BlockSpec((bb, tt, d), tok), pl.BlockSpec((bb, tt, k), tok), _resident((k, d)), per_b,
                  _resident((1, d)), per_b, per_b, per_b,
                  _resident((d, f)), _resident((d, f)), _resident((f, d))],
        out_specs=pl.BlockSpec((bb, tt, d), tok),
        out_shape=jax.ShapeDtypeStruct((b, t, d), F32),
        scratch_shapes=[pltpu.VMEM((bb * tt, d), F32)],
        compiler_params=_params("parallel", "parallel"),
        name="mixer_ffn",
    )(x, a, w_proj, gate1, gain.reshape(1, d), shift, scale, gate2, w_gate, w_up, w_down)


def _mlstm_kernel(q_ref, k_ref, o_ref, vt_ref, g_ref, gt_ref, brow_ref, bcol_ref, hg_ref,
                  c0_ref, n0_ref, m0_ref, h_ref, c_out, n_out, m_out,
                  ct_s, m_s, mnew_s, acol_s, br_s, inter_s, wn_s, wold_s, qk_s, qc_s, sb_s, sv_s, stat_s):
    ci = pl.program_id(1)
    length = q_ref.shape[1]
    ext = ct_s.shape[1]
    dv = ext - ONES_ROWS
    dk = q_ref.shape[2] // M_HEADS
    heads = range(M_HEADS)

    @pl.when(ci == 0)
    def _():
        ct_s[...] = jnp.zeros(ct_s.shape, F32)
        for h in heads:
            c0 = jnp.concatenate([c0_ref[0, h], jnp.zeros((dv - dk, dv), F32)], axis=0)
            ct_s[h, :dv, :] = c0.T
            ct_s[h, dv:, :dk] = jnp.broadcast_to(n0_ref[0, h:h + 1, :], (ONES_ROWS, dk))
        m_s[...] = m0_ref[0]

    row = lax.broadcasted_iota(jnp.int32, (length, length), 0)
    col = lax.broadcasted_iota(jnp.int32, (length, length), 1)
    causal = row <= col
    hi = lax.Precision.HIGHEST
    g = g_ref[0] + brow_ref[...]
    b_cols = jnp.dot((col <= row).astype(F32), _log_sigmoid(g[:, GATE_PAD:]), precision=hi,
                     preferred_element_type=F32)
    acol_s[...] = g[:, :GATE_PAD] - b_cols
    gt = gt_ref[0] + bcol_ref[...]
    ig_rows = gt[:M_HEADS]
    br_s[...] = jnp.dot(_log_sigmoid(gt[M_HEADS:]), causal.astype(F32), precision=hi,
                        preferred_element_type=F32)
    b_rows = br_s[...]
    b_end = br_s[:, length - 1:length]
    m_prev = m_s[...]
    g_rows = b_end - b_rows + ig_rows
    m_new = jnp.maximum(b_end + m_prev, jnp.max(g_rows, axis=-1, keepdims=True))
    mnew_s[...] = m_new
    wold_s[...] = jnp.exp(b_end + m_prev - m_new)
    wn_s[...] = jnp.exp(g_rows - m_new)
    inter_s[...] = b_rows + m_prev

    def q_of(h):
        return q_ref[0, :, h * dk:(h + 1) * dk]

    def k_of(h):
        return k_ref[0, :, h * dk:(h + 1) * dk] * (dk ** -0.5)

    def vt_of(h):
        return vt_ref[0, h * ext:(h + 1) * ext, :]

    for h in heads:
        qk_s[h] = _dot_nt(k_of(h), q_of(h))
        qc_s[h] = _dot_nt(ct_s[h, :, :dk].astype(BF16), q_of(h))
    for h in heads:
        inter = inter_s[h:h + 1, :]
        dmat = jnp.where(causal, acol_s[:, h:h + 1] + br_s[h:h + 1, :], -jnp.inf)
        m_t = jnp.maximum(inter, jnp.max(dmat, axis=0, keepdims=True))
        sb_s[h] = (qk_s[h] * jnp.exp(dmat - m_t)).astype(BF16)
        stat_s[0, h] = jnp.exp(inter - m_t)
        stat_s[1, h] = jnp.exp(-m_t)
    for h in heads:
        sv_s[h] = _dot(vt_of(h), sb_s[h])
    for h in heads:
        w_inter = stat_s[0, h]
        tot = w_inter * qc_s[h] + sv_s[h]
        hh = tot[:dv] / jnp.maximum(jnp.abs(tot[dv:dv + 1]), stat_s[1, h])
        hh = hh * lax.rsqrt(jnp.mean(hh * hh, axis=0, keepdims=True) + EPS)
        og = o_ref[0, :, h * dv:(h + 1) * dv].astype(F32)
        h_ref[0, :, h * dv:(h + 1) * dv] = (_to_token_major(hh) * hg_ref[:, h * dv:(h + 1) * dv]
                                            * jax.nn.sigmoid(og)).astype(h_ref.dtype)
    for h in heads:
        vtw = (vt_of(h).astype(F32) * wn_s[h:h + 1, :]).astype(BF16)
        ct_s[h, :, :dk] = wold_s[h:h + 1, :] * ct_s[h, :, :dk] + _dot(vtw, k_of(h))
    m_s[...] = mnew_s[...]

    @pl.when(ci == pl.num_programs(1) - 1)
    def _():
        for h in heads:
            c_out[0, h] = ct_s[h, :dv, :].T[:dk]
            n_out[0, h:h + 1, :] = ct_s[h, dv:dv + 1, :dk]
        m_out[0] = m_s[...]


def _to_token_major(x):
    dv, length = x.shape
    if length % LANES:
        pad = LANES - length % LANES
        x = jnp.concatenate([x, jnp.zeros((dv, pad), x.dtype)], axis=1)
    return x.T[:length]


def _mlstm(qko, vt, g, gt, b_gates, head_g, c0, n0, m0, *, chunk):
    b, t, _ = qko.shape
    heads, dk, dv = c0.shape[1:]
    d = heads * dv
    qk = heads * dk
    ext = dv + ONES_ROWS
    assert qk * 2 == d and t % chunk == 0 and heads == M_HEADS and vt.shape[1] == heads * ext
    nc = t // chunk
    two_h = 2 * heads
    brow = (jnp.zeros((1, 2 * GATE_PAD), F32).at[0, :heads].set(b_gates[:heads])
            .at[0, GATE_PAD:GATE_PAD + heads].set(b_gates[heads:]))
    st4 = lambda i, c: (i, 0, 0, 0)
    st3 = lambda i, c: (i, 0, 0)
    outs = pl.pallas_call(
        _mlstm_kernel,
        grid=(b, nc),
        in_specs=[
            pl.BlockSpec((1, chunk, qk), lambda i, c: (i, c, 0)),
            pl.BlockSpec((1, chunk, qk), lambda i, c: (i, c, 1)),
            pl.BlockSpec((1, chunk, d), lambda i, c: (i, c, 1)),
            pl.BlockSpec((1, heads * ext, chunk), lambda i, c: (i, 0, c)),
            pl.BlockSpec((1, chunk, 2 * GATE_PAD), lambda i, c: (i, c, 0)),
            pl.BlockSpec((1, two_h, chunk), lambda i, c: (i, 0, c)),
            _resident((1, 2 * GATE_PAD)), _resident((two_h, 1)), _resident((1, d)),
            pl.BlockSpec((1, heads, dk, dv), st4),
            pl.BlockSpec((1, heads, dk), st3),
            pl.BlockSpec((1, heads, 1), st3),
        ],
        out_specs=[
            pl.BlockSpec((1, chunk, d), lambda i, c: (i, c, 0)),
            pl.BlockSpec((1, heads, dk, dv), st4),
            pl.BlockSpec((1, heads, dk), st3),
            pl.BlockSpec((1, heads, 1), st3),
        ],
        out_shape=[
            jax.ShapeDtypeStruct((b, t, d), BF16),
            jax.ShapeDtypeStruct((b, heads, dk, dv), F32),
            jax.ShapeDtypeStruct((b, heads, dk), F32),
            jax.ShapeDtypeStruct((b, heads, 1), F32),
        ],
        scratch_shapes=[pltpu.VMEM((heads, ext, dv), F32),
                        pltpu.VMEM((heads, 1), F32), pltpu.VMEM((heads, 1), F32),
                        pltpu.VMEM((chunk, GATE_PAD), F32),
                        pltpu.VMEM((heads, chunk), F32), pltpu.VMEM((heads, chunk), F32),
                        pltpu.VMEM((heads, chunk), F32), pltpu.VMEM((heads, 1), F32),
                        pltpu.VMEM((heads, chunk, chunk), F32),
                        pltpu.VMEM((heads, ext, chunk), F32),
                        pltpu.VMEM((heads, chunk, chunk), BF16),
                        pltpu.VMEM((heads, ext, chunk), F32),
                        pltpu.VMEM((2, heads, 1, chunk), F32)],
        compiler_params=_params("parallel", "arbitrary"),
        name="mlstm",
    )(qko, qko, qko, vt, g, gt, brow, b_gates.reshape(two_h, 1), head_g.reshape(1, d),
      c0, n0, m0.reshape(b, heads, 1))
    hn, c_new, n_new, m_new = outs
    return hn, c_new, n_new, m_new.reshape(b, heads)


def _rel_bucket(rel):
    nb = N_BUCKETS // 2
    max_exact = nb // 2
    ret = jnp.where(rel > 0, nb, 0)
    n = jnp.abs(rel)
    large = max_exact + (jnp.log(jnp.maximum(n, 1).astype(F32) / max_exact)
                         / math.log(MAX_DISTANCE / max_exact) * (nb - max_exact)).astype(jnp.int32)
    large = jnp.minimum(large, nb - 1)
    return ret + jnp.where(n < max_exact, n, large)


def _bias_tile(rel_bias, q0, k0, tq, tk):
    table = rel_bias.astype(F32)
    span = tq + tk
    r = jnp.arange(span)
    r = jnp.where(r < tk, r, r - span)
    vec = (table[_rel_bucket(r + (k0 - q0))] - table[N_BUCKETS // 2 - 1]).T
    bias = jnp.tile(vec, (1, tq))[:, :tq * (span - 1)].reshape(-1, tq, span - 1)[:, :, :tk]
    q_chunk = (q0 + lax.broadcasted_iota(jnp.int32, (tq, tk), 0)) // CHUNK
    k_chunk = (k0 + lax.broadcasted_iota(jnp.int32, (tq, tk), 1)) // CHUNK
    return jnp.where((k_chunk <= q_chunk)[None], bias * LOG2E, NEG_BIG)


def _attn_prompt_kernel(q_ref, k_ref, vt_ref, bias_ref, lam_ref, hg_ref, o_ref,
                        m_s, alpha_s, acc_s, s_scr, p_scr, *, lam_init):
    qi = pl.program_id(1)
    tile = q_ref.shape[2]
    ext = acc_s.shape[1]
    dv = o_ref.shape[2] // A_HEADS
    dh = dv // 2
    n_chain = 2 * A_HEADS
    m_s[...] = jnp.full(m_s.shape, NEG_BIG, F32)
    acc_s[...] = jnp.zeros(acc_s.shape, F32)

    def kv_tiles(j, slot, count):
        rows = count * tile
        koff = pl.multiple_of(j * tile, tile)
        for hc in range(n_chain):
            lo = hc * dh
            s_scr[hc, :rows] = _dot(k_ref[0, pl.ds(koff, rows), lo:lo + dh], q_ref[0, lo:lo + dh, :])
        for hc in range(n_chain):
            m_prev = m_s[hc]
            m_part = None
            for r in range(0, rows, ROW_CHUNK):
                s = s_scr[hc, r:r + ROW_CHUNK]
                if slot is not None:
                    s = s + bias_ref[hc // 2, slot, r:r + ROW_CHUNK]
                    s_scr[hc, r:r + ROW_CHUNK] = s
                part = jnp.max(s.reshape(ROW_CHUNK // SUBLANES, SUBLANES, tile), axis=0)
                m_part = part if m_part is None else jnp.maximum(m_part, part)
            m_new = jnp.maximum(m_prev, jnp.max(m_part, axis=0, keepdims=True))
            alpha_s[hc] = jnp.exp2(m_prev - m_new)
            m_s[hc] = m_new
        for hc in range(n_chain):
            for r in range(0, rows, ROW_CHUNK):
                p_scr[hc, r:r + ROW_CHUNK] = jnp.exp2(s_scr[hc, r:r + ROW_CHUNK] - m_s[hc]).astype(BF16)
        for hc in range(n_chain):
            h = hc // 2
            pv = _dot(vt_ref[0, j, h * ext:(h + 1) * ext, :], p_scr[hc, :tile])
            for t in range(1, count):
                pv = pv + _dot(vt_ref[0, j + t, h * ext:(h + 1) * ext, :], p_scr[hc, t * tile:(t + 1) * tile])
            acc_s[hc] = alpha_s[hc] * acc_s[hc] + pv

    n_far = jnp.maximum(qi - 1, 0)

    def far_body(i, carry):
        kv_tiles(FAR_TILES * i, None, FAR_TILES)
        return carry

    lax.fori_loop(0, n_far // FAR_TILES, far_body, 0)

    def far_rest(j, carry):
        kv_tiles(j, None, 1)
        return carry

    lax.fori_loop(n_far - n_far % FAR_TILES, n_far, far_rest, 0)

    @pl.when(qi > 0)
    def _():
        kv_tiles(qi - 1, 0, 1)

    kv_tiles(qi, 1, 1)


    lp = lam_ref[...]
    lam = (jnp.exp(jnp.sum(lp[0:1] * lp[1:2], axis=-1, keepdims=True))
           - jnp.exp(jnp.sum(lp[2:3] * lp[3:4], axis=-1, keepdims=True)) + lam_init)
    for h in range(A_HEADS):
        o = (acc_s[2 * h, :dv] / acc_s[2 * h, dv:dv + 1]
             - lam * (acc_s[2 * h + 1, :dv] / acc_s[2 * h + 1, dv:dv + 1]))
        o = o * lax.rsqrt(jnp.mean(o * o, axis=0, keepdims=True) + EPS) * hg_ref[...] * (1.0 - lam_init)
        o_ref[0, :, h * dv:(h + 1) * dv] = o.T.astype(o_ref.dtype)


def _attn_prompt(q, k, vt, rel_bias, lam_p, head_g, lam_init):
    b, d, t = q.shape
    tile = ATTN_TILE
    assert t % tile == 0 and tile % CHUNK == 0 and tile >= MAX_DISTANCE and vt.shape[-1] == tile
    n = t // tile
    dv = d // A_HEADS
    ext = vt.shape[2] // A_HEADS
    n_chain = 2 * A_HEADS
    bias = jnp.stack([_bias_tile(rel_bias, tile, 0, tile, tile), _bias_tile(rel_bias, 0, 0, tile, tile)],
                     axis=1).swapaxes(-1, -2)
    return pl.pallas_call(
        functools.partial(_attn_prompt_kernel, lam_init=lam_init),
        grid=(b, n),
        in_specs=[pl.BlockSpec((1, d, tile), lambda i, qi: (i, 0, qi)),
                  pl.BlockSpec((1, t, d), lambda i, qi: (i, 0, 0), pipeline_mode=pl.Buffered(1)),
                  pl.BlockSpec((1,) + vt.shape[1:], lambda i, qi: (i, 0, 0, 0), pipeline_mode=pl.Buffered(1)),
                  _resident(bias.shape), _resident(lam_p.shape), _resident((dv, 1))],
        out_specs=pl.BlockSpec((1, tile, d), lambda i, qi: (i, qi, 0)),
        out_shape=jax.ShapeDtypeStruct((b, t, d), BF16),
        scratch_shapes=[pltpu.VMEM((n_chain, 1, tile), F32), pltpu.VMEM((n_chain, 1, tile), F32),
                        pltpu.VMEM((n_chain, ext, tile), F32),
                        pltpu.VMEM((n_chain, FAR_TILES * tile, tile), F32),
                        pltpu.VMEM((n_chain, FAR_TILES * tile, tile), BF16)],
        compiler_params=_params("parallel", "arbitrary"),
        name="attn_prompt",
    )(q, k, vt, bias, lam_p, head_g.reshape(dv, 1))


def _attn_sample_kernel(q_ref, kc_ref, vtc_ref, kn_ref, vtn_ref, bias_c_ref, bias_n_ref, lam_ref, hg_ref,
                        o_ref, qbd_s, m_s, alpha_s, acc_s, s_scr, p_scr, *, lam_init):
    tq = q_ref.shape[2]
    tile = vtc_ref.shape[3]
    n_cache = vtc_ref.shape[1]
    ext = acc_s.shape[1]
    dv = o_ref.shape[2] // A_HEADS
    gw = qbd_s.shape[1]
    n_group = qbd_s.shape[0]
    n_sub = gw // tq
    row_blk = lax.broadcasted_iota(jnp.int32, (gw, gw), 0) // (gw // n_sub)
    col_blk = lax.broadcasted_iota(jnp.int32, (gw, gw), 1) // tq
    for g in range(n_group):
        qg = q_ref[0, g * gw:(g + 1) * gw, :]
        qrep = jnp.concatenate([qg] * n_sub, axis=1)
        qbd_s[g] = jnp.where(row_blk == col_blk, qrep, jnp.zeros_like(qrep))
    m_s[...] = jnp.full(m_s.shape, NEG_BIG, F32)
    acc_s[...] = jnp.zeros(acc_s.shape, F32)

    def kv_tile(k_tile, vt_tile, bias_ref, rows):
        for g in range(n_group):
            s_scr[g, :rows] = _dot(k_tile(g), qbd_s[g])
        for g in range(n_group):
            s = s_scr[g, :rows]
            if bias_ref is not None:
                s = s + bias_ref[g]
                s_scr[g, :rows] = s
            m_prev = m_s[g]
            m_new = jnp.maximum(m_prev, jnp.max(s, axis=0, keepdims=True))
            alpha_s[g] = jnp.exp2(m_prev - m_new)
            m_s[g] = m_new
        for g in range(n_group):
            for r in range(0, rows, ROW_CHUNK):
                p_scr[g, r:r + ROW_CHUNK] = jnp.exp2(s_scr[g, r:r + ROW_CHUNK] - m_s[g]).astype(BF16)
        for h in range(A_HEADS):
            g, half = divmod(h, 2)
            lanes = slice(half * 2 * tq, (half + 1) * 2 * tq)
            acc_s[h] = alpha_s[g, :, lanes] * acc_s[h] + _dot(vt_tile(h), p_scr[g, :rows, lanes])

    ones = jnp.ones((ext - dv, tile), BF16)

    def cache_tile(j, bias_ref):
        koff = pl.multiple_of(j * tile, tile)
        kv_tile(lambda g: kc_ref[0, pl.ds(koff, tile), g * gw:(g + 1) * gw],
                lambda h: jnp.concatenate([vtc_ref[0, j, h * dv:(h + 1) * dv, :], ones], axis=0),
                bias_ref, tile)

    def far_body(j, carry):
        cache_tile(j, None)
        return carry

    lax.fori_loop(0, n_cache - 1, far_body, 0)
    cache_tile(n_cache - 1, bias_c_ref)
    kv_tile(lambda g: kn_ref[0, :, g * gw:(g + 1) * gw],
            lambda h: vtn_ref[0, 0, h * ext:(h + 1) * ext, :], bias_n_ref, tq)

    lp = lam_ref[...]
    lam = (jnp.exp(jnp.sum(lp[0:1] * lp[1:2], axis=-1, keepdims=True))
           - jnp.exp(jnp.sum(lp[2:3] * lp[3:4], axis=-1, keepdims=True)) + lam_init)
    for h in range(A_HEADS):
        both = (acc_s[h, :dv] / acc_s[h, dv:dv + 1]).T
        o = both[:tq] - lam * both[tq:]
        o = _rms(o, hg_ref[...]) * (1.0 - lam_init)
        o_ref[0, :, h * dv:(h + 1) * dv] = o.astype(o_ref.dtype)


def _attn_sample(q, cache_k, cache_vt, new_k, new_vt, rel_bias, lam_p, head_g, lam_init):
    b, d, t = q.shape
    past = cache_k.shape[1]
    n_cache, _, tile = cache_vt.shape[1:]
    d_ext = new_vt.shape[2]
    assert past == n_cache * tile and past % CHUNK == 0 and tile >= MAX_DISTANCE
    dv = d // A_HEADS
    ext = d_ext // A_HEADS
    gw = 4 * t
    assert gw == 2 * dv and A_HEADS % 2 == 0
    n_group = A_HEADS // 2

    def packed(bias):
        bt = bias.swapaxes(-1, -2).reshape(n_group, 2, -1, t)
        return jnp.concatenate([bt[:, 0], bt[:, 0], bt[:, 1], bt[:, 1]], axis=-1)

    bias_c = packed(_bias_tile(rel_bias, past, past - tile, t, tile))
    bias_n = packed(_bias_tile(rel_bias, past, past, t, t))
    tok_spec = pl.BlockSpec((1, t, d), lambda i: (i, 0, 0))
    return pl.pallas_call(
        functools.partial(_attn_sample_kernel, lam_init=lam_init),
        grid=(b,),
        in_specs=[pl.BlockSpec((1, d, t), lambda i: (i, 0, 0)),
                  pl.BlockSpec((1, past, d), lambda i: (i, 0, 0)),
                  pl.BlockSpec((1, n_cache, d, tile), lambda i: (i, 0, 0, 0)),
                  tok_spec,
                  pl.BlockSpec((1, 1, d_ext, t), lambda i: (i, 0, 0, 0)),
                  _resident(bias_c.shape), _resident(bias_n.shape), _resident(lam_p.shape),
                  _resident((1, dv))],
        out_specs=tok_spec,
        out_shape=jax.ShapeDtypeStruct((b, t, d), BF16),
        scratch_shapes=[pltpu.VMEM((n_group, gw, gw), BF16),
                        pltpu.VMEM((n_group, 1, gw), F32), pltpu.VMEM((n_group, 1, gw), F32),
                        pltpu.VMEM((A_HEADS, ext, 2 * t), F32),
                        pltpu.VMEM((n_group, tile, gw), F32), pltpu.VMEM((n_group, tile, gw), BF16)],
        compiler_params=_params("parallel"),
        name="attn_sample",
    )(q, cache_k, cache_vt, new_k, new_vt, bias_c, bias_n, lam_p, head_g.reshape(1, dv))


def _cache_layouts(cache_k, cache_v, tile):
    b, past, heads, two, dh = cache_k.shape
    dv = two * dh
    d = heads * dv
    n = past // tile
    per = math.gcd(n, 4)
    span = per * tile
    k = pl.pallas_call(
        _keys_to_token_major_kernel,
        grid=(b, past // span),
        in_specs=[pl.BlockSpec((1, d, span), lambda i, j: (i, 0, j))],
        out_specs=pl.BlockSpec((1, span, d), lambda i, j: (i, j, 0)),
        out_shape=jax.ShapeDtypeStruct((b, past, d), BF16),
        compiler_params=_params("parallel", "parallel"),
        name="cache_keys_t",
    )(cache_k.transpose(0, 2, 3, 4, 1).reshape(b, d, past))
    vt = pl.pallas_call(
        functools.partial(_values_to_feature_major_kernel, heads=heads),
        grid=(b, n // per),
        in_specs=[pl.BlockSpec((1, span * heads, dv), lambda i, j: (i, j, 0))],
        out_specs=pl.BlockSpec((1, per, d, tile), lambda i, j: (i, j, 0, 0)),
        out_shape=jax.ShapeDtypeStruct((b, n, d, tile), BF16),
        compiler_params=_params("parallel", "parallel"),
        name="cache_values_t",
    )(cache_v.reshape(b, past * heads, dv))
    return k, vt


def _keys_to_token_major_kernel(k_ref, o_ref):
    for c in range(k_ref.shape[1] // LANES):
        o_ref[0, :, c * LANES:(c + 1) * LANES] = k_ref[0, c * LANES:(c + 1) * LANES, :].T.astype(BF16)


def _values_to_feature_major_kernel(v_ref, o_ref, *, heads):
    per, tile = o_ref.shape[1], o_ref.shape[3]
    dv = v_ref.shape[2]
    for j in range(per):
        for h in range(heads):
            rows = v_ref[0, pl.ds(j * tile * heads + h, tile, stride=heads), :]
            o_ref[0, j, h * dv:(h + 1) * dv, :] = rows.T.astype(BF16)


def _trunk(x, mod, mstate, past, chunk, rows, p):
    b, t, d = x.shape
    depth = p["norm_g"].shape[0]
    n_a = p["mlstm_w_in"].shape[0]
    qk = M_HEADS * (d // M_HEADS // 2)
    new_c, new_n, new_m = [], [], []
    prompt = past is None
    k_f32 = v_f32 = k_bf = vt_bf = None
    for l in range(depth):
        sh1, sc1, g1, sh2, sc2, g2 = [mod[l, :, i * d:(i + 1) * d].reshape(b, 1, d) for i in range(6)]
        if l < n_a:
            w_in = p["mlstm_w_in"][l]
            w_qko = jnp.concatenate([w_in[:, :2 * qk], w_in[:, 2 * qk + d:2 * qk + 2 * d]], axis=1)
            qko, vt, g, gt = _mlstm_proj(x, p["norm_g"][l, 0], sh1, sc1, w_qko, w_in[:, 2 * qk:2 * qk + d],
                                         w_in[:, 2 * qk + 2 * d:], rows=rows)
            hn, c_l, n_l, m_l = _mlstm(qko, vt, g, gt, p["mlstm_b_gates"][l], p["mlstm_head_g"][l],
                                       mstate[0][l], mstate[1][l], mstate[2][l], chunk=chunk)
            new_c.append(c_l)
            new_n.append(n_l)
            new_m.append(m_l)
            mixed, w_proj = hn, p["mlstm_w_out"][l]
        else:
            j = l - n_a
            lam_init = 0.8 - 0.6 * math.exp(-0.3 * l)
            dh = p["q_norm_g"].shape[-1]
            q = _q_proj(x, p["norm_g"][l, 0], sh1, sc1, p["attn_w_q"][j], p["q_norm_g"][j],
                        dh ** -0.5 * LOG2E, rows=rows)
            if prompt:
                o = _attn_prompt(q, k_bf, vt_bf, p["rel_bias"], p["attn_lambda"][j], p["attn_head_g"][j],
                                 lam_init)
            else:
                o = _attn_sample(q, past[0], past[1], k_bf, vt_bf, p["rel_bias"], p["attn_lambda"][j],
                                 p["attn_head_g"][j], lam_init)
            mixed, w_proj = o, p["attn_w_o"][j]
        x = _mixer_ffn(x, mixed, w_proj, g1, p["norm_g"][l, 1], sh2, sc2, g2, p["ffn_w_gate"][l],
                       p["ffn_w_up"][l], p["ffn_w_down"][l], rows=rows)
        if l == n_a - 1:
            k_f32, v_f32, k_bf, vt_bf = _shared_kv(x, p["kv_norm_g"], p["w_k"], p["w_v"], p["k_norm_g"],
                                                   rows=rows, attn_tile=ATTN_TILE if prompt else t)
    dh = p["k_norm_g"].shape[-1]
    return (x, jnp.stack(new_c), jnp.stack(new_n), jnp.stack(new_m),
            k_f32.reshape(b, t, A_HEADS, 2, dh), v_f32.reshape(b, t, A_HEADS, 2 * dh))


def kernel(x_prompt, x_sample, c_prompt, c_sample, state_mlstm_C, state_mlstm_n, state_mlstm_m, cache_k, cache_v, ada_w, ada_b, norm_g, mlstm_w_in, mlstm_b_gates, mlstm_head_g, mlstm_w_out, kv_norm_g, w_k, w_v, k_norm_g, attn_w_q, q_norm_g, attn_lambda, attn_head_g, attn_w_o, rel_bias, ffn_w_gate, ffn_w_up, ffn_w_down):
    bf = lambda w: w.astype(BF16)
    p = dict(norm_g=norm_g, mlstm_w_in=bf(mlstm_w_in), mlstm_b_gates=mlstm_b_gates,
             mlstm_head_g=mlstm_head_g, mlstm_w_out=bf(mlstm_w_out), kv_norm_g=kv_norm_g, w_k=bf(w_k),
             w_v=bf(w_v), k_norm_g=k_norm_g, attn_w_q=bf(attn_w_q), q_norm_g=q_norm_g,
             attn_lambda=attn_lambda, attn_head_g=attn_head_g, attn_w_o=bf(attn_w_o), rel_bias=rel_bias,
             ffn_w_gate=bf(ffn_w_gate), ffn_w_up=bf(ffn_w_up), ffn_w_down=bf(ffn_w_down))
    bp = x_prompt.shape[0]
    n_a, _, heads, dk, dv = state_mlstm_C.shape
    mod = _adaln(jnp.concatenate([c_prompt, c_sample], axis=0), ada_w, ada_b)
    zero_state = (jnp.zeros((n_a, bp, heads, dk, dv), F32), jnp.zeros((n_a, bp, heads, dk), F32),
                  jnp.zeros((n_a, bp, heads), F32))
    y_p, c_p, n_p, m_p, k_p, v_p = _trunk(x_prompt, mod[:, :bp], zero_state, None, MLSTM_PROMPT_CHUNK, 512, p)
    y_s, c_s, n_s, m_s, k_s, v_s = _trunk(x_sample, mod[:, bp:], (state_mlstm_C, state_mlstm_n, state_mlstm_m),
                                          _cache_layouts(cache_k, cache_v, CACHE_TILE), x_sample.shape[1], 512, p)
    return (y_p, y_s, c_p, n_p, m_p, k_p, v_p, c_s, n_s, m_s, k_s, v_s)
```

```python
import functools
import math

import jax
import jax.numpy as jnp
from jax import lax
from jax.experimental import pallas as pl
from jax.experimental.pallas import tpu as pltpu

F32 = jnp.float32
BF16 = jnp.bfloat16

EPS = 1e-6
NEG_BIG = -1e30
CHUNK = 64
N_BUCKETS = 32
MAX_DISTANCE = 128
M_HEADS = 8
A_HEADS = 8
GATE_PAD = 128
LANES = 128
SUBLANES = 8
ONES_ROWS = 16
LOG2E = math.log2(math.e)

V7X_VMEM_BYTES = 64 * 1024 * 1024
VMEM_LIMIT = V7X_VMEM_BYTES - 8 * 1024 * 1024

ATTN_TILE = 256
FAR_TILES = 2
CACHE_TILE = 512
MLSTM_PROMPT_CHUNK = 256
FFN_CHUNK = 256
ROW_CHUNK = 64


def _params(*sem, flags=None):
    return pltpu.CompilerParams(dimension_semantics=sem, vmem_limit_bytes=VMEM_LIMIT, flags=flags)


def _rms(x, g):
    return x * lax.rsqrt(jnp.mean(x * x, axis=-1, keepdims=True) + EPS) * g


def _log_sigmoid(x):
    return jnp.minimum(x, 0.0) - jnp.log1p(jnp.exp(-jnp.abs(x)))


def _dot(a, b):
    return jnp.dot(a, b, preferred_element_type=F32)


def _dot_nt(a, b):
    return lax.dot_general(a, b, (((1,), (1,)), ((), ())), preferred_element_type=F32)


def _dot_tn(a, b):
    return lax.dot_general(a, b, (((0,), (0,)), ((), ())), preferred_element_type=F32)


def _resident(shape):
    nd = len(shape)
    return pl.BlockSpec(shape, lambda *_: (0,) * nd, pipeline_mode=pl.Buffered(1))


def _adaln_kernel(c_ref, w_ref, b_ref, o_ref):
    c = c_ref[...]
    a = (c * jax.nn.sigmoid(c)).astype(BF16)
    o_ref[0] = _dot(a, w_ref[0].astype(BF16)) + b_ref[0]


def _adaln(c, ada_w, ada_b):
    depth, d, n = ada_w.shape
    rows = c.shape[0]
    tn = 1536
    assert n % tn == 0
    return pl.pallas_call(
        _adaln_kernel,
        grid=(depth, n // tn),
        in_specs=[
            pl.BlockSpec((rows, d), lambda l, j: (0, 0)),
            pl.BlockSpec((1, d, tn), lambda l, j: (l, 0, j)),
            pl.BlockSpec((1, 1, tn), lambda l, j: (l, 0, j)),
        ],
        out_specs=pl.BlockSpec((1, rows, tn), lambda l, j: (l, 0, j)),
        out_shape=jax.ShapeDtypeStruct((depth, rows, n), F32),
        compiler_params=_params("parallel", "parallel"),
        name="adaln",
    )(c, ada_w, ada_b.reshape(depth, 1, n))


def _modulated(x_ref, g_ref, sh_ref, sc_ref):
    x = x_ref[...]
    bb, tt, d = x.shape
    xm = _rms(x, g_ref[...]) * (1.0 + sc_ref[...]) + sh_ref[...]
    return xm.reshape(bb * tt, d).astype(BF16)


def _group_rms_feature_major(yt, gain_col):
    width = gain_col.shape[0]
    groups = []
    for g in range(yt.shape[0] // width):
        seg = yt[g * width:(g + 1) * width]
        groups.append(seg * lax.rsqrt(jnp.mean(seg * seg, axis=0, keepdims=True) + EPS) * gain_col)
    return groups


def _store_values_ext(dst_ref, lead, vt, col, width, heads):
    dv = vt.shape[0] // heads
    ext = dv + ONES_ROWS
    ones = jnp.ones((ONES_ROWS, width), BF16)
    for hh in range(heads):
        dst_ref[lead + (slice(hh * ext, hh * ext + dv), slice(None))] = vt[hh * dv:(hh + 1) * dv, col:col + width]
        dst_ref[lead + (slice(hh * ext + dv, (hh + 1) * ext), slice(None))] = ones


def _mlstm_proj_kernel(x_ref, g_ref, sh_ref, sc_ref, w_ref, wf_ref, wg_ref, wgt_ref,
                       o_ref, of_ref, og_ref, ogt_ref, *, n_chunk):
    bb, tt, n = o_ref.shape
    xm = _modulated(x_ref, g_ref, sh_ref, sc_ref)
    for c in range(n // n_chunk):
        y = _dot(xm, w_ref[:, c * n_chunk:(c + 1) * n_chunk])
        o_ref[:, :, c * n_chunk:(c + 1) * n_chunk] = y.reshape(bb, tt, n_chunk).astype(o_ref.dtype)
    vt = _dot_nt(wf_ref[...], xm).astype(BF16)
    for i in range(bb):
        _store_values_ext(of_ref, (i,), vt, i * tt, tt, M_HEADS)
        ogt_ref[i] = _dot_nt(wgt_ref[...], xm[i * tt:(i + 1) * tt])
    og_ref[...] = _dot(xm, wg_ref[...]).reshape(bb, tt, 2 * GATE_PAD)


def _q_proj_kernel(x_ref, g_ref, sh_ref, sc_ref, wt_ref, qg_ref, o_ref, *, qk_scale):
    bb, n, tt = o_ref.shape
    xm = _modulated(x_ref, g_ref, sh_ref, sc_ref)
    groups = _group_rms_feature_major(_dot_nt(wt_ref[...], xm), qg_ref[...])
    width = qg_ref.shape[0]
    for g, seg in enumerate(groups):
        seg = (seg * qk_scale).astype(o_ref.dtype)
        for i in range(bb):
            o_ref[i, g * width:(g + 1) * width, :] = seg[:, i * tt:(i + 1) * tt]


def _row_blocks(b, t, rows):
    if t >= rows:
        assert t % rows == 0
        return 1, rows
    assert rows % t == 0 and b % (rows // t) == 0
    return rows // t, t


def _token_specs(x, rows):
    b, t, d = x.shape
    bb, tt = _row_blocks(b, t, rows)
    per_b = pl.BlockSpec((bb, 1, d), lambda i, j: (i, 0, 0))
    return bb, tt, (b // bb, t // tt), pl.BlockSpec((bb, tt, d), lambda i, j: (i, j, 0)), per_b


def _mlstm_proj(x, gain, shift, scale, w, w_feat, w_gate, *, rows):
    b, t, d = x.shape
    n = w.shape[1]
    bb, tt, grid, x_spec, per_b = _token_specs(x, rows)
    tok = lambda i, j: (i, j, 0)
    feat = lambda i, j: (i, 0, j)
    ng = w_gate.shape[1]
    half = ng // 2
    wg = (jnp.zeros((d, 2 * GATE_PAD), BF16).at[:, :half].set(w_gate[:, :half])
          .at[:, GATE_PAD:GATE_PAD + half].set(w_gate[:, half:]))
    n_feat = w_feat.shape[1]
    d_ext = n_feat + M_HEADS * ONES_ROWS
    return pl.pallas_call(
        functools.partial(_mlstm_proj_kernel, n_chunk=min(n, 1024)),
        grid=grid,
        in_specs=[x_spec, _resident((1, d)), per_b, per_b, _resident((d, n)),
                  _resident((n_feat, d)), _resident((d, 2 * GATE_PAD)), _resident((ng, d))],
        out_specs=[pl.BlockSpec((bb, tt, n), tok), pl.BlockSpec((bb, d_ext, tt), feat),
                   pl.BlockSpec((bb, tt, 2 * GATE_PAD), tok), pl.BlockSpec((bb, ng, tt), feat)],
        out_shape=[jax.ShapeDtypeStruct((b, t, n), BF16), jax.ShapeDtypeStruct((b, d_ext, t), BF16),
                   jax.ShapeDtypeStruct((b, t, 2 * GATE_PAD), F32), jax.ShapeDtypeStruct((b, ng, t), F32)],
        compiler_params=_params("parallel", "parallel"), name="mlstm_proj",
    )(x, gain.reshape(1, d), shift, scale, w, w_feat.T, wg, w_gate.T)


def _q_proj(x, gain, shift, scale, w, qk_gain, qk_scale, *, rows):
    b, t, d = x.shape
    n = w.shape[1]
    bb, tt, grid, x_spec, per_b = _token_specs(x, rows)
    return pl.pallas_call(
        functools.partial(_q_proj_kernel, qk_scale=qk_scale),
        grid=grid,
        in_specs=[x_spec, _resident((1, d)), per_b, per_b, _resident((n, d)), _resident((qk_gain.shape[-1], 1))],
        out_specs=pl.BlockSpec((bb, n, tt), lambda i, j: (i, 0, j)),
        out_shape=jax.ShapeDtypeStruct((b, n, t), BF16),
        compiler_params=_params("parallel", "parallel"), name="q_proj",
    )(x, gain.reshape(1, d), shift, scale, w.T, qk_gain.reshape(-1, 1))


def _kv_kernel(x_ref, g_ref, wkt_ref, wv_ref, wvt_ref, kg_ref, k_ref, v_ref, kb_ref, vt_ref):
    x = x_ref[...]
    bb, tt, d = x.shape
    h = _rms(x, g_ref[...]).reshape(bb * tt, d).astype(BF16)
    groups = _group_rms_feature_major(_dot_nt(wkt_ref[...], h), kg_ref[...])
    per_tile = LANES // kg_ref.shape[0]
    for c in range(d // LANES):
        k_tok = jnp.concatenate(groups[c * per_tile:(c + 1) * per_tile], axis=0).T.reshape(bb, tt, LANES)
        k_ref[:, :, c * LANES:(c + 1) * LANES] = k_tok
        kb_ref[:, :, c * LANES:(c + 1) * LANES] = k_tok.astype(BF16)
    v_ref[...] = _dot(h, wv_ref[...]).reshape(bb, tt, d)
    vt = _dot_nt(wvt_ref[...], h).astype(BF16)
    per, tile = vt_ref.shape[1], vt_ref.shape[3]
    for i in range(bb):
        for j in range(per):
            _store_values_ext(vt_ref, (i, j), vt, (i * per + j) * tile, tile, A_HEADS)


def _shared_kv(x, kv_g, w_k, w_v, k_g, *, rows, attn_tile):
    b, t, d = x.shape
    bb, tt = _row_blocks(b, t, rows)
    assert tt % attn_tile == 0
    per = tt // attn_tile
    d_ext = d + A_HEADS * ONES_ROWS
    tok = lambda i, j: (i, j, 0)
    blk = pl.BlockSpec((bb, tt, d), tok)
    return pl.pallas_call(
        _kv_kernel,
        grid=(b // bb, t // tt),
        in_specs=[blk, _resident((1, d)), _resident((d, d)), _resident((d, d)), _resident((d, d)),
                  _resident((k_g.shape[-1], 1))],
        out_specs=[blk, blk, blk, pl.BlockSpec((bb, per, d_ext, attn_tile), lambda i, j: (i, j, 0, 0))],
        out_shape=[jax.ShapeDtypeStruct((b, t, d), F32), jax.ShapeDtypeStruct((b, t, d), F32),
                   jax.ShapeDtypeStruct((b, t, d), BF16),
                   jax.ShapeDtypeStruct((b, t // attn_tile, d_ext, attn_tile), BF16)],
        compiler_params=_params("parallel", "parallel"),
        name="shared_kv",
    )(x, kv_g.reshape(1, d), w_k.T, w_v, w_v.T, k_g.reshape(-1, 1))


def _mixer_ffn_kernel(x_ref, a_ref, wp_ref, gate1_ref, g_ref, sh_ref, sc_ref, gate2_ref, wg_ref, wu_ref, wd_ref,
                      o_ref, acc_ref):
    bb, tt, d = x_ref.shape
    f = wg_ref.shape[1]
    a = a_ref[...].reshape(bb * tt, a_ref.shape[-1])
    o_ref[...] = x_ref[...] + gate1_ref[...] * _dot(a, wp_ref[...]).reshape(bb, tt, d)
    xm = _modulated(o_ref, g_ref, sh_ref, sc_ref)
    for c in range(f // FFN_CHUNK):
        sl = slice(c * FFN_CHUNK, (c + 1) * FFN_CHUNK)
        hg = _dot(xm, wg_ref[:, sl])
        hu = _dot(xm, wu_ref[:, sl])
        act = (hg * jax.nn.sigmoid(hg) * hu).astype(BF16)
        part = _dot(act, wd_ref[sl, :])
        if c == 0:
            acc_ref[...] = part
        else:
            acc_ref[...] += part
    o_ref[...] += gate2_ref[...] * acc_ref[...].reshape(bb, tt, d)


def _mixer_ffn(x, a, w_proj, gate1, gain, shift, scale, gate2, w_gate, w_up, w_down, *, rows):
    b, t, d = x.shape
    k = a.shape[-1]
    f = w_gate.shape[1]
    assert f % FFN_CHUNK == 0
    bb, tt = _row_blocks(b, t, rows)
    tok = lambda i, j: (i, j, 0)
    per_b = pl.BlockSpec((bb, 1, d), lambda i, j: (i, 0, 0))
    return pl.pallas_call(
        _mixer_ffn_kernel,
        grid=(b // bb, t // tt),
        in_specs=[pl.BlockSpec((bb, tt, d), tok), pl.BlockSpec((bb, tt, k), tok), _resident((k, d)), per_b,
                  _resident((1, d)), per_b, per_b, per_b,
                  _resident((d, f)), _resident((d, f)), _resident((f, d))],
        out_specs=pl.BlockSpec((bb, tt, d), tok),
        out_shape=jax.ShapeDtypeStruct((b, t, d), F32),
        scratch_shapes=[pltpu.VMEM((bb * tt, d), F32)],
        compiler_params=_params("parallel", "parallel"),
        name="mixer_ffn",
    )(x, a, w_proj, gate1, gain.reshape(1, d), shift, scale, gate2, w_gate, w_up, w_down)


def _mlstm_kernel(q_ref, k_ref, o_ref, vt_ref, g_ref, gt_ref, brow_ref, bcol_ref, hg_ref,
                  c0_ref, n0_ref, m0_ref, h_ref, c_out, n_out, m_out,
                  ct_s, m_s, mnew_s, acol_s, br_s, inter_s, wn_s, wold_s, qk_s, qc_s, sb_s, sv_s, stat_s):
    ci = pl.program_id(1)
    length = q_ref.shape[1]
    ext = ct_s.shape[1]
    dv = ext - ONES_ROWS
    dk = q_ref.shape[2] // M_HEADS
    heads = range(M_HEADS)

    @pl.when(ci == 0)
    def _():
        ct_s[...] = jnp.zeros(ct_s.shape, F32)
        for h in heads:
            c0 = jnp.concatenate([c0_ref[0, h], jnp.zeros((dv - dk, dv), F32)], axis=0)
            ct_s[h, :dv, :] = c0.T
            ct_s[h, dv:, :dk] = jnp.broadcast_to(n0_ref[0, h:h + 1, :], (ONES_ROWS, dk))
        m_s[...] = m0_ref[0]

    row = lax.broadcasted_iota(jnp.int32, (length, length), 0)
    col = lax.broadcasted_iota(jnp.int32, (length, length), 1)
    causal = row <= col
    hi = lax.Precision.HIGHEST
    g = g_ref[0] + brow_ref[...]
    b_cols = jnp.dot((col <= row).astype(F32), _log_sigmoid(g[:, GATE_PAD:]), precision=hi,
                     preferred_element_type=F32)
    acol_s[...] = g[:, :GATE_PAD] - b_cols
    gt = gt_ref[0] + bcol_ref[...]
    ig_rows = gt[:M_HEADS]
    br_s[...] = jnp.dot(_log_sigmoid(gt[M_HEADS:]), causal.astype(F32), precision=hi,
                        preferred_element_type=F32)
    b_rows = br_s[...]
    b_end = br_s[:, length - 1:length]
    m_prev = m_s[...]
    g_rows = b_end - b_rows + ig_rows
    m_new = jnp.maximum(b_end + m_prev, jnp.max(g_rows, axis=-1, keepdims=True))
    mnew_s[...] = m_new
    wold_s[...] = jnp.exp(b_end + m_prev - m_new)
    wn_s[...] = jnp.exp(g_rows - m_new)
    inter_s[...] = b_rows + m_prev

    def q_of(h):
        return q_ref[0, :, h * dk:(h + 1) * dk]

    def k_of(h):
        return k_ref[0, :, h * dk:(h + 1) * dk] * (dk ** -0.5)

    def vt_of(h):
        return vt_ref[0, h * ext:(h + 1) * ext, :]

    for h in heads:
        qk_s[h] = _dot_nt(k_of(h), q_of(h))
        qc_s[h] = _dot_nt(ct_s[h, :, :dk].astype(BF16), q_of(h))
    for h in heads:
        inter = inter_s[h:h + 1, :]
        dmat = jnp.where(causal, acol_s[:, h:h + 1] + br_s[h:h + 1, :], -jnp.inf)
        m_t = jnp.maximum(inter, jnp.max(dmat, axis=0, keepdims=True))
        sb_s[h] = (qk_s[h] * jnp.exp(dmat - m_t)).astype(BF16)
        stat_s[0, h] = jnp.exp(inter - m_t)
        stat_s[1, h] = jnp.exp(-m_t)
    for h in heads:
        sv_s[h] = _dot(vt_of(h), sb_s[h])
    for h in heads:
        w_inter = stat_s[0, h]
        tot = w_inter * qc_s[h] + sv_s[h]
        hh = tot[:dv] / jnp.maximum(jnp.abs(tot[dv:dv + 1]), stat_s[1, h])
        hh = hh * lax.rsqrt(jnp.mean(hh * hh, axis=0, keepdims=True) + EPS)
        og = o_ref[0, :, h * dv:(h + 1) * dv].astype(F32)
        h_ref[0, :, h * dv:(h + 1) * dv] = (_to_token_major(hh) * hg_ref[:, h * dv:(h + 1) * dv]
                                            * jax.nn.sigmoid(og)).astype(h_ref.dtype)
    for h in heads:
        vtw = (vt_of(h).astype(F32) * wn_s[h:h + 1, :]).astype(BF16)
        ct_s[h, :, :dk] = wold_s[h:h + 1, :] * ct_s[h, :, :dk] + _dot(vtw, k_of(h))
    m_s[...] = mnew_s[...]

    @pl.when(ci == pl.num_programs(1) - 1)
    def _():
        for h in heads:
            c_out[0, h] = ct_s[h, :dv, :].T[:dk]
            n_out[0, h:h + 1, :] = ct_s[h, dv:dv + 1, :dk]
        m_out[0] = m_s[...]


def _to_token_major(x):
    dv, length = x.shape
    if length % LANES:
        pad = LANES - length % LANES
        x = jnp.concatenate([x, jnp.zeros((dv, pad), x.dtype)], axis=1)
    return x.T[:length]


def _mlstm(qko, vt, g, gt, b_gates, head_g, c0, n0, m0, *, chunk):
    b, t, _ = qko.shape
    heads, dk, dv = c0.shape[1:]
    d = heads * dv
    qk = heads * dk
    ext = dv + ONES_ROWS
    assert qk * 2 == d and t % chunk == 0 and heads == M_HEADS and vt.shape[1] == heads * ext
    nc = t // chunk
    two_h = 2 * heads
    brow = (jnp.zeros((1, 2 * GATE_PAD), F32).at[0, :heads].set(b_gates[:heads])
            .at[0, GATE_PAD:GATE_PAD + heads].set(b_gates[heads:]))
    st4 = lambda i, c: (i, 0, 0, 0)
    st3 = lambda i, c: (i, 0, 0)
    outs = pl.pallas_call(
        _mlstm_kernel,
        grid=(b, nc),
        in_specs=[
            pl.BlockSpec((1, chunk, qk), lambda i, c: (i, c, 0)),
            pl.BlockSpec((1, chunk, qk), lambda i, c: (i, c, 1)),
            pl.BlockSpec((1, chunk, d), lambda i, c: (i, c, 1)),
            pl.BlockSpec((1, heads * ext, chunk), lambda i, c: (i, 0, c)),
            pl.BlockSpec((1, chunk, 2 * GATE_PAD), lambda i, c: (i, c, 0)),
            pl.BlockSpec((1, two_h, chunk), lambda i, c: (i, 0, c)),
            _resident((1, 2 * GATE_PAD)), _resident((two_h, 1)), _resident((1, d)),
            pl.BlockSpec((1, heads, dk, dv), st4),
            pl.BlockSpec((1, heads, dk), st3),
            pl.BlockSpec((1, heads, 1), st3),
        ],
        out_specs=[
            pl.BlockSpec((1, chunk, d), lambda i, c: (i, c, 0)),
            pl.BlockSpec((1, heads, dk, dv), st4),
            pl.BlockSpec((1, heads, dk), st3),
            pl.BlockSpec((1, heads, 1), st3),
        ],
        out_shape=[
            jax.ShapeDtypeStruct((b, t, d), BF16),
            jax.ShapeDtypeStruct((b, heads, dk, dv), F32),
            jax.ShapeDtypeStruct((b, heads, dk), F32),
            jax.ShapeDtypeStruct((b, heads, 1), F32),
        ],
        scratch_shapes=[pltpu.VMEM((heads, ext, dv), F32),
                        pltpu.VMEM((heads, 1), F32), pltpu.VMEM((heads, 1), F32),
                        pltpu.VMEM((chunk, GATE_PAD), F32),
                        pltpu.VMEM((heads, chunk), F32), pltpu.VMEM((heads, chunk), F32),
                        pltpu.VMEM((heads, chunk), F32), pltpu.VMEM((heads, 1), F32),
                        pltpu.VMEM((heads, chunk, chunk), F32),
                        pltpu.VMEM((heads, ext, chunk), F32),
                        pltpu.VMEM((heads, chunk, chunk), BF16),
                        pltpu.VMEM((heads, ext, chunk), F32),
                        pltpu.VMEM((2, heads, 1, chunk), F32)],
        compiler_params=_params("parallel", "arbitrary"),
        name="mlstm",
    )(qko, qko, qko, vt, g, gt, brow, b_gates.reshape(two_h, 1), head_g.reshape(1, d),
      c0, n0, m0.reshape(b, heads, 1))
    hn, c_new, n_new, m_new = outs
    return hn, c_new, n_new, m_new.reshape(b, heads)


def _rel_bucket(rel):
    nb = N_BUCKETS // 2
    max_exact = nb // 2
    ret = jnp.where(rel > 0, nb, 0)
    n = jnp.abs(rel)
    large = max_exact + (jnp.log(jnp.maximum(n, 1).astype(F32) / max_exact)
                         / math.log(MAX_DISTANCE / max_exact) * (nb - max_exact)).astype(jnp.int32)
    large = jnp.minimum(large, nb - 1)
    return ret + jnp.where(n < max_exact, n, large)


def _bias_tile(rel_bias, q0, k0, tq, tk):
    table = rel_bias.astype(F32)
    span = tq + tk
    r = jnp.arange(span)
    r = jnp.where(r < tk, r, r - span)
    vec = (table[_rel_bucket(r + (k0 - q0))] - table[N_BUCKETS // 2 - 1]).T
    bias = jnp.tile(vec, (1, tq))[:, :tq * (span - 1)].reshape(-1, tq, span - 1)[:, :, :tk]
    q_chunk = (q0 + lax.broadcasted_iota(jnp.int32, (tq, tk), 0)) // CHUNK
    k_chunk = (k0 + lax.broadcasted_iota(jnp.int32, (tq, tk), 1)) // CHUNK
    return jnp.where((k_chunk <= q_chunk)[None], bias * LOG2E, NEG_BIG)


def _attn_prompt_kernel(q_ref, k_ref, vt_ref, bias_ref, lam_ref, hg_ref, o_ref,
                        m_s, alpha_s, acc_s, s_scr, p_scr, *, lam_init):
    qi = pl.program_id(1)
    tile = q_ref.shape[2]
    ext = acc_s.shape[1]
    dv = o_ref.shape[2] // A_HEADS
    dh = dv // 2
    n_chain = 2 * A_HEADS
    m_s[...] = jnp.full(m_s.shape, NEG_BIG, F32)
    acc_s[...] = jnp.zeros(acc_s.shape, F32)

    def kv_tiles(j, slots, count):
        rows = count * tile
        koff = pl.multiple_of(j * tile, tile)
        for hc in range(n_chain):
            lo = hc * dh
            s_scr[hc, :rows] = _dot(k_ref[0, pl.ds(koff, rows), lo:lo + dh], q_ref[0, lo:lo + dh, :])
        for hc in range(n_chain):
            m_prev = m_s[hc]
            m_part = None
            for r in range(0, rows, ROW_CHUNK):
                s = s_scr[hc, r:r + ROW_CHUNK]
                if slots is not None:
                    t, rt = divmod(r, tile)
                    s = s + bias_ref[hc // 2, slots[t], rt:rt + ROW_CHUNK]
                    s_scr[hc, r:r + ROW_CHUNK] = s
                part = jnp.max(s.reshape(ROW_CHUNK // SUBLANES, SUBLANES, tile), axis=0)
                m_part = part if m_part is None else jnp.maximum(m_part, part)
            m_new = jnp.maximum(m_prev, jnp.max(m_part, axis=0, keepdims=True))
            alpha_s[hc] = jnp.exp2(m_prev - m_new)
            m_s[hc] = m_new
        for hc in range(n_chain):
            for r in range(0, rows, ROW_CHUNK):
                p_scr[hc, r:r + ROW_CHUNK] = jnp.exp2(s_scr[hc, r:r + ROW_CHUNK] - m_s[hc]).astype(BF16)
        for hc in range(n_chain):
            h = hc // 2
            pv = _dot(vt_ref[0, j, h * ext:(h + 1) * ext, :], p_scr[hc, :tile])
            for t in range(1, count):
                pv = pv + _dot(vt_ref[0, j + t, h * ext:(h + 1) * ext, :], p_scr[hc, t * tile:(t + 1) * tile])
            acc_s[hc] = alpha_s[hc] * acc_s[hc] + pv

    n_far = jnp.maximum(qi - 1, 0)

    def far_body(i, carry):
        kv_tiles(FAR_TILES * i, None, FAR_TILES)
        return carry

    lax.fori_loop(0, n_far // FAR_TILES, far_body, 0)

    def far_rest(j, carry):
        kv_tiles(j, None, 1)
        return carry

    lax.fori_loop(n_far - n_far % FAR_TILES, n_far, far_rest, 0)

    @pl.when(qi > 0)
    def _():
        kv_tiles(qi - 1, (0, 1), 2)

    @pl.when(qi == 0)
    def _():
        kv_tiles(qi, (1,), 1)


    lp = lam_ref[...]
    lam = (jnp.exp(jnp.sum(lp[0:1] * lp[1:2], axis=-1, keepdims=True))
           - jnp.exp(jnp.sum(lp[2:3] * lp[3:4], axis=-1, keepdims=True)) + lam_init)
    for h in range(A_HEADS):
        o = (acc_s[2 * h, :dv] / acc_s[2 * h, dv:dv + 1]
             - lam * (acc_s[2 * h + 1, :dv] / acc_s[2 * h + 1, dv:dv + 1]))
        o = o * lax.rsqrt(jnp.mean(o * o, axis=0, keepdims=True) + EPS) * hg_ref[...] * (1.0 - lam_init)
        o_ref[0, :, h * dv:(h + 1) * dv] = o.T.astype(o_ref.dtype)


def _attn_prompt(q, k, vt, rel_bias, lam_p, head_g, lam_init):
    b, d, t = q.shape
    tile = ATTN_TILE
    assert t % tile == 0 and tile % CHUNK == 0 and tile >= MAX_DISTANCE and vt.shape[-1] == tile
    n = t // tile
    dv = d // A_HEADS
    ext = vt.shape[2] // A_HEADS
    n_chain = 2 * A_HEADS
    bias = jnp.stack([_bias_tile(rel_bias, tile, 0, tile, tile), _bias_tile(rel_bias, 0, 0, tile, tile)],
                     axis=1).swapaxes(-1, -2)
    return pl.pallas_call(
        functools.partial(_attn_prompt_kernel, lam_init=lam_init),
        grid=(b, n),
        in_specs=[pl.BlockSpec((1, d, tile), lambda i, qi: (i, 0, qi)),
                  pl.BlockSpec((1, t, d), lambda i, qi: (i, 0, 0), pipeline_mode=pl.Buffered(1)),
                  pl.BlockSpec((1,) + vt.shape[1:], lambda i, qi: (i, 0, 0, 0), pipeline_mode=pl.Buffered(1)),
                  _resident(bias.shape), _resident(lam_p.shape), _resident((dv, 1))],
        out_specs=pl.BlockSpec((1, tile, d), lambda i, qi: (i, qi, 0)),
        out_shape=jax.ShapeDtypeStruct((b, t, d), BF16),
        scratch_shapes=[pltpu.VMEM((n_chain, 1, tile), F32), pltpu.VMEM((n_chain, 1, tile), F32),
                        pltpu.VMEM((n_chain, ext, tile), F32),
                        pltpu.VMEM((n_chain, FAR_TILES * tile, tile), F32),
                        pltpu.VMEM((n_chain, FAR_TILES * tile, tile), BF16)],
        compiler_params=_params("parallel", "arbitrary"),
        name="attn_prompt",
    )(q, k, vt, bias, lam_p, head_g.reshape(dv, 1))


def _attn_sample_kernel(q_ref, kc_ref, vtc_ref, kn_ref, vtn_ref, bias_c_ref, bias_n_ref, lam_ref, hg_ref,
                        o_ref, qbd_s, m_s, alpha_s, acc_s, s_scr, p_scr, *, lam_init):
    tq = q_ref.shape[2]
    tile = vtc_ref.shape[3]
    n_cache = vtc_ref.shape[1]
    ext = acc_s.shape[1]
    dv = o_ref.shape[2] // A_HEADS
    gw = qbd_s.shape[1]
    n_group = qbd_s.shape[0]
    n_sub = gw // tq
    row_blk = lax.broadcasted_iota(jnp.int32, (gw, gw), 0) // (gw // n_sub)
    col_blk = lax.broadcasted_iota(jnp.int32, (gw, gw), 1) // tq
    for g in range(n_group):
        qg = q_ref[0, g * gw:(g + 1) * gw, :]
        qrep = jnp.concatenate([qg] * n_sub, axis=1)
        qbd_s[g] = jnp.where(row_blk == col_blk, qrep, jnp.zeros_like(qrep))
    m_s[...] = jnp.full(m_s.shape, NEG_BIG, F32)
    acc_s[...] = jnp.zeros(acc_s.shape, F32)

    def kv_tile(k_tile, vt_tile, bias_ref, rows):
        for g in range(n_group):
            s_scr[g, :rows] = _dot(k_tile(g), qbd_s[g])
        for g in range(n_group):
            s = s_scr[g, :rows]
            if bias_ref is not None:
                s = s + bias_ref[g]
                s_scr[g, :rows] = s
            m_prev = m_s[g]
            m_new = jnp.maximum(m_prev, jnp.max(s, axis=0, keepdims=True))
            alpha_s[g] = jnp.exp2(m_prev - m_new)
            m_s[g] = m_new
        for g in range(n_group):
            for r in range(0, rows, ROW_CHUNK):
                p_scr[g, r:r + ROW_CHUNK] = jnp.exp2(s_scr[g, r:r + ROW_CHUNK] - m_s[g]).astype(BF16)
        for h in range(A_HEADS):
            g, half = divmod(h, 2)
            lanes = slice(half * 2 * tq, (half + 1) * 2 * tq)
            acc_s[h] = alpha_s[g, :, lanes] * acc_s[h] + _dot(vt_tile(h), p_scr[g, :rows, lanes])

    ones = jnp.ones((ext - dv, tile), BF16)

    def cache_tile(j, bias_ref):
        koff = pl.multiple_of(j * tile, tile)
        kv_tile(lambda g: kc_ref[0, pl.ds(koff, tile), g * gw:(g + 1) * gw],
                lambda h: jnp.concatenate([vtc_ref[0, j, h * dv:(h + 1) * dv, :], ones], axis=0),
                bias_ref, tile)

    def far_body(j, carry):
        cache_tile(j, None)
        return carry

    lax.fori_loop(0, n_cache - 1, far_body, 0)
    cache_tile(n_cache - 1, bias_c_ref)
    kv_tile(lambda g: kn_ref[0, :, g * gw:(g + 1) * gw],
            lambda h: vtn_ref[0, 0, h * ext:(h + 1) * ext, :], bias_n_ref, tq)

    lp = lam_ref[...]
    lam = (jnp.exp(jnp.sum(lp[0:1] * lp[1:2], axis=-1, keepdims=True))
           - jnp.exp(jnp.sum(lp[2:3] * lp[3:4], axis=-1, keepdims=True)) + lam_init)
    for h in range(A_HEADS):
        both = (acc_s[h, :dv] / acc_s[h, dv:dv + 1]).T
        o = both[:tq] - lam * both[tq:]
        o = _rms(o, hg_ref[...]) * (1.0 - lam_init)
        o_ref[0, :, h * dv:(h + 1) * dv] = o.astype(o_ref.dtype)


def _attn_sample(q, cache_k, cache_vt, new_k, new_vt, rel_bias, lam_p, head_g, lam_init):
    b, d, t = q.shape
    past = cache_k.shape[1]
    n_cache, _, tile = cache_vt.shape[1:]
    d_ext = new_vt.shape[2]
    assert past == n_cache * tile and past % CHUNK == 0 and tile >= MAX_DISTANCE
    dv = d // A_HEADS
    ext = d_ext // A_HEADS
    gw = 4 * t
    assert gw == 2 * dv and A_HEADS % 2 == 0
    n_group = A_HEADS // 2

    def packed(bias):
        bt = bias.swapaxes(-1, -2).reshape(n_group, 2, -1, t)
        return jnp.concatenate([bt[:, 0], bt[:, 0], bt[:, 1], bt[:, 1]], axis=-1)

    bias_c = packed(_bias_tile(rel_bias, past, past - tile, t, tile))
    bias_n = packed(_bias_tile(rel_bias, past, past, t, t))
    tok_spec = pl.BlockSpec((1, t, d), lambda i: (i, 0, 0))
    return pl.pallas_call(
        functools.partial(_attn_sample_kernel, lam_init=lam_init),
        grid=(b,),
        in_specs=[pl.BlockSpec((1, d, t), lambda i: (i, 0, 0)),
                  pl.BlockSpec((1, past, d), lambda i: (i, 0, 0)),
                  pl.BlockSpec((1, n_cache, d, tile), lambda i: (i, 0, 0, 0)),
                  tok_spec,
                  pl.BlockSpec((1, 1, d_ext, t), lambda i: (i, 0, 0, 0)),
                  _resident(bias_c.shape), _resident(bias_n.shape), _resident(lam_p.shape),
                  _resident((1, dv))],
        out_specs=tok_spec,
        out_shape=jax.ShapeDtypeStruct((b, t, d), BF16),
        scratch_shapes=[pltpu.VMEM((n_group, gw, gw), BF16),
                        pltpu.VMEM((n_group, 1, gw), F32), pltpu.VMEM((n_group, 1, gw), F32),
                        pltpu.VMEM((A_HEADS, ext, 2 * t), F32),
                        pltpu.VMEM((n_group, tile, gw), F32), pltpu.VMEM((n_group, tile, gw), BF16)],
        compiler_params=_params("parallel"),
        name="attn_sample",
    )(q, cache_k, cache_vt, new_k, new_vt, bias_c, bias_n, lam_p, head_g.reshape(1, dv))


def _cache_layouts(cache_k, cache_v, tile):
    b, past, heads, two, dh = cache_k.shape
    dv = two * dh
    d = heads * dv
    n = past // tile
    per = math.gcd(n, 4)
    span = per * tile
    k = pl.pallas_call(
        _keys_to_token_major_kernel,
        grid=(b, past // span),
        in_specs=[pl.BlockSpec((1, d, span), lambda i, j: (i, 0, j))],
        out_specs=pl.BlockSpec((1, span, d), lambda i, j: (i, j, 0)),
        out_shape=jax.ShapeDtypeStruct((b, past, d), BF16),
        compiler_params=_params("parallel", "parallel"),
        name="cache_keys_t",
    )(cache_k.transpose(0, 2, 3, 4, 1).reshape(b, d, past))
    vt = pl.pallas_call(
        functools.partial(_values_to_feature_major_kernel, heads=heads),
        grid=(b, n // per),
        in_specs=[pl.BlockSpec((1, span * heads, dv), lambda i, j: (i, j, 0))],
        out_specs=pl.BlockSpec((1, per, d, tile), lambda i, j: (i, j, 0, 0)),
        out_shape=jax.ShapeDtypeStruct((b, n, d, tile), BF16),
        compiler_params=_params("parallel", "parallel"),
        name="cache_values_t",
    )(cache_v.reshape(b, past * heads, dv))
    return k, vt


def _keys_to_token_major_kernel(k_ref, o_ref):
    for c in range(k_ref.shape[1] // LANES):
        o_ref[0, :, c * LANES:(c + 1) * LANES] = k_ref[0, c * LANES:(c + 1) * LANES, :].T.astype(BF16)


def _values_to_feature_major_kernel(v_ref, o_ref, *, heads):
    per, tile = o_ref.shape[1], o_ref.shape[3]
    dv = v_ref.shape[2]
    for j in range(per):
        for h in range(heads):
            rows = v_ref[0, pl.ds(j * tile * heads + h, tile, stride=heads), :]
            o_ref[0, j, h * dv:(h + 1) * dv, :] = rows.T.astype(BF16)


def _trunk(x, mod, mstate, past, chunk, rows, p):
    b, t, d = x.shape
    depth = p["norm_g"].shape[0]
    n_a = p["mlstm_w_in"].shape[0]
    qk = M_HEADS * (d // M_HEADS // 2)
    new_c, new_n, new_m = [], [], []
    prompt = past is None
    k_f32 = v_f32 = k_bf = vt_bf = None
    for l in range(depth):
        sh1, sc1, g1, sh2, sc2, g2 = [mod[l, :, i * d:(i + 1) * d].reshape(b, 1, d) for i in range(6)]
        if l < n_a:
            w_in = p["mlstm_w_in"][l]
            w_qko = jnp.concatenate([w_in[:, :2 * qk], w_in[:, 2 * qk + d:2 * qk + 2 * d]], axis=1)
            qko, vt, g, gt = _mlstm_proj(x, p["norm_g"][l, 0], sh1, sc1, w_qko, w_in[:, 2 * qk:2 * qk + d],
                                         w_in[:, 2 * qk + 2 * d:], rows=rows)
            hn, c_l, n_l, m_l = _mlstm(qko, vt, g, gt, p["mlstm_b_gates"][l], p["mlstm_head_g"][l],
                                       mstate[0][l], mstate[1][l], mstate[2][l], chunk=chunk)
            new_c.append(c_l)
            new_n.append(n_l)
            new_m.append(m_l)
            mixed, w_proj = hn, p["mlstm_w_out"][l]
        else:
            j = l - n_a
            lam_init = 0.8 - 0.6 * math.exp(-0.3 * l)
            dh = p["q_norm_g"].shape[-1]
            q = _q_proj(x, p["norm_g"][l, 0], sh1, sc1, p["attn_w_q"][j], p["q_norm_g"][j],
                        dh ** -0.5 * LOG2E, rows=rows)
            if prompt:
                o = _attn_prompt(q, k_bf, vt_bf, p["rel_bias"], p["attn_lambda"][j], p["attn_head_g"][j],
                                 lam_init)
            else:
                o = _attn_sample(q, past[0], past[1], k_bf, vt_bf, p["rel_bias"], p["attn_lambda"][j],
                                 p["attn_head_g"][j], lam_init)
            mixed, w_proj = o, p["attn_w_o"][j]
        x = _mixer_ffn(x, mixed, w_proj, g1, p["norm_g"][l, 1], sh2, sc2, g2, p["ffn_w_gate"][l],
                       p["ffn_w_up"][l], p["ffn_w_down"][l], rows=rows)
        if l == n_a - 1:
            k_f32, v_f32, k_bf, vt_bf = _shared_kv(x, p["kv_norm_g"], p["w_k"], p["w_v"], p["k_norm_g"],
                                                   rows=rows, attn_tile=ATTN_TILE if prompt else t)
    dh = p["k_norm_g"].shape[-1]
    return (x, jnp.stack(new_c), jnp.stack(new_n), jnp.stack(new_m),
            k_f32.reshape(b, t, A_HEADS, 2, dh), v_f32.reshape(b, t, A_HEADS, 2 * dh))


def kernel(x_prompt, x_sample, c_prompt, c_sample, state_mlstm_C, state_mlstm_n, state_mlstm_m, cache_k, cache_v, ada_w, ada_b, norm_g, mlstm_w_in, mlstm_b_gates, mlstm_head_g, mlstm_w_out, kv_norm_g, w_k, w_v, k_norm_g, attn_w_q, q_norm_g, attn_lambda, attn_head_g, attn_w_o, rel_bias, ffn_w_gate, ffn_w_up, ffn_w_down):
    bf = lambda w: w.astype(BF16)
    p = dict(norm_g=norm_g, mlstm_w_in=bf(mlstm_w_in), mlstm_b_gates=mlstm_b_gates,
             mlstm_head_g=mlstm_head_g, mlstm_w_out=bf(mlstm_w_out), kv_norm_g=kv_norm_g, w_k=bf(w_k),
             w_v=bf(w_v), k_norm_g=k_norm_g, attn_w_q=bf(attn_w_q), q_norm_g=q_norm_g,
             attn_lambda=attn_lambda, attn_head_g=attn_head_g, attn_w_o=bf(attn_w_o), rel_bias=rel_bias,
             ffn_w_gate=bf(ffn_w_gate), ffn_w_up=bf(ffn_w_up), ffn_w_down=bf(ffn_w_down))
    bp = x_prompt.shape[0]
    n_a, _, heads, dk, dv = state_mlstm_C.shape
    mod = _adaln(jnp.concatenate([c_prompt, c_sample], axis=0), ada_w, ada_b)
    zero_state = (jnp.zeros((n_a, bp, heads, dk, dv), F32), jnp.zeros((n_a, bp, heads, dk), F32),
                  jnp.zeros((n_a, bp, heads), F32))
    y_p, c_p, n_p, m_p, k_p, v_p = _trunk(x_prompt, mod[:, :bp], zero_state, None, MLSTM_PROMPT_CHUNK, 512, p)
    y_s, c_s, n_s, m_s, k_s, v_s = _trunk(x_sample, mod[:, bp:], (state_mlstm_C, state_mlstm_n, state_mlstm_m),
                                          _cache_layouts(cache_k, cache_v, CACHE_TILE), x_sample.shape[1], 512, p)
    return (y_p, y_s, c_p, n_p, m_p, k_p, v_p, c_s, n_s, m_s, k_s, v_s)
```

```python
import functools
import math

import jax
import jax.numpy as jnp
from jax import lax
from jax.experimental import pallas as pl
from jax.experimental.pallas import tpu as pltpu

F32 = jnp.float32
BF16 = jnp.bfloat16

EPS = 1e-6
NEG_BIG = -1e30
CHUNK = 64
N_BUCKETS = 32
MAX_DISTANCE = 128
M_HEADS = 8
A_HEADS = 8
GATE_PAD = 128
LANES = 128
SUBLANES = 8
ONES_ROWS = 16
LOG2E = math.log2(math.e)

V7X_VMEM_BYTES = 64 * 1024 * 1024
VMEM_LIMIT = V7X_VMEM_BYTES - 8 * 1024 * 1024

ATTN_TILE = 256
FAR_TILES = 2
CACHE_TILE = 512
MLSTM_PROMPT_CHUNK = 256
FFN_CHUNK = 256
ROW_CHUNK = 64


def _params(*sem, flags=None):
    return pltpu.CompilerParams(dimension_semantics=sem, vmem_limit_bytes=VMEM_LIMIT, flags=flags)


def _rms(x, g):
    return x * lax.rsqrt(jnp.mean(x * x, axis=-1, keepdims=True) + EPS) * g


def _log_sigmoid(x):
    return jnp.minimum(x, 0.0) - jnp.log1p(jnp.exp(-jnp.abs(x)))


def _dot(a, b):
    return jnp.dot(a, b, preferred_element_type=F32)


def _dot_nt(a, b):
    return lax.dot_general(a, b, (((1,), (1,)), ((), ())), preferred_element_type=F32)


def _dot_tn(a, b):
    return lax.dot_general(a, b, (((0,), (0,)), ((), ())), preferred_element_type=F32)


def _resident(shape):
    nd = len(shape)
    return pl.BlockSpec(shape, lambda *_: (0,) * nd, pipeline_mode=pl.Buffered(1))


def _adaln_kernel(c_ref, w_ref, b_ref, o_ref):
    c = c_ref[...]
    a = (c * jax.nn.sigmoid(c)).astype(BF16)
    o_ref[0] = _dot(a, w_ref[0].astype(BF16)) + b_ref[0]


def _adaln(c, ada_w, ada_b):
    depth, d, n = ada_w.shape
    rows = c.shape[0]
    tn = 1536
    assert n % tn == 0
    return pl.pallas_call(
        _adaln_kernel,
        grid=(depth, n // tn),
        in_specs=[
            pl.BlockSpec((rows, d), lambda l, j: (0, 0)),
            pl.BlockSpec((1, d, tn), lambda l, j: (l, 0, j)),
            pl.BlockSpec((1, 1, tn), lambda l, j: (l, 0, j)),
        ],
        out_specs=pl.BlockSpec((1, rows, tn), lambda l, j: (l, 0, j)),
        out_shape=jax.ShapeDtypeStruct((depth, rows, n), F32),
        compiler_params=_params("parallel", "parallel"),
        name="adaln",
    )(c, ada_w, ada_b.reshape(depth, 1, n))


def _modulated(x_ref, g_ref, sh_ref, sc_ref):
    x = x_ref[...]
    bb, tt, d = x.shape
    xm = _rms(x, g_ref[...]) * (1.0 + sc_ref[...]) + sh_ref[...]
    return xm.reshape(bb * tt, d).astype(BF16)


def _group_rms_feature_major(yt, gain_col):
    width = gain_col.shape[0]
    groups = []
    for g in range(yt.shape[0] // width):
        seg = yt[g * width:(g + 1) * width]
        groups.append(seg * lax.rsqrt(jnp.mean(seg * seg, axis=0, keepdims=True) + EPS) * gain_col)
    return groups


def _store_values_ext(dst_ref, lead, vt, col, width, heads):
    dv = vt.shape[0] // heads
    ext = dv + ONES_ROWS
    ones = jnp.ones((ONES_ROWS, width), BF16)
    for hh in range(heads):
        dst_ref[lead + (slice(hh * ext, hh * ext + dv), slice(None))] = vt[hh * dv:(hh + 1) * dv, col:col + width]
        dst_ref[lead + (slice(hh * ext + dv, (hh + 1) * ext), slice(None))] = ones


def _mlstm_proj_kernel(x_ref, g_ref, sh_ref, sc_ref, w_ref, wf_ref, wg_ref, wgt_ref,
                       o_ref, of_ref, og_ref, ogt_ref, *, n_chunk):
    bb, tt, n = o_ref.shape
    xm = _modulated(x_ref, g_ref, sh_ref, sc_ref)
    for c in range(n // n_chunk):
        y = _dot(xm, w_ref[:, c * n_chunk:(c + 1) * n_chunk])
        o_ref[:, :, c * n_chunk:(c + 1) * n_chunk] = y.reshape(bb, tt, n_chunk).astype(o_ref.dtype)
    vt = _dot_nt(wf_ref[...], xm).astype(BF16)
    for i in range(bb):
        _store_values_ext(of_ref, (i,), vt, i * tt, tt, M_HEADS)
        ogt_ref[i] = _dot_nt(wgt_ref[...], xm[i * tt:(i + 1) * tt])
    og_ref[...] = _dot(xm, wg_ref[...]).reshape(bb, tt, 2 * GATE_PAD)


def _q_proj_kernel(x_ref, g_ref, sh_ref, sc_ref, wt_ref, qg_ref, o_ref, *, qk_scale):
    bb, n, tt = o_ref.shape
    xm = _modulated(x_ref, g_ref, sh_ref, sc_ref)
    groups = _group_rms_feature_major(_dot_nt(wt_ref[...], xm), qg_ref[...])
    width = qg_ref.shape[0]
    for g, seg in enumerate(groups):
        seg = (seg * qk_scale).astype(o_ref.dtype)
        for i in range(bb):
            o_ref[i, g * width:(g + 1) * width, :] = seg[:, i * tt:(i + 1) * tt]


def _row_blocks(b, t, rows):
    if t >= rows:
        assert t % rows == 0
        return 1, rows
    assert rows % t == 0 and b % (rows // t) == 0
    return rows // t, t


def _token_specs(x, rows):
    b, t, d = x.shape
    bb, tt = _row_blocks(b, t, rows)
    per_b = pl.BlockSpec((bb, 1, d), lambda i, j: (i, 0, 0))
    return bb, tt, (b // bb, t // tt), pl.BlockSpec((bb, tt, d), lambda i, j: (i, j, 0)), per_b


def _mlstm_proj(x, gain, shift, scale, w, w_feat, w_gate, *, rows):
    b, t, d = x.shape
    n = w.shape[1]
    bb, tt, grid, x_spec, per_b = _token_specs(x, rows)
    tok = lambda i, j: (i, j, 0)
    feat = lambda i, j: (i, 0, j)
    ng = w_gate.shape[1]
    half = ng // 2
    wg = (jnp.zeros((d, 2 * GATE_PAD), BF16).at[:, :half].set(w_gate[:, :half])
          .at[:, GATE_PAD:GATE_PAD + half].set(w_gate[:, half:]))
    n_feat = w_feat.shape[1]
    d_ext = n_feat + M_HEADS * ONES_ROWS
    return pl.pallas_call(
        functools.partial(_mlstm_proj_kernel, n_chunk=min(n, 1024)),
        grid=grid,
        in_specs=[x_spec, _resident((1, d)), per_b, per_b, _resident((d, n)),
                  _resident((n_feat, d)), _resident((d, 2 * GATE_PAD)), _resident((ng, d))],
        out_specs=[pl.BlockSpec((bb, tt, n), tok), pl.BlockSpec((bb, d_ext, tt), feat),
                   pl.BlockSpec((bb, tt, 2 * GATE_PAD), tok), pl.BlockSpec((bb, ng, tt), feat)],
        out_shape=[jax.ShapeDtypeStruct((b, t, n), BF16), jax.ShapeDtypeStruct((b, d_ext, t), BF16),
                   jax.ShapeDtypeStruct((b, t, 2 * GATE_PAD), F32), jax.ShapeDtypeStruct((b, ng, t), F32)],
        compiler_params=_params("parallel", "parallel"), name="mlstm_proj",
    )(x, gain.reshape(1, d), shift, scale, w, w_feat.T, wg, w_gate.T)


def _q_proj(x, gain, shift, scale, w, qk_gain, qk_scale, *, rows):
    b, t, d = x.shape
    n = w.shape[1]
    bb, tt, grid, x_spec, per_b = _token_specs(x, rows)
    return pl.pallas_call(
        functools.partial(_q_proj_kernel, qk_scale=qk_scale),
        grid=grid,
        in_specs=[x_spec, _resident((1, d)), per_b, per_b, _resident((n, d)), _resident((qk_gain.shape[-1], 1))],
        out_specs=pl.BlockSpec((bb, n, tt), lambda i, j: (i, 0, j)),
        out_shape=jax.ShapeDtypeStruct((b, n, t), BF16),
        compiler_params=_params("parallel", "parallel"), name="q_proj",
    )(x, gain.reshape(1, d), shift, scale, w.T, qk_gain.reshape(-1, 1))


def _kv_kernel(x_ref, g_ref, wkt_ref, wv_ref, wvt_ref, kg_ref, k_ref, v_ref, kb_ref, vt_ref):
    x = x_ref[...]
    bb, tt, d = x.shape
    h = _rms(x, g_ref[...]).reshape(bb * tt, d).astype(BF16)
    groups = _group_rms_feature_major(_dot_nt(wkt_ref[...], h), kg_ref[...])
    per_tile = LANES // kg_ref.shape[0]
    for c in range(d // LANES):
        k_tok = jnp.concatenate(groups[c * per_tile:(c + 1) * per_tile], axis=0).T.reshape(bb, tt, LANES)
        k_ref[:, :, c * LANES:(c + 1) * LANES] = k_tok
        kb_ref[:, :, c * LANES:(c + 1) * LANES] = k_tok.astype(BF16)
    v_ref[...] = _dot(h, wv_ref[...]).reshape(bb, tt, d)
    vt = _dot_nt(wvt_ref[...], h).astype(BF16)
    per, tile = vt_ref.shape[1], vt_ref.shape[3]
    for i in range(bb):
        for j in range(per):
            _store_values_ext(vt_ref, (i, j), vt, (i * per + j) * tile, tile, A_HEADS)


def _shared_kv(x, kv_g, w_k, w_v, k_g, *, rows, attn_tile):
    b, t, d = x.shape
    bb, tt = _row_blocks(b, t, rows)
    assert tt % attn_tile == 0
    per = tt // attn_tile
    d_ext = d + A_HEADS * ONES_ROWS
    tok = lambda i, j: (i, j, 0)
    blk = pl.BlockSpec((bb, tt, d), tok)
    return pl.pallas_call(
        _kv_kernel,
        grid=(b // bb, t // tt),
        in_specs=[blk, _resident((1, d)), _resident((d, d)), _resident((d, d)), _resident((d, d)),
                  _resident((k_g.shape[-1], 1))],
        out_specs=[blk, blk, blk, pl.BlockSpec((bb, per, d_ext, attn_tile), lambda i, j: (i, j, 0, 0))],
        out_shape=[jax.ShapeDtypeStruct((b, t, d), F32), jax.ShapeDtypeStruct((b, t, d), F32),
                   jax.ShapeDtypeStruct((b, t, d), BF16),
                   jax.ShapeDtypeStruct((b, t // attn_tile, d_ext, attn_tile), BF16)],
        compiler_params=_params("parallel", "parallel"),
        name="shared_kv",
    )(x, kv_g.reshape(1, d), w_k.T, w_v, w_v.T, k_g.reshape(-1, 1))


def _mixer_ffn_kernel(x_ref, a_ref, wp_ref, gate1_ref, g_ref, sh_ref, sc_ref, gate2_ref, wg_ref, wu_ref, wd_ref,
                      o_ref, acc_ref):
    bb, tt, d = x_ref.shape
    f = wg_ref.shape[1]
    a = a_ref[...].reshape(bb * tt, a_ref.shape[-1])
    o_ref[...] = x_ref[...] + gate1_ref[...] * _dot(a, wp_ref[...]).reshape(bb, tt, d)
    xm = _modulated(o_ref, g_ref, sh_ref, sc_ref)
    for c in range(f // FFN_CHUNK):
        sl = slice(c * FFN_CHUNK, (c + 1) * FFN_CHUNK)
        hg = _dot(xm, wg_ref[:, sl])
        hu = _dot(xm, wu_ref[:, sl])
        act = (hg * jax.nn.sigmoid(hg) * hu).astype(BF16)
        part = _dot(act, wd_ref[sl, :])
        if c == 0:
            acc_ref[...] = part
        else:
            acc_ref[...] += part
    o_ref[...] += gate2_ref[...] * acc_ref[...].reshape(bb, tt, d)


def _mixer_ffn(x, a, w_proj, gate1, gain, shift, scale, gate2, w_gate, w_up, w_down, *, rows):
    b, t, d = x.shape
    k = a.shape[-1]
    f = w_gate.shape[1]
    assert f % FFN_CHUNK == 0
    bb, tt = _row_blocks(b, t, rows)
    tok = lambda i, j: (i, j, 0)
    per_b = pl.BlockSpec((bb, 1, d), lambda i, j: (i, 0, 0))
    return pl.pallas_call(
        _mixer_ffn_kernel,
        grid=(b // bb, t // tt),
        in_specs=[pl.BlockSpec((bb, tt, d), tok), pl.BlockSpec((bb, tt, k), tok), _resident((k, d)), per_b,
                  _resident((1, d)), per_b, per_b, per_b,
                  _resident((d, f)), _resident((d, f)), _resident((f, d))],
        out_specs=pl.BlockSpec((bb, tt, d), tok),
        out_shape=jax.ShapeDtypeStruct((b, t, d), F32),
        scratch_shapes=[pltpu.VMEM((bb * tt, d), F32)],
        compiler_params=_params("parallel", "parallel"),
        name="mixer_ffn",
    )(x, a, w_proj, gate1, gain.reshape(1, d), shift, scale, gate2, w_gate, w_up, w_down)


def _mlstm_kernel(q_ref, k_ref, o_ref, vt_ref, g_ref, gt_ref, brow_ref, bcol_ref, hg_ref,
                  c0_ref, n0_ref, m0_ref, h_ref, c_out, n_out, m_out,
                  ct_s, m_s, mnew_s, acol_s, br_s, inter_s, wn_s, wold_s, qk_s, qc_s, sb_s, sv_s, stat_s):
    ci = pl.program_id(1)
    length = q_ref.shape[1]
    ext = ct_s.shape[1]
    dv = ext - ONES_ROWS
    dk = q_ref.shape[2] // M_HEADS
    heads = range(M_HEADS)

    @pl.when(ci == 0)
    def _():
        ct_s[...] = jnp.zeros(ct_s.shape, F32)
        for h in heads:
            c0 = jnp.concatenate([c0_ref[0, h], jnp.zeros((dv - dk, dv), F32)], axis=0)
            ct_s[h, :dv, :] = c0.T
            ct_s[h, dv:, :dk] = jnp.broadcast_to(n0_ref[0, h:h + 1, :], (ONES_ROWS, dk))
        m_s[...] = m0_ref[0]

    row = lax.broadcasted_iota(jnp.int32, (length, length), 0)
    col = lax.broadcasted_iota(jnp.int32, (length, length), 1)
    causal = row <= col
    hi = lax.Precision.HIGHEST
    g = g_ref[0] + brow_ref[...]
    b_cols = jnp.dot((col <= row).astype(F32), _log_sigmoid(g[:, GATE_PAD:]), precision=hi,
                     preferred_element_type=F32)
    acol_s[...] = g[:, :GATE_PAD] - b_cols
    gt = gt_ref[0] + bcol_ref[...]
    ig_rows = gt[:M_HEADS]
    br_s[...] = jnp.dot(_log_sigmoid(gt[M_HEADS:]), causal.astype(F32), precision=hi,
                        preferred_element_type=F32)
    b_rows = br_s[...]
    b_end = br_s[:, length - 1:length]
    m_prev = m_s[...]
    g_rows = b_end - b_rows + ig_rows
    m_new = jnp.maximum(b_end + m_prev, jnp.max(g_rows, axis=-1, keepdims=True))
    mnew_s[...] = m_new
    wold_s[...] = jnp.exp(b_end + m_prev - m_new)
    wn_s[...] = jnp.exp(g_rows - m_new)
    inter_s[...] = b_rows + m_prev

    def q_of(h):
        return q_ref[0, :, h * dk:(h + 1) * dk]

    def k_of(h):
        return k_ref[0, :, h * dk:(h + 1) * dk] * (dk ** -0.5)

    def vt_of(h):
        return vt_ref[0, h * ext:(h + 1) * ext, :]

    for h in heads:
        qk_s[h] = _dot_nt(k_of(h), q_of(h))
        qc_s[h] = _dot_nt(ct_s[h, :, :dk].astype(BF16), q_of(h))
    for h in heads:
        inter = inter_s[h:h + 1, :]
        dmat = jnp.where(causal, acol_s[:, h:h + 1] + br_s[h:h + 1, :], -jnp.inf)
        m_t = jnp.maximum(inter, jnp.max(dmat, axis=0, keepdims=True))
        sb_s[h] = (qk_s[h] * jnp.exp(dmat - m_t)).astype(BF16)
        stat_s[0, h] = jnp.exp(inter - m_t)
        stat_s[1, h] = jnp.exp(-m_t)
    for h in heads:
        sv_s[h] = _dot(vt_of(h), sb_s[h])
    for h in heads:
        w_inter = stat_s[0, h]
        tot = w_inter * qc_s[h] + sv_s[h]
        hh = tot[:dv] / jnp.maximum(jnp.abs(tot[dv:dv + 1]), stat_s[1, h])
        hh = hh * lax.rsqrt(jnp.mean(hh * hh, axis=0, keepdims=True) + EPS)
        og = o_ref[0, :, h * dv:(h + 1) * dv].astype(F32)
        h_ref[0, :, h * dv:(h + 1) * dv] = (_to_token_major(hh) * hg_ref[:, h * dv:(h + 1) * dv]
                                            * jax.nn.sigmoid(og)).astype(h_ref.dtype)
    for h in heads:
        vtw = (vt_of(h).astype(F32) * wn_s[h:h + 1, :]).astype(BF16)
        ct_s[h, :, :dk] = wold_s[h:h + 1, :] * ct_s[h, :, :dk] + _dot(vtw, k_of(h))
    m_s[...] = mnew_s[...]

    @pl.when(ci == pl.num_programs(1) - 1)
    def _():
        for h in heads:
            c_out[0, h] = ct_s[h, :dv, :].T[:dk]
            n_out[0, h:h + 1, :] = ct_s[h, dv:dv + 1, :dk]
        m_out[0] = m_s[...]


def _to_token_major(x):
    dv, length = x.shape
    if length % LANES:
        pad = LANES - length % LANES
        x = jnp.concatenate([x, jnp.zeros((dv, pad), x.dtype)], axis=1)
    return x.T[:length]


def _mlstm(qko, vt, g, gt, b_gates, head_g, c0, n0, m0, *, chunk):
    b, t, _ = qko.shape
    heads, dk, dv = c0.shape[1:]
    d = heads * dv
    qk = heads * dk
    ext = dv + ONES_ROWS
    assert qk * 2 == d and t % chunk == 0 and heads == M_HEADS and vt.shape[1] == heads * ext
    nc = t // chunk
    two_h = 2 * heads
    brow = (jnp.zeros((1, 2 * GATE_PAD), F32).at[0, :heads].set(b_gates[:heads])
            .at[0, GATE_PAD:GATE_PAD + heads].set(b_gates[heads:]))
    st4 = lambda i, c: (i, 0, 0, 0)
    st3 = lambda i, c: (i, 0, 0)
    outs = pl.pallas_call(
        _mlstm_kernel,
        grid=(b, nc),
        in_specs=[
            pl.BlockSpec((1, chunk, qk), lambda i, c: (i, c, 0)),
            pl.BlockSpec((1, chunk, qk), lambda i, c: (i, c, 1)),
            pl.BlockSpec((1, chunk, d), lambda i, c: (i, c, 1)),
            pl.BlockSpec((1, heads * ext, chunk), lambda i, c: (i, 0, c)),
            pl.BlockSpec((1, chunk, 2 * GATE_PAD), lambda i, c: (i, c, 0)),
            pl.BlockSpec((1, two_h, chunk), lambda i, c: (i, 0, c)),
            _resident((1, 2 * GATE_PAD)), _resident((two_h, 1)), _resident((1, d)),
            pl.BlockSpec((1, heads, dk, dv), st4),
            pl.BlockSpec((1, heads, dk), st3),
            pl.BlockSpec((1, heads, 1), st3),
        ],
        out_specs=[
            pl.BlockSpec((1, chunk, d), lambda i, c: (i, c, 0)),
            pl.BlockSpec((1, heads, dk, dv), st4),
            pl.BlockSpec((1, heads, dk), st3),
            pl.BlockSpec((1, heads, 1), st3),
        ],
        out_shape=[
            jax.ShapeDtypeStruct((b, t, d), BF16),
            jax.ShapeDtypeStruct((b, heads, dk, dv), F32),
            jax.ShapeDtypeStruct((b, heads, dk), F32),
            jax.ShapeDtypeStruct((b, heads, 1), F32),
        ],
        scratch_shapes=[pltpu.VMEM((heads, ext, dv), F32),
                        pltpu.VMEM((heads, 1), F32), pltpu.VMEM((heads, 1), F32),
                        pltpu.VMEM((chunk, GATE_PAD), F32),
                        pltpu.VMEM((heads, chunk), F32), pltpu.VMEM((heads, chunk), F32),
                        pltpu.VMEM((heads, chunk), F32), pltpu.VMEM((heads, 1), F32),
                        pltpu.VMEM((heads, chunk, chunk), F32),
                        pltpu.VMEM((heads, ext, chunk), F32),
                        pltpu.VMEM((heads, chunk, chunk), BF16),
                        pltpu.VMEM((heads, ext, chunk), F32),
                        pltpu.VMEM((2, heads, 1, chunk), F32)],
        compiler_params=_params("parallel", "arbitrary"),
        name="mlstm",
    )(qko, qko, qko, vt, g, gt, brow, b_gates.reshape(two_h, 1), head_g.reshape(1, d),
      c0, n0, m0.reshape(b, heads, 1))
    hn, c_new, n_new, m_new = outs
    return hn, c_new, n_new, m_new.reshape(b, heads)


def _rel_bucket(rel):
    nb = N_BUCKETS // 2
    max_exact = nb // 2
    ret = jnp.where(rel > 0, nb, 0)
    n = jnp.abs(rel)
    large = max_exact + (jnp.log(jnp.maximum(n, 1).astype(F32) / max_exact)
                         / math.log(MAX_DISTANCE / max_exact) * (nb - max_exact)).astype(jnp.int32)
    large = jnp.minimum(large, nb - 1)
    return ret + jnp.where(n < max_exact, n, large)


def _bias_tile(rel_bias, q0, k0, tq, tk):
    table = rel_bias.astype(F32)
    span = tq + tk
    r = jnp.arange(span)
    r = jnp.where(r < tk, r, r - span)
    vec = (table[_rel_bucket(r + (k0 - q0))] - table[N_BUCKETS // 2 - 1]).T
    bias = jnp.tile(vec, (1, tq))[:, :tq * (span - 1)].reshape(-1, tq, span - 1)[:, :, :tk]
    q_chunk = (q0 + lax.broadcasted_iota(jnp.int32, (tq, tk), 0)) // CHUNK
    k_chunk = (k0 + lax.broadcasted_iota(jnp.int32, (tq, tk), 1)) // CHUNK
    return jnp.where((k_chunk <= q_chunk)[None], bias * LOG2E, NEG_BIG)


def _attn_prompt_kernel(q_ref, k_ref, vt_ref, bias_ref, lam_ref, hg_ref, o_ref,
                        m_s, alpha_s, acc_s, s_scr, p_scr, *, lam_init):
    qi = pl.program_id(1)
    tile = q_ref.shape[2]
    ext = acc_s.shape[1]
    dv = o_ref.shape[2] // A_HEADS
    dh = dv // 2
    n_chain = 2 * A_HEADS
    m_s[...] = jnp.full(m_s.shape, NEG_BIG, F32)
    acc_s[...] = jnp.zeros(acc_s.shape, F32)

    def kv_tiles(j, slots, count):
        rows = count * tile
        koff = pl.multiple_of(j * tile, tile)
        for hc in range(n_chain):
            lo = hc * dh
            s_scr[hc, :rows] = _dot(k_ref[0, pl.ds(koff, rows), lo:lo + dh], q_ref[0, lo:lo + dh, :])
        for hc in range(n_chain):
            m_prev = m_s[hc]
            m_part = None
            for r in range(0, rows, ROW_CHUNK):
                s = s_scr[hc, r:r + ROW_CHUNK]
                if slots is not None:
                    t, rt = divmod(r, tile)
                    s = s + bias_ref[hc // 2, slots[t], rt:rt + ROW_CHUNK]
                    s_scr[hc, r:r + ROW_CHUNK] = s
                part = jnp.max(s.reshape(ROW_CHUNK // SUBLANES, SUBLANES, tile), axis=0)
                m_part = part if m_part is None else jnp.maximum(m_part, part)
            m_new = jnp.maximum(m_prev, jnp.max(m_part, axis=0, keepdims=True))
            alpha_s[hc] = jnp.exp2(m_prev - m_new)
            m_s[hc] = m_new
        for hc in range(n_chain):
            for r in range(0, rows, ROW_CHUNK):
                p_scr[hc, r:r + ROW_CHUNK] = jnp.exp2(s_scr[hc, r:r + ROW_CHUNK] - m_s[hc]).astype(BF16)
        for hc in range(n_chain):
            h = hc // 2
            pv = _dot(vt_ref[0, j, h * ext:(h + 1) * ext, :], p_scr[hc, :tile])
            for t in range(1, count):
                pv = pv + _dot(vt_ref[0, j + t, h * ext:(h + 1) * ext, :], p_scr[hc, t * tile:(t + 1) * tile])
            acc_s[hc] = alpha_s[hc] * acc_s[hc] + pv

    n_far = jnp.maximum(qi - 1, 0)

    def far_body(i, carry):
        kv_tiles(FAR_TILES * i, None, FAR_TILES)
        return carry

    lax.fori_loop(0, n_far // FAR_TILES, far_body, 0)

    def far_rest(j, carry):
        kv_tiles(j, None, 1)
        return carry

    lax.fori_loop(n_far - n_far % FAR_TILES, n_far, far_rest, 0)

    @pl.when(qi > 0)
    def _():
        kv_tiles(qi - 1, (0, 1), 2)

    @pl.when(qi == 0)
    def _():
        kv_tiles(qi, (1,), 1)


    lp = lam_ref[...]
    lam = (jnp.exp(jnp.sum(lp[0:1] * lp[1:2], axis=-1, keepdims=True))
           - jnp.exp(jnp.sum(lp[2:3] * lp[3:4], axis=-1, keepdims=True)) + lam_init)
    for h in range(A_HEADS):
        o = (acc_s[2 * h, :dv] / acc_s[2 * h, dv:dv + 1]
             - lam * (acc_s[2 * h + 1, :dv] / acc_s[2 * h + 1, dv:dv + 1]))
        o = o * lax.rsqrt(jnp.mean(o * o, axis=0, keepdims=True) + EPS) * hg_ref[...] * (1.0 - lam_init)
        o_ref[0, :, h * dv:(h + 1) * dv] = o.T.astype(o_ref.dtype)


def _attn_prompt(q, k, vt, rel_bias, lam_p, head_g, lam_init):
    b, d, t = q.shape
    tile = ATTN_TILE
    assert t % tile == 0 and tile % CHUNK == 0 and tile >= MAX_DISTANCE and vt.shape[-1] == tile
    n = t // tile
    dv = d // A_HEADS
    ext = vt.shape[2] // A_HEADS
    n_chain = 2 * A_HEADS
    bias = jnp.stack([_bias_tile(rel_bias, tile, 0, tile, tile), _bias_tile(rel_bias, 0, 0, tile, tile)],
                     axis=1).swapaxes(-1, -2)
    return pl.pallas_call(
        functools.partial(_attn_prompt_kernel, lam_init=lam_init),
        grid=(b, n),
        in_specs=[pl.BlockSpec((1, d, tile), lambda i, qi: (i, 0, qi)),
                  pl.BlockSpec((1, t, d), lambda i, qi: (i, 0, 0), pipeline_mode=pl.Buffered(1)),
                  pl.BlockSpec((1,) + vt.shape[1:], lambda i, qi: (i, 0, 0, 0), pipeline_mode=pl.Buffered(1)),
                  _resident(bias.shape), _resident(lam_p.shape), _resident((dv, 1))],
        out_specs=pl.BlockSpec((1, tile, d), lambda i, qi: (i, qi, 0)),
        out_shape=jax.ShapeDtypeStruct((b, t, d), BF16),
        scratch_shapes=[pltpu.VMEM((n_chain, 1, tile), F32), pltpu.VMEM((n_chain, 1, tile), F32),
                        pltpu.VMEM((n_chain, ext, tile), F32),
                        pltpu.VMEM((n_chain, FAR_TILES * tile, tile), F32),
                        pltpu.VMEM((n_chain, FAR_TILES * tile, tile), BF16)],
        compiler_params=_params("parallel", "arbitrary"),
        name="attn_prompt",
    )(q, k, vt, bias, lam_p, head_g.reshape(dv, 1))


def _attn_sample_kernel(q_ref, kc_ref, vtc_ref, kn_ref, vtn_ref, bias_c_ref, bias_n_ref, lam_ref, hg_ref,
                        o_ref, qbd_s, m_s, alpha_s, acc_s, s_scr, p_scr, *, lam_init):
    tq = q_ref.shape[2]
    tile = vtc_ref.shape[3]
    n_cache = vtc_ref.shape[1]
    ext = acc_s.shape[1]
    dv = o_ref.shape[2] // A_HEADS
    gw = qbd_s.shape[1]
    n_group = qbd_s.shape[0]
    n_sub = gw // tq
    row_blk = lax.broadcasted_iota(jnp.int32, (gw, gw), 0) // (gw // n_sub)
    col_blk = lax.broadcasted_iota(jnp.int32, (gw, gw), 1) // tq
    for g in range(n_group):
        qg = q_ref[0, g * gw:(g + 1) * gw, :]
        qrep = jnp.concatenate([qg] * n_sub, axis=1)
        qbd_s[g] = jnp.where(row_blk == col_blk, qrep, jnp.zeros_like(qrep))
    m_s[...] = jnp.full(m_s.shape, NEG_BIG, F32)
    acc_s[...] = jnp.zeros(acc_s.shape, F32)

    def kv_tile(k_tile, vt_tile, bias_ref, rows):
        for g in range(n_group):
            s_scr[g, :rows] = _dot(k_tile(g), qbd_s[g])
        for g in range(n_group):
            s = s_scr[g, :rows]
            if bias_ref is not None:
                s = s + bias_ref[g]
                s_scr[g, :rows] = s
            m_prev = m_s[g]
            m_new = jnp.maximum(m_prev, jnp.max(s, axis=0, keepdims=True))
            alpha_s[g] = jnp.exp2(m_prev - m_new)
            m_s[g] = m_new
        for g in range(n_group):
            for r in range(0, rows, ROW_CHUNK):
                p_scr[g, r:r + ROW_CHUNK] = jnp.exp2(s_scr[g, r:r + ROW_CHUNK] - m_s[g]).astype(BF16)
        for h in range(A_HEADS):
            g, half = divmod(h, 2)
            lanes = slice(half * 2 * tq, (half + 1) * 2 * tq)
            acc_s[h] = alpha_s[g, :, lanes] * acc_s[h] + _dot(vt_tile(h), p_scr[g, :rows, lanes])

    ones = jnp.ones((ext - dv, tile), BF16)

    def cache_tile(j, bias_ref):
        koff = pl.multiple_of(j * tile, tile)
        kv_tile(lambda g: kc_ref[0, pl.ds(koff, tile), g * gw:(g + 1) * gw],
                lambda h: jnp.concatenate([vtc_ref[0, j, h * dv:(h + 1) * dv, :], ones], axis=0),
                bias_ref, tile)

    def far_body(j, carry):
        cache_tile(j, None)
        return carry

    lax.fori_loop(0, n_cache - 1, far_body, 0)
    cache_tile(n_cache - 1, bias_c_ref)
    kv_tile(lambda g: kn_ref[0, :, g * gw:(g + 1) * gw],
            lambda h: vtn_ref[0, 0, h * ext:(h + 1) * ext, :], bias_n_ref, tq)

    lp = lam_ref[...]
    lam = (jnp.exp(jnp.sum(lp[0:1] * lp[1:2], axis=-1, keepdims=True))
           - jnp.exp(jnp.sum(lp[2:3] * lp[3:4], axis=-1, keepdims=True)) + lam_init)
    for h in range(A_HEADS):
        both = (acc_s[h, :dv] / acc_s[h, dv:dv + 1]).T
        o = both[:tq] - lam * both[tq:]
        o = _rms(o, hg_ref[...]) * (1.0 - lam_init)
        o_ref[0, :, h * dv:(h + 1) * dv] = o.astype(o_ref.dtype)


def _attn_sample(q, cache_k, cache_vt, new_k, new_vt, rel_bias, lam_p, head_g, lam_init):
    b, d, t = q.shape
    past = cache_k.shape[1]
    n_cache, _, tile = cache_vt.shape[1:]
    d_ext = new_vt.shape[2]
    assert past == n_cache * tile and past % CHUNK == 0 and tile >= MAX_DISTANCE
    dv = d // A_HEADS
    ext = d_ext // A_HEADS
    gw = 4 * t
    assert gw == 2 * dv and A_HEADS % 2 == 0
    n_group = A_HEADS // 2

    def packed(bias):
        bt = bias.swapaxes(-1, -2).reshape(n_group, 2, -1, t)
        return jnp.concatenate([bt[:, 0], bt[:, 0], bt[:, 1], bt[:, 1]], axis=-1)

    bias_c = packed(_bias_tile(rel_bias, past, past - tile, t, tile))
    bias_n = packed(_bias_tile(rel_bias, past, past, t, t))
    tok_spec = pl.BlockSpec((1, t, d), lambda i: (i, 0, 0))
    return pl.pallas_call(
        functools.partial(_attn_sample_kernel, lam_init=lam_init),
        grid=(b,),
        in_specs=[pl.BlockSpec((1, d, t), lambda i: (i, 0, 0)),
                  pl.BlockSpec((1, past, d), lambda i: (i, 0, 0)),
                  pl.BlockSpec((1, n_cache, d, tile), lambda i: (i, 0, 0, 0)),
                  tok_spec,
                  pl.BlockSpec((1, 1, d_ext, t), lambda i: (i, 0, 0, 0)),
                  _resident(bias_c.shape), _resident(bias_n.shape), _resident(lam_p.shape),
                  _resident((1, dv))],
        out_specs=tok_spec,
        out_shape=jax.ShapeDtypeStruct((b, t, d), BF16),
        scratch_shapes=[pltpu.VMEM((n_group, gw, gw), BF16),
                        pltpu.VMEM((n_group, 1, gw), F32), pltpu.VMEM((n_group, 1, gw), F32),
                        pltpu.VMEM((A_HEADS, ext, 2 * t), F32),
                        pltpu.VMEM((n_group, tile, gw), F32), pltpu.VMEM((n_group, tile, gw), BF16)],
        compiler_params=_params("parallel"),
        name="attn_sample",
    )(q, cache_k, cache_vt, new_k, new_vt, bias_c, bias_n, lam_p, head_g.reshape(1, dv))


def _cache_layouts(cache_k, cache_v, tile):
    b, past, heads, two, dh = cache_k.shape
    dv = two * dh
    d = heads * dv
    n = past // tile
    per = math.gcd(n, 4)
    span = per * tile
    k = pl.pallas_call(
        _keys_to_token_major_kernel,
        grid=(b,),
        in_specs=[pl.BlockSpec((1, d, past), lambda i: (i, 0, 0))],
        out_specs=pl.BlockSpec((1, past, d), lambda i: (i, 0, 0)),
        out_shape=jax.ShapeDtypeStruct((b, past, d), BF16),
        compiler_params=_params("parallel"),
        name="cache_keys_t",
    )(cache_k.transpose(0, 2, 3, 4, 1).reshape(b, d, past))
    vt = pl.pallas_call(
        functools.partial(_values_to_feature_major_kernel, heads=heads),
        grid=(b, n // per),
        in_specs=[pl.BlockSpec((1, span * heads, dv), lambda i, j: (i, j, 0))],
        out_specs=pl.BlockSpec((1, per, d, tile), lambda i, j: (i, j, 0, 0)),
        out_shape=jax.ShapeDtypeStruct((b, n, d, tile), BF16),
        compiler_params=_params("parallel", "parallel"),
        name="cache_values_t",
    )(cache_v.reshape(b, past * heads, dv))
    return k, vt


def _keys_to_token_major_kernel(k_ref, o_ref):
    for c in range(k_ref.shape[1] // LANES):
        o_ref[0, :, c * LANES:(c + 1) * LANES] = k_ref[0, c * LANES:(c + 1) * LANES, :].T.astype(BF16)


def _values_to_feature_major_kernel(v_ref, o_ref, *, heads):
    per, tile = o_ref.shape[1], o_ref.shape[3]
    dv = v_ref.shape[2]
    for j in range(per):
        for h in range(heads):
            rows = v_ref[0, pl.ds(j * tile * heads + h, tile, stride=heads), :]
            o_ref[0, j, h * dv:(h + 1) * dv, :] = rows.T.astype(BF16)


def _trunk(x, mod, mstate, past, chunk, rows, p):
    b, t, d = x.shape
    depth = p["norm_g"].shape[0]
    n_a = p["mlstm_w_in"].shape[0]
    qk = M_HEADS * (d // M_HEADS // 2)
    new_c, new_n, new_m = [], [], []
    prompt = past is None
    k_f32 = v_f32 = k_bf = vt_bf = None
    for l in range(depth):
        sh1, sc1, g1, sh2, sc2, g2 = [mod[l, :, i * d:(i + 1) * d].reshape(b, 1, d) for i in range(6)]
        if l < n_a:
            w_in = p["mlstm_w_in"][l]
            w_qko = jnp.concatenate([w_in[:, :2 * qk], w_in[:, 2 * qk + d:2 * qk + 2 * d]], axis=1)
            qko, vt, g, gt = _mlstm_proj(x, p["norm_g"][l, 0], sh1, sc1, w_qko, w_in[:, 2 * qk:2 * qk + d],
                                         w_in[:, 2 * qk + 2 * d:], rows=rows)
            hn, c_l, n_l, m_l = _mlstm(qko, vt, g, gt, p["mlstm_b_gates"][l], p["mlstm_head_g"][l],
                                       mstate[0][l], mstate[1][l], mstate[2][l], chunk=chunk)
            new_c.append(c_l)
            new_n.append(n_l)
            new_m.append(m_l)
            mixed, w_proj = hn, p["mlstm_w_out"][l]
        else:
            j = l - n_a
            lam_init = 0.8 - 0.6 * math.exp(-0.3 * l)
            dh = p["q_norm_g"].shape[-1]
            q = _q_proj(x, p["norm_g"][l, 0], sh1, sc1, p["attn_w_q"][j], p["q_norm_g"][j],
                        dh ** -0.5 * LOG2E, rows=rows)
            if prompt:
                o = _attn_prompt(q, k_bf, vt_bf, p["rel_bias"], p["attn_lambda"][j], p["attn_head_g"][j],
                                 lam_init)
            else:
                o = _attn_sample(q, past[0], past[1], k_bf, vt_bf, p["rel_bias"], p["attn_lambda"][j],
                                 p["attn_head_g"][j], lam_init)
            mixed, w_proj = o, p["attn_w_o"][j]
        x = _mixer_ffn(x, mixed, w_proj, g1, p["norm_g"][l, 1], sh2, sc2, g2, p["ffn_w_gate"][l],
                       p["ffn_w_up"][l], p["ffn_w_down"][l], rows=rows)
        if l == n_a - 1:
            k_f32, v_f32, k_bf, vt_bf = _shared_kv(x, p["kv_norm_g"], p["w_k"], p["w_v"], p["k_norm_g"],
                                                   rows=rows, attn_tile=ATTN_TILE if prompt else t)
    dh = p["k_norm_g"].shape[-1]
    return (x, jnp.stack(new_c), jnp.stack(new_n), jnp.stack(new_m),
            k_f32.reshape(b, t, A_HEADS, 2, dh), v_f32.reshape(b, t, A_HEADS, 2 * dh))


def kernel(x_prompt, x_sample, c_prompt, c_sample, state_mlstm_C, state_mlstm_n, state_mlstm_m, cache_k, cache_v, ada_w, ada_b, norm_g, mlstm_w_in, mlstm_b_gates, mlstm_head_g, mlstm_w_out, kv_norm_g, w_k, w_v, k_norm_g, attn_w_q, q_norm_g, attn_lambda, attn_head_g, attn_w_o, rel_bias, ffn_w_gate, ffn_w_up, ffn_w_down):
    bf = lambda w: w.astype(BF16)
    p = dict(norm_g=norm_g, mlstm_w_in=bf(mlstm_w_in), mlstm_b_gates=mlstm_b_gates,
             mlstm_head_g=mlstm_head_g, mlstm_w_out=bf(mlstm_w_out), kv_norm_g=kv_norm_g, w_k=bf(w_k),
             w_v=bf(w_v), k_norm_g=k_norm_g, attn_w_q=bf(attn_w_q), q_norm_g=q_norm_g,
             attn_lambda=attn_lambda, attn_head_g=attn_head_g, attn_w_o=bf(attn_w_o), rel_bias=rel_bias,
             ffn_w_gate=bf(ffn_w_gate), ffn_w_up=bf(ffn_w_up), ffn_w_down=bf(ffn_w_down))
    bp = x_prompt.shape[0]
    n_a, _, heads, dk, dv = state_mlstm_C.shape
    mod = _adaln(jnp.concatenate([c_prompt, c_sample], axis=0), ada_w, ada_b)
    zero_state = (jnp.zeros((n_a, bp, heads, dk, dv), F32), jnp.zeros((n_a, bp, heads, dk), F32),
                  jnp.zeros((n_a, bp, heads), F32))
    y_p, c_p, n_p, m_p, k_p, v_p = _trunk(x_prompt, mod[:, :bp], zero_state, None, MLSTM_PROMPT_CHUNK, 512, p)
    y_s, c_s, n_s, m_s, k_s, v_s = _trunk(x_sample, mod[:, bp:], (state_mlstm_C, state_mlstm_n, state_mlstm_m),
                                          _cache_layouts(cache_k, cache_v, CACHE_TILE), x_sample.shape[1], 512, p)
    return (y_p, y_s, c_p, n_p, m_p, k_p, v_p, c_s, n_s, m_s, k_s, v_s)
```

```python
import functools
import math

import jax
import jax.numpy as jnp
from jax import lax
from jax.experimental import pallas as pl
from jax.experimental.pallas import tpu as pltpu

F32 = jnp.float32
BF16 = jnp.bfloat16

EPS = 1e-6
NEG_BIG = -1e30
CHUNK = 64
N_BUCKETS = 32
MAX_DISTANCE = 128
M_HEADS = 8
A_HEADS = 8
GATE_PAD = 128
LANES = 128
SUBLANES = 8
ONES_ROWS = 16
LOG2E = math.log2(math.e)

V7X_VMEM_BYTES = 64 * 1024 * 1024
VMEM_LIMIT = V7X_VMEM_BYTES - 8 * 1024 * 1024

ATTN_TILE = 256
FAR_TILES = 2
CACHE_TILE = 512
MLSTM_PROMPT_CHUNK = 256
FFN_CHUNK = 256
ROW_CHUNK = 64
TOKEN_ROWS = 512
ADALN_TILE = 1536


def _params(*sem):
    return pltpu.CompilerParams(dimension_semantics=sem, vmem_limit_bytes=VMEM_LIMIT)


def _rms(x, g):
    return x * lax.rsqrt(jnp.mean(x * x, axis=-1, keepdims=True) + EPS) * g


def _log_sigmoid(x):
    return jnp.minimum(x, 0.0) - jnp.log1p(jnp.exp(-jnp.abs(x)))


def _dot(a, b):
    return jnp.dot(a, b, preferred_element_type=F32)


def _dot_nt(a, b):
    return lax.dot_general(a, b, (((1,), (1,)), ((), ())), preferred_element_type=F32)


def _resident(shape):
    nd = len(shape)
    return pl.BlockSpec(shape, lambda *_: (0,) * nd, pipeline_mode=pl.Buffered(1))


def _adaln_kernel(c_ref, w_ref, b_ref, o_ref):
    c = c_ref[...]
    a = (c * jax.nn.sigmoid(c)).astype(BF16)
    o_ref[0] = _dot(a, w_ref[0].astype(BF16)) + b_ref[0]


def _adaln(c, ada_w, ada_b):
    depth, d, n = ada_w.shape
    rows = c.shape[0]
    tn = ADALN_TILE
    assert n % tn == 0
    return pl.pallas_call(
        _adaln_kernel,
        grid=(depth, n // tn),
        in_specs=[
            pl.BlockSpec((rows, d), lambda l, j: (0, 0)),
            pl.BlockSpec((1, d, tn), lambda l, j: (l, 0, j)),
            pl.BlockSpec((1, 1, tn), lambda l, j: (l, 0, j)),
        ],
        out_specs=pl.BlockSpec((1, rows, tn), lambda l, j: (l, 0, j)),
        out_shape=jax.ShapeDtypeStruct((depth, rows, n), F32),
        compiler_params=_params("parallel", "parallel"),
        name="adaln",
    )(c, ada_w, ada_b.reshape(depth, 1, n))


def _modulated(x_ref, g_ref, sh_ref, sc_ref):
    x = x_ref[...]
    bb, tt, d = x.shape
    xm = _rms(x, g_ref[...]) * (1.0 + sc_ref[...]) + sh_ref[...]
    return xm.reshape(bb * tt, d).astype(BF16)


def _group_rms_feature_major(yt, gain_col):
    width = gain_col.shape[0]
    groups = []
    for g in range(yt.shape[0] // width):
        seg = yt[g * width:(g + 1) * width]
        groups.append(seg * lax.rsqrt(jnp.mean(seg * seg, axis=0, keepdims=True) + EPS) * gain_col)
    return groups


def _store_values_ext(dst_ref, lead, vt, col, width, heads):
    dv = vt.shape[0] // heads
    ext = dv + ONES_ROWS
    ones = jnp.ones((ONES_ROWS, width), BF16)
    for hh in range(heads):
        dst_ref[lead + (slice(hh * ext, hh * ext + dv), slice(None))] = vt[hh * dv:(hh + 1) * dv, col:col + width]
        dst_ref[lead + (slice(hh * ext + dv, (hh + 1) * ext), slice(None))] = ones


def _mlstm_proj_kernel(x_ref, g_ref, sh_ref, sc_ref, w_ref, wf_ref, wg_ref, wgt_ref,
                       o_ref, of_ref, og_ref, ogt_ref, *, n_chunk):
    bb, tt, n = o_ref.shape
    xm = _modulated(x_ref, g_ref, sh_ref, sc_ref)
    for c in range(n // n_chunk):
        y = _dot(xm, w_ref[:, c * n_chunk:(c + 1) * n_chunk])
        o_ref[:, :, c * n_chunk:(c + 1) * n_chunk] = y.reshape(bb, tt, n_chunk).astype(o_ref.dtype)
    vt = _dot_nt(wf_ref[...], xm).astype(BF16)
    for i in range(bb):
        _store_values_ext(of_ref, (i,), vt, i * tt, tt, M_HEADS)
        ogt_ref[i] = _dot_nt(wgt_ref[...], xm[i * tt:(i + 1) * tt])
    og_ref[...] = _dot(xm, wg_ref[...]).reshape(bb, tt, 2 * GATE_PAD)


def _q_proj_kernel(x_ref, g_ref, sh_ref, sc_ref, wt_ref, qg_ref, o_ref, *, qk_scale):
    bb, n, tt = o_ref.shape
    xm = _modulated(x_ref, g_ref, sh_ref, sc_ref)
    groups = _group_rms_feature_major(_dot_nt(wt_ref[...], xm), qg_ref[...])
    width = qg_ref.shape[0]
    for g, seg in enumerate(groups):
        seg = (seg * qk_scale).astype(o_ref.dtype)
        for i in range(bb):
            o_ref[i, g * width:(g + 1) * width, :] = seg[:, i * tt:(i + 1) * tt]


def _row_blocks(b, t, rows):
    if t >= rows:
        assert t % rows == 0
        return 1, rows
    assert rows % t == 0 and b % (rows // t) == 0
    return rows // t, t


def _token_specs(x, rows):
    b, t, d = x.shape
    bb, tt = _row_blocks(b, t, rows)
    per_b = pl.BlockSpec((bb, 1, d), lambda i, j: (i, 0, 0))
    return bb, tt, (b // bb, t // tt), pl.BlockSpec((bb, tt, d), lambda i, j: (i, j, 0)), per_b


def _mlstm_proj(x, gain, shift, scale, w, w_feat, w_gate, *, rows):
    b, t, d = x.shape
    n = w.shape[1]
    bb, tt, grid, x_spec, per_b = _token_specs(x, rows)
    tok = lambda i, j: (i, j, 0)
    feat = lambda i, j: (i, 0, j)
    ng = w_gate.shape[1]
    half = ng // 2
    wg = (jnp.zeros((d, 2 * GATE_PAD), BF16).at[:, :half].set(w_gate[:, :half])
          .at[:, GATE_PAD:GATE_PAD + half].set(w_gate[:, half:]))
    n_feat = w_feat.shape[1]
    d_ext = n_feat + M_HEADS * ONES_ROWS
    return pl.pallas_call(
        functools.partial(_mlstm_proj_kernel, n_chunk=min(n, 1024)),
        grid=grid,
        in_specs=[x_spec, _resident((1, d)), per_b, per_b, _resident((d, n)),
                  _resident((n_feat, d)), _resident((d, 2 * GATE_PAD)), _resident((ng, d))],
        out_specs=[pl.BlockSpec((bb, tt, n), tok), pl.BlockSpec((bb, d_ext, tt), feat),
                   pl.BlockSpec((bb, tt, 2 * GATE_PAD), tok), pl.BlockSpec((bb, ng, tt), feat)],
        out_shape=[jax.ShapeDtypeStruct((b, t, n), BF16), jax.ShapeDtypeStruct((b, d_ext, t), BF16),
                   jax.ShapeDtypeStruct((b, t, 2 * GATE_PAD), F32), jax.ShapeDtypeStruct((b, ng, t), F32)],
        compiler_params=_params("parallel", "parallel"), name="mlstm_proj",
    )(x, gain.reshape(1, d), shift, scale, w, w_feat.T, wg, w_gate.T)


def _q_proj(x, gain, shift, scale, w, qk_gain, qk_scale, *, rows):
    b, t, d = x.shape
    n = w.shape[1]
    bb, tt, grid, x_spec, per_b = _token_specs(x, rows)
    return pl.pallas_call(
        functools.partial(_q_proj_kernel, qk_scale=qk_scale),
        grid=grid,
        in_specs=[x_spec, _resident((1, d)), per_b, per_b, _resident((n, d)), _resident((qk_gain.shape[-1], 1))],
        out_specs=pl.BlockSpec((bb, n, tt), lambda i, j: (i, 0, j)),
        out_shape=jax.ShapeDtypeStruct((b, n, t), BF16),
        compiler_params=_params("parallel", "parallel"), name="q_proj",
    )(x, gain.reshape(1, d), shift, scale, w.T, qk_gain.reshape(-1, 1))


def _kv_kernel(x_ref, g_ref, wkt_ref, wv_ref, wvt_ref, kg_ref, k_ref, v_ref, kb_ref, vt_ref):
    x = x_ref[...]
    bb, tt, d = x.shape
    h = _rms(x, g_ref[...]).reshape(bb * tt, d).astype(BF16)
    groups = _group_rms_feature_major(_dot_nt(wkt_ref[...], h), kg_ref[...])
    per_tile = LANES // kg_ref.shape[0]
    for c in range(d // LANES):
        k_tok = jnp.concatenate(groups[c * per_tile:(c + 1) * per_tile], axis=0).T.reshape(bb, tt, LANES)
        k_ref[:, :, c * LANES:(c + 1) * LANES] = k_tok
        kb_ref[:, :, c * LANES:(c + 1) * LANES] = k_tok.astype(BF16)
    v_ref[...] = _dot(h, wv_ref[...]).reshape(bb, tt, d)
    vt = _dot_nt(wvt_ref[...], h).astype(BF16)
    per, tile = vt_ref.shape[1], vt_ref.shape[3]
    for i in range(bb):
        for j in range(per):
            _store_values_ext(vt_ref, (i, j), vt, (i * per + j) * tile, tile, A_HEADS)


def _shared_kv(x, kv_g, w_k, w_v, k_g, *, rows, attn_tile):
    b, t, d = x.shape
    bb, tt = _row_blocks(b, t, rows)
    assert tt % attn_tile == 0
    per = tt // attn_tile
    d_ext = d + A_HEADS * ONES_ROWS
    tok = lambda i, j: (i, j, 0)
    blk = pl.BlockSpec((bb, tt, d), tok)
    return pl.pallas_call(
        _kv_kernel,
        grid=(b // bb, t // tt),
        in_specs=[blk, _resident((1, d)), _resident((d, d)), _resident((d, d)), _resident((d, d)),
                  _resident((k_g.shape[-1], 1))],
        out_specs=[blk, blk, blk, pl.BlockSpec((bb, per, d_ext, attn_tile), lambda i, j: (i, j, 0, 0))],
        out_shape=[jax.ShapeDtypeStruct((b, t, d), F32), jax.ShapeDtypeStruct((b, t, d), F32),
                   jax.ShapeDtypeStruct((b, t, d), BF16),
                   jax.ShapeDtypeStruct((b, t // attn_tile, d_ext, attn_tile), BF16)],
        compiler_params=_params("parallel", "parallel"),
        name="shared_kv",
    )(x, kv_g.reshape(1, d), w_k.T, w_v, w_v.T, k_g.reshape(-1, 1))


def _mixer_ffn_kernel(x_ref, a_ref, wp_ref, gate1_ref, g_ref, sh_ref, sc_ref, gate2_ref, wg_ref, wu_ref, wd_ref,
                      o_ref, acc_ref):
    bb, tt, d = x_ref.shape
    f = wg_ref.shape[1]
    a = a_ref[...].reshape(bb * tt, a_ref.shape[-1])
    o_ref[...] = x_ref[...] + gate1_ref[...] * _dot(a, wp_ref[...]).reshape(bb, tt, d)
    xm = _modulated(o_ref, g_ref, sh_ref, sc_ref)
    for c in range(f // FFN_CHUNK):
        sl = slice(c * FFN_CHUNK, (c + 1) * FFN_CHUNK)
        hg = _dot(xm, wg_ref[:, sl])
        hu = _dot(xm, wu_ref[:, sl])
        act = (hg * jax.nn.sigmoid(hg) * hu).astype(BF16)
        part = _dot(act, wd_ref[sl, :])
        if c == 0:
            acc_ref[...] = part
        else:
            acc_ref[...] += part
    o_ref[...] += gate2_ref[...] * acc_ref[...].reshape(bb, tt, d)


def _mixer_ffn(x, a, w_proj, gate1, gain, shift, scale, gate2, w_gate, w_up, w_down, *, rows):
    b, t, d = x.shape
    k = a.shape[-1]
    f = w_gate.shape[1]
    assert f % FFN_CHUNK == 0
    bb, tt = _row_blocks(b, t, rows)
    tok = lambda i, j: (i, j, 0)
    per_b = pl.BlockSpec((bb, 1, d), lambda i, j: (i, 0, 0))
    return pl.pallas_call(
        _mixer_ffn_kernel,
        grid=(b // bb, t // tt),
        in_specs=[pl.BlockSpec((bb, tt, d), tok), pl.BlockSpec((bb, tt, k), tok), _resident((k, d)), per_b,
                  _resident((1, d)), per_b, per_b, per_b,
                  _resident((d, f)), _resident((d, f)), _resident((f, d))],
        out_specs=pl.BlockSpec((bb, tt, d), tok),
        out_shape=jax.ShapeDtypeStruct((b, t, d), F32),
        scratch_shapes=[pltpu.VMEM((bb * tt, d), F32)],
        compiler_params=_params("parallel", "parallel"),
        name="mixer_ffn",
    )(x, a, w_proj, gate1, gain.reshape(1, d), shift, scale, gate2, w_gate, w_up, w_down)


def _mlstm_kernel(q_ref, k_ref, o_ref, vt_ref, g_ref, gt_ref, brow_ref, bcol_ref, hg_ref,
                  c0_ref, n0_ref, m0_ref, h_ref, c_out, n_out, m_out,
                  ct_s, m_s, mnew_s, acol_s, br_s, inter_s, wn_s, wold_s, qk_s, qc_s, sb_s, sv_s, stat_s):
    ci = pl.program_id(1)
    length = q_ref.shape[1]
    ext = ct_s.shape[1]
    dv = ext - ONES_ROWS
    dk = q_ref.shape[2] // M_HEADS
    heads = range(M_HEADS)

    @pl.when(ci == 0)
    def _():
        ct_s[...] = jnp.zeros(ct_s.shape, F32)
        for h in heads:
            c0 = jnp.concatenate([c0_ref[0, h], jnp.zeros((dv - dk, dv), F32)], axis=0)
            ct_s[h, :dv, :] = c0.T
            ct_s[h, dv:, :dk] = jnp.broadcast_to(n0_ref[0, h:h + 1, :], (ONES_ROWS, dk))
        m_s[...] = m0_ref[0]

    row = lax.broadcasted_iota(jnp.int32, (length, length), 0)
    col = lax.broadcasted_iota(jnp.int32, (length, length), 1)
    causal = row <= col
    hi = lax.Precision.HIGHEST
    g = g_ref[0] + brow_ref[...]
    b_cols = jnp.dot((col <= row).astype(F32), _log_sigmoid(g[:, GATE_PAD:]), precision=hi,
                     preferred_element_type=F32)
    acol_s[...] = g[:, :GATE_PAD] - b_cols
    gt = gt_ref[0] + bcol_ref[...]
    ig_rows = gt[:M_HEADS]
    br_s[...] = jnp.dot(_log_sigmoid(gt[M_HEADS:]), causal.astype(F32), precision=hi,
                        preferred_element_type=F32)
    b_rows = br_s[...]
    b_end = br_s[:, length - 1:length]
    m_prev = m_s[...]
    g_rows = b_end - b_rows + ig_rows
    m_new = jnp.maximum(b_end + m_prev, jnp.max(g_rows, axis=-1, keepdims=True))
    mnew_s[...] = m_new
    wold_s[...] = jnp.exp(b_end + m_prev - m_new)
    wn_s[...] = jnp.exp(g_rows - m_new)
    inter_s[...] = b_rows + m_prev

    def q_of(h):
        return q_ref[0, :, h * dk:(h + 1) * dk]

    def k_of(h):
        return k_ref[0, :, h * dk:(h + 1) * dk] * (dk ** -0.5)

    def vt_of(h):
        return vt_ref[0, h * ext:(h + 1) * ext, :]

    for h in heads:
        qk_s[h] = _dot_nt(k_of(h), q_of(h))
        qc_s[h] = _dot_nt(ct_s[h, :, :dk].astype(BF16), q_of(h))
    for h in heads:
        inter = inter_s[h:h + 1, :]
        dmat = jnp.where(causal, acol_s[:, h:h + 1] + br_s[h:h + 1, :], -jnp.inf)
        m_t = jnp.maximum(inter, jnp.max(dmat, axis=0, keepdims=True))
        sb_s[h] = (qk_s[h] * jnp.exp(dmat - m_t)).astype(BF16)
        stat_s[0, h] = jnp.exp(inter - m_t)
        stat_s[1, h] = jnp.exp(-m_t)
    for h in heads:
        sv_s[h] = _dot(vt_of(h), sb_s[h])
    for h in heads:
        w_inter = stat_s[0, h]
        tot = w_inter * qc_s[h] + sv_s[h]
        hh = tot[:dv] / jnp.maximum(jnp.abs(tot[dv:dv + 1]), stat_s[1, h])
        hh = hh * lax.rsqrt(jnp.mean(hh * hh, axis=0, keepdims=True) + EPS)
        og = o_ref[0, :, h * dv:(h + 1) * dv].astype(F32)
        h_ref[0, :, h * dv:(h + 1) * dv] = (_to_token_major(hh) * hg_ref[:, h * dv:(h + 1) * dv]
                                            * jax.nn.sigmoid(og)).astype(h_ref.dtype)
    for h in heads:
        vtw = (vt_of(h).astype(F32) * wn_s[h:h + 1, :]).astype(BF16)
        ct_s[h, :, :dk] = wold_s[h:h + 1, :] * ct_s[h, :, :dk] + _dot(vtw, k_of(h))
    m_s[...] = mnew_s[...]

    @pl.when(ci == pl.num_programs(1) - 1)
    def _():
        for h in heads:
            c_out[0, h] = ct_s[h, :dv, :].T[:dk]
            n_out[0, h:h + 1, :] = ct_s[h, dv:dv + 1, :dk]
        m_out[0] = m_s[...]


def _to_token_major(x):
    dv, length = x.shape
    if length % LANES:
        pad = LANES - length % LANES
        x = jnp.concatenate([x, jnp.zeros((dv, pad), x.dtype)], axis=1)
    return x.T[:length]


def _mlstm(qko, vt, g, gt, b_gates, head_g, c0, n0, m0, *, chunk):
    b, t, _ = qko.shape
    heads, dk, dv = c0.shape[1:]
    d = heads * dv
    qk = heads * dk
    ext = dv + ONES_ROWS
    assert qk * 2 == d and t % chunk == 0 and heads == M_HEADS and vt.shape[1] == heads * ext
    nc = t // chunk
    two_h = 2 * heads
    brow = (jnp.zeros((1, 2 * GATE_PAD), F32).at[0, :heads].set(b_gates[:heads])
            .at[0, GATE_PAD:GATE_PAD + heads].set(b_gates[heads:]))
    st4 = lambda i, c: (i, 0, 0, 0)
    st3 = lambda i, c: (i, 0, 0)
    outs = pl.pallas_call(
        _mlstm_kernel,
        grid=(b, nc),
        in_specs=[
            pl.BlockSpec((1, chunk, qk), lambda i, c: (i, c, 0)),
            pl.BlockSpec((1, chunk, qk), lambda i, c: (i, c, 1)),
            pl.BlockSpec((1, chunk, d), lambda i, c: (i, c, 1)),
            pl.BlockSpec((1, heads * ext, chunk), lambda i, c: (i, 0, c)),
            pl.BlockSpec((1, chunk, 2 * GATE_PAD), lambda i, c: (i, c, 0)),
            pl.BlockSpec((1, two_h, chunk), lambda i, c: (i, 0, c)),
            _resident((1, 2 * GATE_PAD)), _resident((two_h, 1)), _resident((1, d)),
            pl.BlockSpec((1, heads, dk, dv), st4),
            pl.BlockSpec((1, heads, dk), st3),
            pl.BlockSpec((1, heads, 1), st3),
        ],
        out_specs=[
            pl.BlockSpec((1, chunk, d), lambda i, c: (i, c, 0)),
            pl.BlockSpec((1, heads, dk, dv), st4),
            pl.BlockSpec((1, heads, dk), st3),
            pl.BlockSpec((1, heads, 1), st3),
        ],
        out_shape=[
            jax.ShapeDtypeStruct((b, t, d), BF16),
            jax.ShapeDtypeStruct((b, heads, dk, dv), F32),
            jax.ShapeDtypeStruct((b, heads, dk), F32),
            jax.ShapeDtypeStruct((b, heads, 1), F32),
        ],
        scratch_shapes=[pltpu.VMEM((heads, ext, dv), F32),
                        pltpu.VMEM((heads, 1), F32), pltpu.VMEM((heads, 1), F32),
                        pltpu.VMEM((chunk, GATE_PAD), F32),
                        pltpu.VMEM((heads, chunk), F32), pltpu.VMEM((heads, chunk), F32),
                        pltpu.VMEM((heads, chunk), F32), pltpu.VMEM((heads, 1), F32),
                        pltpu.VMEM((heads, chunk, chunk), F32),
                        pltpu.VMEM((heads, ext, chunk), F32),
                        pltpu.VMEM((heads, chunk, chunk), BF16),
                        pltpu.VMEM((heads, ext, chunk), F32),
                        pltpu.VMEM((2, heads, 1, chunk), F32)],
        compiler_params=_params("parallel", "arbitrary"),
        name="mlstm",
    )(qko, qko, qko, vt, g, gt, brow, b_gates.reshape(two_h, 1), head_g.reshape(1, d),
      c0, n0, m0.reshape(b, heads, 1))
    hn, c_new, n_new, m_new = outs
    return hn, c_new, n_new, m_new.reshape(b, heads)


def _rel_bucket(rel):
    nb = N_BUCKETS // 2
    max_exact = nb // 2
    ret = jnp.where(rel > 0, nb, 0)
    n = jnp.abs(rel)
    large = max_exact + (jnp.log(jnp.maximum(n, 1).astype(F32) / max_exact)
                         / math.log(MAX_DISTANCE / max_exact) * (nb - max_exact)).astype(jnp.int32)
    large = jnp.minimum(large, nb - 1)
    return ret + jnp.where(n < max_exact, n, large)


def _bias_tile(rel_bias, q0, k0, tq, tk):
    table = rel_bias.astype(F32)
    span = tq + tk
    r = jnp.arange(span)
    r = jnp.where(r < tk, r, r - span)
    vec = (table[_rel_bucket(r + (k0 - q0))] - table[N_BUCKETS // 2 - 1]).T
    bias = jnp.tile(vec, (1, tq))[:, :tq * (span - 1)].reshape(-1, tq, span - 1)[:, :, :tk]
    q_chunk = (q0 + lax.broadcasted_iota(jnp.int32, (tq, tk), 0)) // CHUNK
    k_chunk = (k0 + lax.broadcasted_iota(jnp.int32, (tq, tk), 1)) // CHUNK
    return jnp.where((k_chunk <= q_chunk)[None], bias * LOG2E, NEG_BIG)


def _attn_prompt_kernel(q_ref, k_ref, vt_ref, bias_ref, lam_ref, hg_ref, o_ref,
                        m_s, alpha_s, acc_s, s_scr, p_scr, *, lam_init):
    qi = pl.program_id(1)
    tile = q_ref.shape[2]
    ext = acc_s.shape[1]
    dv = o_ref.shape[2] // A_HEADS
    dh = dv // 2
    n_chain = 2 * A_HEADS
    m_s[...] = jnp.full(m_s.shape, NEG_BIG, F32)
    acc_s[...] = jnp.zeros(acc_s.shape, F32)

    def kv_tiles(j, slots, count):
        rows = count * tile
        koff = pl.multiple_of(j * tile, tile)
        for hc in range(n_chain):
            lo = hc * dh
            s_scr[hc, :rows] = _dot(k_ref[0, pl.ds(koff, rows), lo:lo + dh], q_ref[0, lo:lo + dh, :])
        for hc in range(n_chain):
            m_prev = m_s[hc]
            m_part = None
            for r in range(0, rows, ROW_CHUNK):
                s = s_scr[hc, r:r + ROW_CHUNK]
                if slots is not None:
                    t, rt = divmod(r, tile)
                    s = s + bias_ref[hc // 2, slots[t], rt:rt + ROW_CHUNK]
                    s_scr[hc, r:r + ROW_CHUNK] = s
                part = jnp.max(s.reshape(ROW_CHUNK // SUBLANES, SUBLANES, tile), axis=0)
                m_part = part if m_part is None else jnp.maximum(m_part, part)
            m_new = jnp.maximum(m_prev, jnp.max(m_part, axis=0, keepdims=True))
            alpha_s[hc] = jnp.exp2(m_prev - m_new)
            m_s[hc] = m_new
        for hc in range(n_chain):
            for r in range(0, rows, ROW_CHUNK):
                p_scr[hc, r:r + ROW_CHUNK] = jnp.exp2(s_scr[hc, r:r + ROW_CHUNK] - m_s[hc]).astype(BF16)
        for hc in range(n_chain):
            h = hc // 2
            pv = _dot(vt_ref[0, j, h * ext:(h + 1) * ext, :], p_scr[hc, :tile])
            for t in range(1, count):
                pv = pv + _dot(vt_ref[0, j + t, h * ext:(h + 1) * ext, :], p_scr[hc, t * tile:(t + 1) * tile])
            acc_s[hc] = alpha_s[hc] * acc_s[hc] + pv

    n_far = jnp.maximum(qi - 1, 0)

    def far_body(i, carry):
        kv_tiles(FAR_TILES * i, None, FAR_TILES)
        return carry

    lax.fori_loop(0, n_far // FAR_TILES, far_body, 0)

    def far_rest(j, carry):
        kv_tiles(j, None, 1)
        return carry

    lax.fori_loop(n_far - n_far % FAR_TILES, n_far, far_rest, 0)

    @pl.when(qi > 0)
    def _():
        kv_tiles(qi - 1, (0, 1), 2)

    @pl.when(qi == 0)
    def _():
        kv_tiles(qi, (1,), 1)


    lp = lam_ref[...]
    lam = (jnp.exp(jnp.sum(lp[0:1] * lp[1:2], axis=-1, keepdims=True))
           - jnp.exp(jnp.sum(lp[2:3] * lp[3:4], axis=-1, keepdims=True)) + lam_init)
    for h in range(A_HEADS):
        o = (acc_s[2 * h, :dv] / acc_s[2 * h, dv:dv + 1]
             - lam * (acc_s[2 * h + 1, :dv] / acc_s[2 * h + 1, dv:dv + 1]))
        o = o * lax.rsqrt(jnp.mean(o * o, axis=0, keepdims=True) + EPS) * hg_ref[...] * (1.0 - lam_init)
        o_ref[0, :, h * dv:(h + 1) * dv] = o.T.astype(o_ref.dtype)


def _attn_prompt(q, k, vt, rel_bias, lam_p, head_g, lam_init):
    b, d, t = q.shape
    tile = ATTN_TILE
    assert t % tile == 0 and tile % CHUNK == 0 and tile >= MAX_DISTANCE and vt.shape[-1] == tile
    n = t // tile
    dv = d // A_HEADS
    ext = vt.shape[2] // A_HEADS
    n_chain = 2 * A_HEADS
    bias = jnp.stack([_bias_tile(rel_bias, tile, 0, tile, tile), _bias_tile(rel_bias, 0, 0, tile, tile)],
                     axis=1).swapaxes(-1, -2)
    return pl.pallas_call(
        functools.partial(_attn_prompt_kernel, lam_init=lam_init),
        grid=(b, n),
        in_specs=[pl.BlockSpec((1, d, tile), lambda i, qi: (i, 0, qi)),
                  pl.BlockSpec((1, t, d), lambda i, qi: (i, 0, 0), pipeline_mode=pl.Buffered(1)),
                  pl.BlockSpec((1,) + vt.shape[1:], lambda i, qi: (i, 0, 0, 0), pipeline_mode=pl.Buffered(1)),
                  _resident(bias.shape), _resident(lam_p.shape), _resident((dv, 1))],
        out_specs=pl.BlockSpec((1, tile, d), lambda i, qi: (i, qi, 0)),
        out_shape=jax.ShapeDtypeStruct((b, t, d), BF16),
        scratch_shapes=[pltpu.VMEM((n_chain, 1, tile), F32), pltpu.VMEM((n_chain, 1, tile), F32),
                        pltpu.VMEM((n_chain, ext, tile), F32),
                        pltpu.VMEM((n_chain, FAR_TILES * tile, tile), F32),
                        pltpu.VMEM((n_chain, FAR_TILES * tile, tile), BF16)],
        compiler_params=_params("parallel", "arbitrary"),
        name="attn_prompt",
    )(q, k, vt, bias, lam_p, head_g.reshape(dv, 1))


def _attn_sample_kernel(q_ref, kc_ref, vtc_ref, kn_ref, vtn_ref, bias_c_ref, bias_n_ref, lam_ref, hg_ref,
                        o_ref, qbd_s, m_s, alpha_s, acc_s, s_scr, p_scr, *, lam_init):
    tq = q_ref.shape[2]
    tile = vtc_ref.shape[3]
    n_cache = vtc_ref.shape[1]
    ext = acc_s.shape[1]
    dv = o_ref.shape[2] // A_HEADS
    gw = qbd_s.shape[1]
    n_group = qbd_s.shape[0]
    n_sub = gw // tq
    row_blk = lax.broadcasted_iota(jnp.int32, (gw, gw), 0) // (gw // n_sub)
    col_blk = lax.broadcasted_iota(jnp.int32, (gw, gw), 1) // tq
    for g in range(n_group):
        qg = q_ref[0, g * gw:(g + 1) * gw, :]
        qrep = jnp.concatenate([qg] * n_sub, axis=1)
        qbd_s[g] = jnp.where(row_blk == col_blk, qrep, jnp.zeros_like(qrep))
    m_s[...] = jnp.full(m_s.shape, NEG_BIG, F32)
    acc_s[...] = jnp.zeros(acc_s.shape, F32)

    def kv_tile(k_tile, vt_tile, bias_ref, rows):
        for g in range(n_group):
            s_scr[g, :rows] = _dot(k_tile(g), qbd_s[g])
        for g in range(n_group):
            s = s_scr[g, :rows]
            if bias_ref is not None:
                s = s + bias_ref[g]
                s_scr[g, :rows] = s
            m_prev = m_s[g]
            m_new = jnp.maximum(m_prev, jnp.max(s, axis=0, keepdims=True))
            alpha_s[g] = jnp.exp2(m_prev - m_new)
            m_s[g] = m_new
        for g in range(n_group):
            for r in range(0, rows, ROW_CHUNK):
                p_scr[g, r:r + ROW_CHUNK] = jnp.exp2(s_scr[g, r:r + ROW_CHUNK] - m_s[g]).astype(BF16)
        for h in range(A_HEADS):
            g, half = divmod(h, 2)
            lanes = slice(half * 2 * tq, (half + 1) * 2 * tq)
            acc_s[h] = alpha_s[g, :, lanes] * acc_s[h] + _dot(vt_tile(h), p_scr[g, :rows, lanes])

    ones = jnp.ones((ext - dv, tile), BF16)

    def cache_tile(j, bias_ref):
        koff = pl.multiple_of(j * tile, tile)
        kv_tile(lambda g: kc_ref[0, pl.ds(koff, tile), g * gw:(g + 1) * gw],
                lambda h: jnp.concatenate([vtc_ref[0, j, h * dv:(h + 1) * dv, :], ones], axis=0),
                bias_ref, tile)

    def far_body(j, carry):
        cache_tile(j, None)
        return carry

    lax.fori_loop(0, n_cache - 1, far_body, 0)
    cache_tile(n_cache - 1, bias_c_ref)
    kv_tile(lambda g: kn_ref[0, :, g * gw:(g + 1) * gw],
            lambda h: vtn_ref[0, 0, h * ext:(h + 1) * ext, :], bias_n_ref, tq)

    lp = lam_ref[...]
    lam = (jnp.exp(jnp.sum(lp[0:1] * lp[1:2], axis=-1, keepdims=True))
           - jnp.exp(jnp.sum(lp[2:3] * lp[3:4], axis=-1, keepdims=True)) + lam_init)
    for h in range(A_HEADS):
        both = (acc_s[h, :dv] / acc_s[h, dv:dv + 1]).T
        o = both[:tq] - lam * both[tq:]
        o = _rms(o, hg_ref[...]) * (1.0 - lam_init)
        o_ref[0, :, h * dv:(h + 1) * dv] = o.astype(o_ref.dtype)


def _attn_sample(q, cache_k, cache_vt, new_k, new_vt, rel_bias, lam_p, head_g, lam_init):
    b, d, t = q.shape
    past = cache_k.shape[1]
    n_cache, _, tile = cache_vt.shape[1:]
    d_ext = new_vt.shape[2]
    assert past == n_cache * tile and past % CHUNK == 0 and tile >= MAX_DISTANCE
    dv = d // A_HEADS
    ext = d_ext // A_HEADS
    gw = 4 * t
    assert gw == 2 * dv and A_HEADS % 2 == 0
    n_group = A_HEADS // 2

    def packed(bias):
        bt = bias.swapaxes(-1, -2).reshape(n_group, 2, -1, t)
        return jnp.concatenate([bt[:, 0], bt[:, 0], bt[:, 1], bt[:, 1]], axis=-1)

    bias_c = packed(_bias_tile(rel_bias, past, past - tile, t, tile))
    bias_n = packed(_bias_tile(rel_bias, past, past, t, t))
    tok_spec = pl.BlockSpec((1, t, d), lambda i: (i, 0, 0))
    return pl.pallas_call(
        functools.partial(_attn_sample_kernel, lam_init=lam_init),
        grid=(b,),
        in_specs=[pl.BlockSpec((1, d, t), lambda i: (i, 0, 0)),
                  pl.BlockSpec((1, past, d), lambda i: (i, 0, 0)),
                  pl.BlockSpec((1, n_cache, d, tile), lambda i: (i, 0, 0, 0)),
                  tok_spec,
                  pl.BlockSpec((1, 1, d_ext, t), lambda i: (i, 0, 0, 0)),
                  _resident(bias_c.shape), _resident(bias_n.shape), _resident(lam_p.shape),
                  _resident((1, dv))],
        out_specs=tok_spec,
        out_shape=jax.ShapeDtypeStruct((b, t, d), BF16),
        scratch_shapes=[pltpu.VMEM((n_group, gw, gw), BF16),
                        pltpu.VMEM((n_group, 1, gw), F32), pltpu.VMEM((n_group, 1, gw), F32),
                        pltpu.VMEM((A_HEADS, ext, 2 * t), F32),
                        pltpu.VMEM((n_group, tile, gw), F32), pltpu.VMEM((n_group, tile, gw), BF16)],
        compiler_params=_params("parallel"),
        name="attn_sample",
    )(q, cache_k, cache_vt, new_k, new_vt, bias_c, bias_n, lam_p, head_g.reshape(1, dv))


def _cache_layouts(cache_k, cache_v, tile):
    b, past, heads, two, dh = cache_k.shape
    dv = two * dh
    d = heads * dv
    n = past // tile
    per = math.gcd(n, 4)
    span = per * tile
    k = pl.pallas_call(
        _keys_to_token_major_kernel,
        grid=(b, past // span),
        in_specs=[pl.BlockSpec((1, d, span), lambda i, j: (i, 0, j))],
        out_specs=pl.BlockSpec((1, span, d), lambda i, j: (i, j, 0)),
        out_shape=jax.ShapeDtypeStruct((b, past, d), BF16),
        compiler_params=_params("parallel", "parallel"),
        name="cache_keys_t",
    )(cache_k.transpose(0, 2, 3, 4, 1).reshape(b, d, past))
    vt = pl.pallas_call(
        functools.partial(_values_to_feature_major_kernel, heads=heads),
        grid=(b, n // per),
        in_specs=[pl.BlockSpec((1, span * heads, dv), lambda i, j: (i, j, 0))],
        out_specs=pl.BlockSpec((1, per, d, tile), lambda i, j: (i, j, 0, 0)),
        out_shape=jax.ShapeDtypeStruct((b, n, d, tile), BF16),
        compiler_params=_params("parallel", "parallel"),
        name="cache_values_t",
    )(cache_v.reshape(b, past * heads, dv))
    return k, vt


def _keys_to_token_major_kernel(k_ref, o_ref):
    for c in range(k_ref.shape[1] // LANES):
        o_ref[0, :, c * LANES:(c + 1) * LANES] = k_ref[0, c * LANES:(c + 1) * LANES, :].T.astype(BF16)


def _values_to_feature_major_kernel(v_ref, o_ref, *, heads):
    per, tile = o_ref.shape[1], o_ref.shape[3]
    dv = v_ref.shape[2]
    for j in range(per):
        for h in range(heads):
            rows = v_ref[0, pl.ds(j * tile * heads + h, tile, stride=heads), :]
            o_ref[0, j, h * dv:(h + 1) * dv, :] = rows.T.astype(BF16)


def _trunk(x, mod, mstate, past, chunk, rows, p):
    b, t, d = x.shape
    depth = p["norm_g"].shape[0]
    n_a = p["mlstm_w_in"].shape[0]
    qk = M_HEADS * (d // M_HEADS // 2)
    new_c, new_n, new_m = [], [], []
    prompt = past is None
    k_f32 = v_f32 = k_bf = vt_bf = None
    for l in range(depth):
        sh1, sc1, g1, sh2, sc2, g2 = [mod[l, :, i * d:(i + 1) * d].reshape(b, 1, d) for i in range(6)]
        if l < n_a:
            w_in = p["mlstm_w_in"][l]
            w_qko = jnp.concatenate([w_in[:, :2 * qk], w_in[:, 2 * qk + d:2 * qk + 2 * d]], axis=1)
            qko, vt, g, gt = _mlstm_proj(x, p["norm_g"][l, 0], sh1, sc1, w_qko, w_in[:, 2 * qk:2 * qk + d],
                                         w_in[:, 2 * qk + 2 * d:], rows=rows)
            hn, c_l, n_l, m_l = _mlstm(qko, vt, g, gt, p["mlstm_b_gates"][l], p["mlstm_head_g"][l],
                                       mstate[0][l], mstate[1][l], mstate[2][l], chunk=chunk)
            new_c.append(c_l)
            new_n.append(n_l)
            new_m.append(m_l)
            mixed, w_proj = hn, p["mlstm_w_out"][l]
        else:
            j = l - n_a
            lam_init = 0.8 - 0.6 * math.exp(-0.3 * l)
            dh = p["q_norm_g"].shape[-1]
            q = _q_proj(x, p["norm_g"][l, 0], sh1, sc1, p["attn_w_q"][j], p["q_norm_g"][j],
                        dh ** -0.5 * LOG2E, rows=rows)
            if prompt:
                o = _attn_prompt(q, k_bf, vt_bf, p["rel_bias"], p["attn_lambda"][j], p["attn_head_g"][j],
                                 lam_init)
            else:
                o = _attn_sample(q, past[0], past[1], k_bf, vt_bf, p["rel_bias"], p["attn_lambda"][j],
                                 p["attn_head_g"][j], lam_init)
            mixed, w_proj = o, p["attn_w_o"][j]
        x = _mixer_ffn(x, mixed, w_proj, g1, p["norm_g"][l, 1], sh2, sc2, g2, p["ffn_w_gate"][l],
                       p["ffn_w_up"][l], p["ffn_w_down"][l], rows=rows)
        if l == n_a - 1:
            k_f32, v_f32, k_bf, vt_bf = _shared_kv(x, p["kv_norm_g"], p["w_k"], p["w_v"], p["k_norm_g"],
                                                   rows=rows, attn_tile=ATTN_TILE if prompt else t)
    dh = p["k_norm_g"].shape[-1]
    return (x, jnp.stack(new_c), jnp.stack(new_n), jnp.stack(new_m),
            k_f32.reshape(b, t, A_HEADS, 2, dh), v_f32.reshape(b, t, A_HEADS, 2 * dh))


def kernel(x_prompt, x_sample, c_prompt, c_sample, state_mlstm_C, state_mlstm_n, state_mlstm_m, cache_k, cache_v, ada_w, ada_b, norm_g, mlstm_w_in, mlstm_b_gates, mlstm_head_g, mlstm_w_out, kv_norm_g, w_k, w_v, k_norm_g, attn_w_q, q_norm_g, attn_lambda, attn_head_g, attn_w_o, rel_bias, ffn_w_gate, ffn_w_up, ffn_w_down):
    bf = lambda w: w.astype(BF16)
    p = dict(norm_g=norm_g, mlstm_w_in=bf(mlstm_w_in), mlstm_b_gates=mlstm_b_gates,
             mlstm_head_g=mlstm_head_g, mlstm_w_out=bf(mlstm_w_out), kv_norm_g=kv_norm_g, w_k=bf(w_k),
             w_v=bf(w_v), k_norm_g=k_norm_g, attn_w_q=bf(attn_w_q), q_norm_g=q_norm_g,
             attn_lambda=attn_lambda, attn_head_g=attn_head_g, attn_w_o=bf(attn_w_o), rel_bias=rel_bias,
             ffn_w_gate=bf(ffn_w_gate), ffn_w_up=bf(ffn_w_up), ffn_w_down=bf(ffn_w_down))
    bp = x_prompt.shape[0]
    n_a, _, heads, dk, dv = state_mlstm_C.shape
    mod = _adaln(jnp.concatenate([c_prompt, c_sample], axis=0), ada_w, ada_b)
    zero_state = (jnp.zeros((n_a, bp, heads, dk, dv), F32), jnp.zeros((n_a, bp, heads, dk), F32),
                  jnp.zeros((n_a, bp, heads), F32))
    y_p, c_p, n_p, m_p, k_p, v_p = _trunk(x_prompt, mod[:, :bp], zero_state, None, MLSTM_PROMPT_CHUNK,
                                          TOKEN_ROWS, p)
    y_s, c_s, n_s, m_s, k_s, v_s = _trunk(x_sample, mod[:, bp:], (state_mlstm_C, state_mlstm_n, state_mlstm_m),
                                          _cache_layouts(cache_k, cache_v, CACHE_TILE), x_sample.shape[1],
                                          TOKEN_ROWS, p)
    return (y_p, y_s, c_p, n_p, m_p, k_p, v_p, c_s, n_s, m_s, k_s, v_s)
```

```python
import functools
import math

import jax
import jax.numpy as jnp
from jax import lax
from jax.experimental import pallas as pl
from jax.experimental.pallas import tpu as pltpu

F32 = jnp.float32
BF16 = jnp.bfloat16

EPS = 1e-6
NEG_BIG = -1e30
CHUNK = 64
N_BUCKETS = 32
MAX_DISTANCE = 128
M_HEADS = 8
A_HEADS = 8
GATE_PAD = 128
LANES = 128
SUBLANES = 8
ONES_ROWS = 16
LOG2E = math.log2(math.e)

V7X_VMEM_BYTES = 64 * 1024 * 1024
VMEM_LIMIT = V7X_VMEM_BYTES - 8 * 1024 * 1024

ATTN_TILE = 256
FAR_TILES = 2
CACHE_TILE = 512
MLSTM_PROMPT_CHUNK = 256
FFN_CHUNK = 256
ROW_CHUNK = 128
TOKEN_ROWS = 512
ADALN_TILE = 1536


def _params(*sem):
    return pltpu.CompilerParams(dimension_semantics=sem, vmem_limit_bytes=VMEM_LIMIT)


def _rms(x, g):
    return x * lax.rsqrt(jnp.mean(x * x, axis=-1, keepdims=True) + EPS) * g


def _log_sigmoid(x):
    return jnp.minimum(x, 0.0) - jnp.log1p(jnp.exp(-jnp.abs(x)))


def _dot(a, b):
    return jnp.dot(a, b, preferred_element_type=F32)


def _dot_nt(a, b):
    return lax.dot_general(a, b, (((1,), (1,)), ((), ())), preferred_element_type=F32)


def _resident(shape):
    nd = len(shape)
    return pl.BlockSpec(shape, lambda *_: (0,) * nd, pipeline_mode=pl.Buffered(1))


def _adaln_kernel(c_ref, w_ref, b_ref, o_ref):
    c = c_ref[...]
    a = (c * jax.nn.sigmoid(c)).astype(BF16)
    o_ref[0] = _dot(a, w_ref[0].astype(BF16)) + b_ref[0]


def _adaln(c, ada_w, ada_b):
    depth, d, n = ada_w.shape
    rows = c.shape[0]
    tn = ADALN_TILE
    assert n % tn == 0
    return pl.pallas_call(
        _adaln_kernel,
        grid=(depth, n // tn),
        in_specs=[
            pl.BlockSpec((rows, d), lambda l, j: (0, 0)),
            pl.BlockSpec((1, d, tn), lambda l, j: (l, 0, j)),
            pl.BlockSpec((1, 1, tn), lambda l, j: (l, 0, j)),
        ],
        out_specs=pl.BlockSpec((1, rows, tn), lambda l, j: (l, 0, j)),
        out_shape=jax.ShapeDtypeStruct((depth, rows, n), F32),
        compiler_params=_params("parallel", "parallel"),
        name="adaln",
    )(c, ada_w, ada_b.reshape(depth, 1, n))


def _modulated(x_ref, g_ref, sh_ref, sc_ref):
    x = x_ref[...]
    bb, tt, d = x.shape
    xm = _rms(x, g_ref[...]) * (1.0 + sc_ref[...]) + sh_ref[...]
    return xm.reshape(bb * tt, d).astype(BF16)


def _group_rms_feature_major(yt, gain_col):
    width = gain_col.shape[0]
    groups = []
    for g in range(yt.shape[0] // width):
        seg = yt[g * width:(g + 1) * width]
        groups.append(seg * lax.rsqrt(jnp.mean(seg * seg, axis=0, keepdims=True) + EPS) * gain_col)
    return groups


def _store_values_ext(dst_ref, lead, vt, col, width, heads):
    dv = vt.shape[0] // heads
    ext = dv + ONES_ROWS
    ones = jnp.ones((ONES_ROWS, width), BF16)
    for hh in range(heads):
        dst_ref[lead + (slice(hh * ext, hh * ext + dv), slice(None))] = vt[hh * dv:(hh + 1) * dv, col:col + width]
        dst_ref[lead + (slice(hh * ext + dv, (hh + 1) * ext), slice(None))] = ones


def _mlstm_proj_kernel(x_ref, g_ref, sh_ref, sc_ref, w_ref, wf_ref, wg_ref, wgt_ref,
                       o_ref, of_ref, og_ref, ogt_ref, *, n_chunk):
    bb, tt, n = o_ref.shape
    xm = _modulated(x_ref, g_ref, sh_ref, sc_ref)
    for c in range(n // n_chunk):
        y = _dot(xm, w_ref[:, c * n_chunk:(c + 1) * n_chunk])
        o_ref[:, :, c * n_chunk:(c + 1) * n_chunk] = y.reshape(bb, tt, n_chunk).astype(o_ref.dtype)
    vt = _dot_nt(wf_ref[...], xm).astype(BF16)
    for i in range(bb):
        _store_values_ext(of_ref, (i,), vt, i * tt, tt, M_HEADS)
        ogt_ref[i] = _dot_nt(wgt_ref[...], xm[i * tt:(i + 1) * tt])
    og_ref[...] = _dot(xm, wg_ref[...]).reshape(bb, tt, 2 * GATE_PAD)


def _q_proj_kernel(x_ref, g_ref, sh_ref, sc_ref, wt_ref, qg_ref, o_ref, *, qk_scale):
    bb, n, tt = o_ref.shape
    xm = _modulated(x_ref, g_ref, sh_ref, sc_ref)
    groups = _group_rms_feature_major(_dot_nt(wt_ref[...], xm), qg_ref[...])
    width = qg_ref.shape[0]
    for g, seg in enumerate(groups):
        seg = (seg * qk_scale).astype(o_ref.dtype)
        for i in range(bb):
            o_ref[i, g * width:(g + 1) * width, :] = seg[:, i * tt:(i + 1) * tt]


def _row_blocks(b, t, rows):
    if t >= rows:
        assert t % rows == 0
        return 1, rows
    assert rows % t == 0 and b % (rows // t) == 0
    return rows // t, t


def _token_specs(x, rows):
    b, t, d = x.shape
    bb, tt = _row_blocks(b, t, rows)
    per_b = pl.BlockSpec((bb, 1, d), lambda i, j: (i, 0, 0))
    return bb, tt, (b // bb, t // tt), pl.BlockSpec((bb, tt, d), lambda i, j: (i, j, 0)), per_b


def _mlstm_proj(x, gain, shift, scale, w, w_feat, w_gate, *, rows):
    b, t, d = x.shape
    n = w.shape[1]
    bb, tt, grid, x_spec, per_b = _token_specs(x, rows)
    tok = lambda i, j: (i, j, 0)
    feat = lambda i, j: (i, 0, j)
    ng = w_gate.shape[1]
    half = ng // 2
    wg = (jnp.zeros((d, 2 * GATE_PAD), BF16).at[:, :half].set(w_gate[:, :half])
          .at[:, GATE_PAD:GATE_PAD + half].set(w_gate[:, half:]))
    n_feat = w_feat.shape[1]
    d_ext = n_feat + M_HEADS * ONES_ROWS
    return pl.pallas_call(
        functools.partial(_mlstm_proj_kernel, n_chunk=min(n, 1024)),
        grid=grid,
        in_specs=[x_spec, _resident((1, d)), per_b, per_b, _resident((d, n)),
                  _resident((n_feat, d)), _resident((d, 2 * GATE_PAD)), _resident((ng, d))],
        out_specs=[pl.BlockSpec((bb, tt, n), tok), pl.BlockSpec((bb, d_ext, tt), feat),
                   pl.BlockSpec((bb, tt, 2 * GATE_PAD), tok), pl.BlockSpec((bb, ng, tt), feat)],
        out_shape=[jax.ShapeDtypeStruct((b, t, n), BF16), jax.ShapeDtypeStruct((b, d_ext, t), BF16),
                   jax.ShapeDtypeStruct((b, t, 2 * GATE_PAD), F32), jax.ShapeDtypeStruct((b, ng, t), F32)],
        compiler_params=_params("parallel", "parallel"), name="mlstm_proj",
    )(x, gain.reshape(1, d), shift, scale, w, w_feat.T, wg, w_gate.T)


def _q_proj(x, gain, shift, scale, w, qk_gain, qk_scale, *, rows):
    b, t, d = x.shape
    n = w.shape[1]
    bb, tt, grid, x_spec, per_b = _token_specs(x, rows)
    return pl.pallas_call(
        functools.partial(_q_proj_kernel, qk_scale=qk_scale),
        grid=grid,
        in_specs=[x_spec, _resident((1, d)), per_b, per_b, _resident((n, d)), _resident((qk_gain.shape[-1], 1))],
        out_specs=pl.BlockSpec((bb, n, tt), lambda i, j: (i, 0, j)),
        out_shape=jax.ShapeDtypeStruct((b, n, t), BF16),
        compiler_params=_params("parallel", "parallel"), name="q_proj",
    )(x, gain.reshape(1, d), shift, scale, w.T, qk_gain.reshape(-1, 1))


def _kv_kernel(x_ref, g_ref, wkt_ref, wv_ref, wvt_ref, kg_ref, k_ref, v_ref, kb_ref, vt_ref):
    x = x_ref[...]
    bb, tt, d = x.shape
    h = _rms(x, g_ref[...]).reshape(bb * tt, d).astype(BF16)
    groups = _group_rms_feature_major(_dot_nt(wkt_ref[...], h), kg_ref[...])
    per_tile = LANES // kg_ref.shape[0]
    for c in range(d // LANES):
        k_tok = jnp.concatenate(groups[c * per_tile:(c + 1) * per_tile], axis=0).T.reshape(bb, tt, LANES)
        k_ref[:, :, c * LANES:(c + 1) * LANES] = k_tok
        kb_ref[:, :, c * LANES:(c + 1) * LANES] = k_tok.astype(BF16)
    v_ref[...] = _dot(h, wv_ref[...]).reshape(bb, tt, d)
    vt = _dot_nt(wvt_ref[...], h).astype(BF16)
    per, tile = vt_ref.shape[1], vt_ref.shape[3]
    for i in range(bb):
        for j in range(per):
            _store_values_ext(vt_ref, (i, j), vt, (i * per + j) * tile, tile, A_HEADS)


def _shared_kv(x, kv_g, w_k, w_v, k_g, *, rows, attn_tile):
    b, t, d = x.shape
    bb, tt = _row_blocks(b, t, rows)
    assert tt % attn_tile == 0
    per = tt // attn_tile
    d_ext = d + A_HEADS * ONES_ROWS
    tok = lambda i, j: (i, j, 0)
    blk = pl.BlockSpec((bb, tt, d), tok)
    return pl.pallas_call(
        _kv_kernel,
        grid=(b // bb, t // tt),
        in_specs=[blk, _resident((1, d)), _resident((d, d)), _resident((d, d)), _resident((d, d)),
                  _resident((k_g.shape[-1], 1))],
        out_specs=[blk, blk, blk, pl.BlockSpec((bb, per, d_ext, attn_tile), lambda i, j: (i, j, 0, 0))],
        out_shape=[jax.ShapeDtypeStruct((b, t, d), F32), jax.ShapeDtypeStruct((b, t, d), F32),
                   jax.ShapeDtypeStruct((b, t, d), BF16),
                   jax.ShapeDtypeStruct((b, t // attn_tile, d_ext, attn_tile), BF16)],
        compiler_params=_params("parallel", "parallel"),
        name="shared_kv",
    )(x, kv_g.reshape(1, d), w_k.T, w_v, w_v.T, k_g.reshape(-1, 1))


def _mixer_ffn_kernel(x_ref, a_ref, wp_ref, gate1_ref, g_ref, sh_ref, sc_ref, gate2_ref, wg_ref, wu_ref, wd_ref,
                      o_ref, acc_ref):
    bb, tt, d = x_ref.shape
    f = wg_ref.shape[1]
    a = a_ref[...].reshape(bb * tt, a_ref.shape[-1])
    o_ref[...] = x_ref[...] + gate1_ref[...] * _dot(a, wp_ref[...]).reshape(bb, tt, d)
    xm = _modulated(o_ref, g_ref, sh_ref, sc_ref)
    for c in range(f // FFN_CHUNK):
        sl = slice(c * FFN_CHUNK, (c + 1) * FFN_CHUNK)
        hg = _dot(xm, wg_ref[:, sl])
        hu = _dot(xm, wu_ref[:, sl])
        act = (hg * jax.nn.sigmoid(hg) * hu).astype(BF16)
        part = _dot(act, wd_ref[sl, :])
        if c == 0:
            acc_ref[...] = part
        else:
            acc_ref[...] += part
    o_ref[...] += gate2_ref[...] * acc_ref[...].reshape(bb, tt, d)


def _mixer_ffn(x, a, w_proj, gate1, gain, shift, scale, gate2, w_gate, w_up, w_down, *, rows):
    b, t, d = x.shape
    k = a.shape[-1]
    f = w_gate.shape[1]
    assert f % FFN_CHUNK == 0
    bb, tt = _row_blocks(b, t, rows)
    tok = lambda i, j: (i, j, 0)
    per_b = pl.BlockSpec((bb, 1, d), lambda i, j: (i, 0, 0))
    return pl.pallas_call(
        _mixer_ffn_kernel,
        grid=(b // bb, t // tt),
        in_specs=[pl.BlockSpec((bb, tt, d), tok), pl.BlockSpec((bb, tt, k), tok), _resident((k, d)), per_b,
                  _resident((1, d)), per_b, per_b, per_b,
                  _resident((d, f)), _resident((d, f)), _resident((f, d))],
        out_specs=pl.BlockSpec((bb, tt, d), tok),
        out_shape=jax.ShapeDtypeStruct((b, t, d), F32),
        scratch_shapes=[pltpu.VMEM((bb * tt, d), F32)],
        compiler_params=_params("parallel", "parallel"),
        name="mixer_ffn",
    )(x, a, w_proj, gate1, gain.reshape(1, d), shift, scale, gate2, w_gate, w_up, w_down)


def _mlstm_kernel(q_ref, k_ref, o_ref, vt_ref, g_ref, gt_ref, brow_ref, bcol_ref, hg_ref,
                  c0_ref, n0_ref, m0_ref, h_ref, c_out, n_out, m_out,
                  ct_s, m_s, mnew_s, acol_s, br_s, inter_s, wn_s, wold_s, qk_s, qc_s, sb_s, sv_s, stat_s):
    ci = pl.program_id(1)
    length = q_ref.shape[1]
    ext = ct_s.shape[1]
    dv = ext - ONES_ROWS
    dk = q_ref.shape[2] // M_HEADS
    heads = range(M_HEADS)

    @pl.when(ci == 0)
    def _():
        ct_s[...] = jnp.zeros(ct_s.shape, F32)
        for h in heads:
            c0 = jnp.concatenate([c0_ref[0, h], jnp.zeros((dv - dk, dv), F32)], axis=0)
            ct_s[h, :dv, :] = c0.T
            ct_s[h, dv:, :dk] = jnp.broadcast_to(n0_ref[0, h:h + 1, :], (ONES_ROWS, dk))
        m_s[...] = m0_ref[0]

    row = lax.broadcasted_iota(jnp.int32, (length, length), 0)
    col = lax.broadcasted_iota(jnp.int32, (length, length), 1)
    causal = row <= col
    hi = lax.Precision.HIGHEST
    g = g_ref[0] + brow_ref[...]
    b_cols = jnp.dot((col <= row).astype(F32), _log_sigmoid(g[:, GATE_PAD:]), precision=hi,
                     preferred_element_type=F32)
    acol_s[...] = g[:, :GATE_PAD] - b_cols
    gt = gt_ref[0] + bcol_ref[...]
    ig_rows = gt[:M_HEADS]
    br_s[...] = jnp.dot(_log_sigmoid(gt[M_HEADS:]), causal.astype(F32), precision=hi,
                        preferred_element_type=F32)
    b_rows = br_s[...]
    b_end = br_s[:, length - 1:length]
    m_prev = m_s[...]
    g_rows = b_end - b_rows + ig_rows
    m_new = jnp.maximum(b_end + m_prev, jnp.max(g_rows, axis=-1, keepdims=True))
    mnew_s[...] = m_new
    wold_s[...] = jnp.exp(b_end + m_prev - m_new)
    wn_s[...] = jnp.exp(g_rows - m_new)
    inter_s[...] = b_rows + m_prev

    def q_of(h):
        return q_ref[0, :, h * dk:(h + 1) * dk]

    def k_of(h):
        return k_ref[0, :, h * dk:(h + 1) * dk] * (dk ** -0.5)

    def vt_of(h):
        return vt_ref[0, h * ext:(h + 1) * ext, :]

    for h in heads:
        qk_s[h] = _dot_nt(k_of(h), q_of(h))
        qc_s[h] = _dot_nt(ct_s[h, :, :dk].astype(BF16), q_of(h))
    for h in heads:
        inter = inter_s[h:h + 1, :]
        dmat = jnp.where(causal, acol_s[:, h:h + 1] + br_s[h:h + 1, :], -jnp.inf)
        m_t = jnp.maximum(inter, jnp.max(dmat, axis=0, keepdims=True))
        sb_s[h] = (qk_s[h] * jnp.exp(dmat - m_t)).astype(BF16)
        stat_s[0, h] = jnp.exp(inter - m_t)
        stat_s[1, h] = jnp.exp(-m_t)
    for h in heads:
        sv_s[h] = _dot(vt_of(h), sb_s[h])
    for h in heads:
        w_inter = stat_s[0, h]
        tot = w_inter * qc_s[h] + sv_s[h]
        hh = tot[:dv] / jnp.maximum(jnp.abs(tot[dv:dv + 1]), stat_s[1, h])
        hh = hh * lax.rsqrt(jnp.mean(hh * hh, axis=0, keepdims=True) + EPS)
        og = o_ref[0, :, h * dv:(h + 1) * dv].astype(F32)
        h_ref[0, :, h * dv:(h + 1) * dv] = (_to_token_major(hh) * hg_ref[:, h * dv:(h + 1) * dv]
                                            * jax.nn.sigmoid(og)).astype(h_ref.dtype)
    for h in heads:
        vtw = (vt_of(h).astype(F32) * wn_s[h:h + 1, :]).astype(BF16)
        ct_s[h, :, :dk] = wold_s[h:h + 1, :] * ct_s[h, :, :dk] + _dot(vtw, k_of(h))
    m_s[...] = mnew_s[...]

    @pl.when(ci == pl.num_programs(1) - 1)
    def _():
        for h in heads:
            c_out[0, h] = ct_s[h, :dv, :].T[:dk]
            n_out[0, h:h + 1, :] = ct_s[h, dv:dv + 1, :dk]
        m_out[0] = m_s[...]


def _to_token_major(x):
    dv, length = x.shape
    if length % LANES:
        pad = LANES - length % LANES
        x = jnp.concatenate([x, jnp.zeros((dv, pad), x.dtype)], axis=1)
    return x.T[:length]


def _mlstm(qko, vt, g, gt, b_gates, head_g, c0, n0, m0, *, chunk):
    b, t, _ = qko.shape
    heads, dk, dv = c0.shape[1:]
    d = heads * dv
    qk = heads * dk
    ext = dv + ONES_ROWS
    assert qk * 2 == d and t % chunk == 0 and heads == M_HEADS and vt.shape[1] == heads * ext
    nc = t // chunk
    two_h = 2 * heads
    brow = (jnp.zeros((1, 2 * GATE_PAD), F32).at[0, :heads].set(b_gates[:heads])
            .at[0, GATE_PAD:GATE_PAD + heads].set(b_gates[heads:]))
    st4 = lambda i, c: (i, 0, 0, 0)
    st3 = lambda i, c: (i, 0, 0)
    outs = pl.pallas_call(
        _mlstm_kernel,
        grid=(b, nc),
        in_specs=[
            pl.BlockSpec((1, chunk, qk), lambda i, c: (i, c, 0)),
            pl.BlockSpec((1, chunk, qk), lambda i, c: (i, c, 1)),
            pl.BlockSpec((1, chunk, d), lambda i, c: (i, c, 1)),
            pl.BlockSpec((1, heads * ext, chunk), lambda i, c: (i, 0, c)),
            pl.BlockSpec((1, chunk, 2 * GATE_PAD), lambda i, c: (i, c, 0)),
            pl.BlockSpec((1, two_h, chunk), lambda i, c: (i, 0, c)),
            _resident((1, 2 * GATE_PAD)), _resident((two_h, 1)), _resident((1, d)),
            pl.BlockSpec((1, heads, dk, dv), st4),
            pl.BlockSpec((1, heads, dk), st3),
            pl.BlockSpec((1, heads, 1), st3),
        ],
        out_specs=[
            pl.BlockSpec((1, chunk, d), lambda i, c: (i, c, 0)),
            pl.BlockSpec((1, heads, dk, dv), st4),
            pl.BlockSpec((1, heads, dk), st3),
            pl.BlockSpec((1, heads, 1), st3),
        ],
        out_shape=[
            jax.ShapeDtypeStruct((b, t, d), BF16),
            jax.ShapeDtypeStruct((b, heads, dk, dv), F32),
            jax.ShapeDtypeStruct((b, heads, dk), F32),
            jax.ShapeDtypeStruct((b, heads, 1), F32),
        ],
        scratch_shapes=[pltpu.VMEM((heads, ext, dv), F32),
                        pltpu.VMEM((heads, 1), F32), pltpu.VMEM((heads, 1), F32),
                        pltpu.VMEM((chunk, GATE_PAD), F32),
                        pltpu.VMEM((heads, chunk), F32), pltpu.VMEM((heads, chunk), F32),
                        pltpu.VMEM((heads, chunk), F32), pltpu.VMEM((heads, 1), F32),
                        pltpu.VMEM((heads, chunk, chunk), F32),
                        pltpu.VMEM((heads, ext, chunk), F32),
                        pltpu.VMEM((heads, chunk, chunk), BF16),
                        pltpu.VMEM((heads, ext, chunk), F32),
                        pltpu.VMEM((2, heads, 1, chunk), F32)],
        compiler_params=_params("parallel", "arbitrary"),
        name="mlstm",
    )(qko, qko, qko, vt, g, gt, brow, b_gates.reshape(two_h, 1), head_g.reshape(1, d),
      c0, n0, m0.reshape(b, heads, 1))
    hn, c_new, n_new, m_new = outs
    return hn, c_new, n_new, m_new.reshape(b, heads)


def _rel_bucket(rel):
    nb = N_BUCKETS // 2
    max_exact = nb // 2
    ret = jnp.where(rel > 0, nb, 0)
    n = jnp.abs(rel)
    large = max_exact + (jnp.log(jnp.maximum(n, 1).astype(F32) / max_exact)
                         / math.log(MAX_DISTANCE / max_exact) * (nb - max_exact)).astype(jnp.int32)
    large = jnp.minimum(large, nb - 1)
    return ret + jnp.where(n < max_exact, n, large)


def _bias_tile(rel_bias, q0, k0, tq, tk):
    table = rel_bias.astype(F32)
    span = tq + tk
    r = jnp.arange(span)
    r = jnp.where(r < tk, r, r - span)
    vec = (table[_rel_bucket(r + (k0 - q0))] - table[N_BUCKETS // 2 - 1]).T
    bias = jnp.tile(vec, (1, tq))[:, :tq * (span - 1)].reshape(-1, tq, span - 1)[:, :, :tk]
    q_chunk = (q0 + lax.broadcasted_iota(jnp.int32, (tq, tk), 0)) // CHUNK
    k_chunk = (k0 + lax.broadcasted_iota(jnp.int32, (tq, tk), 1)) // CHUNK
    return jnp.where((k_chunk <= q_chunk)[None], bias * LOG2E, NEG_BIG)


def _attn_prompt_kernel(q_ref, k_ref, vt_ref, bias_ref, lam_ref, hg_ref, o_ref,
                        m_s, alpha_s, acc_s, s_scr, p_scr, *, lam_init):
    qi = pl.program_id(1)
    tile = q_ref.shape[2]
    ext = acc_s.shape[1]
    dv = o_ref.shape[2] // A_HEADS
    dh = dv // 2
    n_chain = 2 * A_HEADS
    m_s[...] = jnp.full(m_s.shape, NEG_BIG, F32)
    acc_s[...] = jnp.zeros(acc_s.shape, F32)

    def kv_tiles(j, slots, count):
        rows = count * tile
        koff = pl.multiple_of(j * tile, tile)
        for hc in range(n_chain):
            lo = hc * dh
            s_scr[hc, :rows] = _dot(k_ref[0, pl.ds(koff, rows), lo:lo + dh], q_ref[0, lo:lo + dh, :])
        for hc in range(n_chain):
            m_prev = m_s[hc]
            m_part = None
            for r in range(0, rows, ROW_CHUNK):
                s = s_scr[hc, r:r + ROW_CHUNK]
                if slots is not None:
                    t, rt = divmod(r, tile)
                    s = s + bias_ref[hc // 2, slots[t], rt:rt + ROW_CHUNK]
                    s_scr[hc, r:r + ROW_CHUNK] = s
                part = jnp.max(s.reshape(ROW_CHUNK // SUBLANES, SUBLANES, tile), axis=0)
                m_part = part if m_part is None else jnp.maximum(m_part, part)
            m_new = jnp.maximum(m_prev, jnp.max(m_part, axis=0, keepdims=True))
            alpha_s[hc] = jnp.exp2(m_prev - m_new)
            m_s[hc] = m_new
        for hc in range(n_chain):
            for r in range(0, rows, ROW_CHUNK):
                p_scr[hc, r:r + ROW_CHUNK] = jnp.exp2(s_scr[hc, r:r + ROW_CHUNK] - m_s[hc]).astype(BF16)
        for hc in range(n_chain):
            h = hc // 2
            pv = _dot(vt_ref[0, j, h * ext:(h + 1) * ext, :], p_scr[hc, :tile])
            for t in range(1, count):
                pv = pv + _dot(vt_ref[0, j + t, h * ext:(h + 1) * ext, :], p_scr[hc, t * tile:(t + 1) * tile])
            acc_s[hc] = alpha_s[hc] * acc_s[hc] + pv

    n_far = jnp.maximum(qi - 1, 0)

    def far_body(i, carry):
        kv_tiles(FAR_TILES * i, None, FAR_TILES)
        return carry

    lax.fori_loop(0, n_far // FAR_TILES, far_body, 0)

    def far_rest(j, carry):
        kv_tiles(j, None, 1)
        return carry

    lax.fori_loop(n_far - n_far % FAR_TILES, n_far, far_rest, 0)

    @pl.when(qi > 0)
    def _():
        kv_tiles(qi - 1, (0, 1), 2)

    @pl.when(qi == 0)
    def _():
        kv_tiles(qi, (1,), 1)


    lp = lam_ref[...]
    lam = (jnp.exp(jnp.sum(lp[0:1] * lp[1:2], axis=-1, keepdims=True))
           - jnp.exp(jnp.sum(lp[2:3] * lp[3:4], axis=-1, keepdims=True)) + lam_init)
    for h in range(A_HEADS):
        inv0 = 1.0 / acc_s[2 * h, dv:dv + 1]
        inv1 = 1.0 / acc_s[2 * h + 1, dv:dv + 1]
        o = acc_s[2 * h, :dv] * inv0 - (lam * inv1) * acc_s[2 * h + 1, :dv]
        o = o * lax.rsqrt(jnp.mean(o * o, axis=0, keepdims=True) + EPS) * hg_ref[...] * (1.0 - lam_init)
        o_ref[0, :, h * dv:(h + 1) * dv] = o.T.astype(o_ref.dtype)


def _attn_prompt(q, k, vt, rel_bias, lam_p, head_g, lam_init):
    b, d, t = q.shape
    tile = ATTN_TILE
    assert t % tile == 0 and tile % CHUNK == 0 and tile >= MAX_DISTANCE and vt.shape[-1] == tile
    n = t // tile
    dv = d // A_HEADS
    ext = vt.shape[2] // A_HEADS
    n_chain = 2 * A_HEADS
    bias = jnp.stack([_bias_tile(rel_bias, tile, 0, tile, tile), _bias_tile(rel_bias, 0, 0, tile, tile)],
                     axis=1).swapaxes(-1, -2)
    return pl.pallas_call(
        functools.partial(_attn_prompt_kernel, lam_init=lam_init),
        grid=(b, n),
        in_specs=[pl.BlockSpec((1, d, tile), lambda i, qi: (i, 0, qi)),
                  pl.BlockSpec((1, t, d), lambda i, qi: (i, 0, 0), pipeline_mode=pl.Buffered(1)),
                  pl.BlockSpec((1,) + vt.shape[1:], lambda i, qi: (i, 0, 0, 0), pipeline_mode=pl.Buffered(1)),
                  _resident(bias.shape), _resident(lam_p.shape), _resident((dv, 1))],
        out_specs=pl.BlockSpec((1, tile, d), lambda i, qi: (i, qi, 0)),
        out_shape=jax.ShapeDtypeStruct((b, t, d), BF16),
        scratch_shapes=[pltpu.VMEM((n_chain, 1, tile), F32), pltpu.VMEM((n_chain, 1, tile), F32),
                        pltpu.VMEM((n_chain, ext, tile), F32),
                        pltpu.VMEM((n_chain, FAR_TILES * tile, tile), F32),
                        pltpu.VMEM((n_chain, FAR_TILES * tile, tile), BF16)],
        compiler_params=_params("parallel", "arbitrary"),
        name="attn_prompt",
    )(q, k, vt, bias, lam_p, head_g.reshape(dv, 1))


def _attn_sample_kernel(q_ref, kc_ref, vtc_ref, kn_ref, vtn_ref, bias_c_ref, bias_n_ref, lam_ref, hg_ref,
                        o_ref, qbd_s, m_s, alpha_s, acc_s, s_scr, p_scr, *, lam_init):
    tq = q_ref.shape[2]
    tile = vtc_ref.shape[3]
    n_cache = vtc_ref.shape[1]
    ext = acc_s.shape[1]
    dv = o_ref.shape[2] // A_HEADS
    gw = qbd_s.shape[1]
    n_group = qbd_s.shape[0]
    n_sub = gw // tq
    row_blk = lax.broadcasted_iota(jnp.int32, (gw, gw), 0) // (gw // n_sub)
    col_blk = lax.broadcasted_iota(jnp.int32, (gw, gw), 1) // tq
    for g in range(n_group):
        qg = q_ref[0, g * gw:(g + 1) * gw, :]
        qrep = jnp.concatenate([qg] * n_sub, axis=1)
        qbd_s[g] = jnp.where(row_blk == col_blk, qrep, jnp.zeros_like(qrep))
    m_s[...] = jnp.full(m_s.shape, NEG_BIG, F32)
    acc_s[...] = jnp.zeros(acc_s.shape, F32)

    def kv_tile(k_tile, vt_tile, bias_ref, rows):
        for g in range(n_group):
            s_scr[g, :rows] = _dot(k_tile(g), qbd_s[g])
        for g in range(n_group):
            s = s_scr[g, :rows]
            if bias_ref is not None:
                s = s + bias_ref[g]
                s_scr[g, :rows] = s
            m_prev = m_s[g]
            m_new = jnp.maximum(m_prev, jnp.max(s, axis=0, keepdims=True))
            alpha_s[g] = jnp.exp2(m_prev - m_new)
            m_s[g] = m_new
        step = min(ROW_CHUNK, rows)
        for g in range(n_group):
            for r in range(0, rows, step):
                p_scr[g, r:r + step] = jnp.exp2(s_scr[g, r:r + step] - m_s[g]).astype(BF16)
        for h in range(A_HEADS):
            g, half = divmod(h, 2)
            lanes = slice(half * 2 * tq, (half + 1) * 2 * tq)
            acc_s[h] = alpha_s[g, :, lanes] * acc_s[h] + _dot(vt_tile(h), p_scr[g, :rows, lanes])

    ones = jnp.ones((ext - dv, tile), BF16)

    def cache_tile(j, bias_ref):
        koff = pl.multiple_of(j * tile, tile)
        kv_tile(lambda g: kc_ref[0, pl.ds(koff, tile), g * gw:(g + 1) * gw],
                lambda h: jnp.concatenate([vtc_ref[0, j, h * dv:(h + 1) * dv, :], ones], axis=0),
                bias_ref, tile)

    def far_body(j, carry):
        cache_tile(j, None)
        return carry

    lax.fori_loop(0, n_cache - 1, far_body, 0)
    cache_tile(n_cache - 1, bias_c_ref)
    kv_tile(lambda g: kn_ref[0, :, g * gw:(g + 1) * gw],
            lambda h: vtn_ref[0, 0, h * ext:(h + 1) * ext, :], bias_n_ref, tq)

    lp = lam_ref[...]
    lam = (jnp.exp(jnp.sum(lp[0:1] * lp[1:2], axis=-1, keepdims=True))
           - jnp.exp(jnp.sum(lp[2:3] * lp[3:4], axis=-1, keepdims=True)) + lam_init)
    for h in range(A_HEADS):
        both = (acc_s[h, :dv] * (1.0 / acc_s[h, dv:dv + 1])).T
        o = both[:tq] - lam * both[tq:]
        o = _rms(o, hg_ref[...]) * (1.0 - lam_init)
        o_ref[0, :, h * dv:(h + 1) * dv] = o.astype(o_ref.dtype)


def _attn_sample(q, cache_k, cache_vt, new_k, new_vt, rel_bias, lam_p, head_g, lam_init):
    b, d, t = q.shape
    past = cache_k.shape[1]
    n_cache, _, tile = cache_vt.shape[1:]
    d_ext = new_vt.shape[2]
    assert past == n_cache * tile and past % CHUNK == 0 and tile >= MAX_DISTANCE
    dv = d // A_HEADS
    ext = d_ext // A_HEADS
    gw = 4 * t
    assert gw == 2 * dv and A_HEADS % 2 == 0
    n_group = A_HEADS // 2

    def packed(bias):
        bt = bias.swapaxes(-1, -2).reshape(n_group, 2, -1, t)
        return jnp.concatenate([bt[:, 0], bt[:, 0], bt[:, 1], bt[:, 1]], axis=-1)

    bias_c = packed(_bias_tile(rel_bias, past, past - tile, t, tile))
    bias_n = packed(_bias_tile(rel_bias, past, past, t, t))
    tok_spec = pl.BlockSpec((1, t, d), lambda i: (i, 0, 0))
    return pl.pallas_call(
        functools.partial(_attn_sample_kernel, lam_init=lam_init),
        grid=(b,),
        in_specs=[pl.BlockSpec((1, d, t), lambda i: (i, 0, 0)),
                  pl.BlockSpec((1, past, d), lambda i: (i, 0, 0)),
                  pl.BlockSpec((1, n_cache, d, tile), lambda i: (i, 0, 0, 0)),
                  tok_spec,
                  pl.BlockSpec((1, 1, d_ext, t), lambda i: (i, 0, 0, 0)),
                  _resident(bias_c.shape), _resident(bias_n.shape), _resident(lam_p.shape),
                  _resident((1, dv))],
        out_specs=tok_spec,
        out_shape=jax.ShapeDtypeStruct((b, t, d), BF16),
        scratch_shapes=[pltpu.VMEM((n_group, gw, gw), BF16),
                        pltpu.VMEM((n_group, 1, gw), F32), pltpu.VMEM((n_group, 1, gw), F32),
                        pltpu.VMEM((A_HEADS, ext, 2 * t), F32),
                        pltpu.VMEM((n_group, tile, gw), F32), pltpu.VMEM((n_group, tile, gw), BF16)],
        compiler_params=_params("parallel"),
        name="attn_sample",
    )(q, cache_k, cache_vt, new_k, new_vt, bias_c, bias_n, lam_p, head_g.reshape(1, dv))


def _cache_layouts(cache_k, cache_v, tile):
    b, past, heads, two, dh = cache_k.shape
    dv = two * dh
    d = heads * dv
    n = past // tile
    per = math.gcd(n, 4)
    span = per * tile
    k = pl.pallas_call(
        _keys_to_token_major_kernel,
        grid=(b, past // span),
        in_specs=[pl.BlockSpec((1, d, span), lambda i, j: (i, 0, j))],
        out_specs=pl.BlockSpec((1, span, d), lambda i, j: (i, j, 0)),
        out_shape=jax.ShapeDtypeStruct((b, past, d), BF16),
        compiler_params=_params("parallel", "parallel"),
        name="cache_keys_t",
    )(cache_k.transpose(0, 2, 3, 4, 1).reshape(b, d, past))
    vt = pl.pallas_call(
        functools.partial(_values_to_feature_major_kernel, heads=heads),
        grid=(b, n // per),
        in_specs=[pl.BlockSpec((1, span * heads, dv), lambda i, j: (i, j, 0))],
        out_specs=pl.BlockSpec((1, per, d, tile), lambda i, j: (i, j, 0, 0)),
        out_shape=jax.ShapeDtypeStruct((b, n, d, tile), BF16),
        compiler_params=_params("parallel", "parallel"),
        name="cache_values_t",
    )(cache_v.reshape(b, past * heads, dv))
    return k, vt


def _keys_to_token_major_kernel(k_ref, o_ref):
    for c in range(k_ref.shape[1] // LANES):
        o_ref[0, :, c * LANES:(c + 1) * LANES] = k_ref[0, c * LANES:(c + 1) * LANES, :].T.astype(BF16)


def _values_to_feature_major_kernel(v_ref, o_ref, *, heads):
    per, tile = o_ref.shape[1], o_ref.shape[3]
    dv = v_ref.shape[2]
    for j in range(per):
        for h in range(heads):
            rows = v_ref[0, pl.ds(j * tile * heads + h, tile, stride=heads), :]
            o_ref[0, j, h * dv:(h + 1) * dv, :] = rows.T.astype(BF16)


def _trunk(x, mod, mstate, past, chunk, rows, p):
    b, t, d = x.shape
    depth = p["norm_g"].shape[0]
    n_a = p["mlstm_w_in"].shape[0]
    qk = M_HEADS * (d // M_HEADS // 2)
    new_c, new_n, new_m = [], [], []
    prompt = past is None
    k_f32 = v_f32 = k_bf = vt_bf = None
    for l in range(depth):
        sh1, sc1, g1, sh2, sc2, g2 = [mod[l, :, i * d:(i + 1) * d].reshape(b, 1, d) for i in range(6)]
        if l < n_a:
            w_in = p["mlstm_w_in"][l]
            w_qko = jnp.concatenate([w_in[:, :2 * qk], w_in[:, 2 * qk + d:2 * qk + 2 * d]], axis=1)
            qko, vt, g, gt = _mlstm_proj(x, p["norm_g"][l, 0], sh1, sc1, w_qko, w_in[:, 2 * qk:2 * qk + d],
                                         w_in[:, 2 * qk + 2 * d:], rows=rows)
            hn, c_l, n_l, m_l = _mlstm(qko, vt, g, gt, p["mlstm_b_gates"][l], p["mlstm_head_g"][l],
                                       mstate[0][l], mstate[1][l], mstate[2][l], chunk=chunk)
            new_c.append(c_l)
            new_n.append(n_l)
            new_m.append(m_l)
            mixed, w_proj = hn, p["mlstm_w_out"][l]
        else:
            j = l - n_a
            lam_init = 0.8 - 0.6 * math.exp(-0.3 * l)
            dh = p["q_norm_g"].shape[-1]
            q = _q_proj(x, p["norm_g"][l, 0], sh1, sc1, p["attn_w_q"][j], p["q_norm_g"][j],
                        dh ** -0.5 * LOG2E, rows=rows)
            if prompt:
                o = _attn_prompt(q, k_bf, vt_bf, p["rel_bias"], p["attn_lambda"][j], p["attn_head_g"][j],
                                 lam_init)
            else:
                o = _attn_sample(q, past[0], past[1], k_bf, vt_bf, p["rel_bias"], p["attn_lambda"][j],
                                 p["attn_head_g"][j], lam_init)
            mixed, w_proj = o, p["attn_w_o"][j]
        x = _mixer_ffn(x, mixed, w_proj, g1, p["norm_g"][l, 1], sh2, sc2, g2, p["ffn_w_gate"][l],
                       p["ffn_w_up"][l], p["ffn_w_down"][l], rows=rows)
        if l == n_a - 1:
            k_f32, v_f32, k_bf, vt_bf = _shared_kv(x, p["kv_norm_g"], p["w_k"], p["w_v"], p["k_norm_g"],
                                                   rows=rows, attn_tile=ATTN_TILE if prompt else t)
    dh = p["k_norm_g"].shape[-1]
    return (x, jnp.stack(new_c), jnp.stack(new_n), jnp.stack(new_m),
            k_f32.reshape(b, t, A_HEADS, 2, dh), v_f32.reshape(b, t, A_HEADS, 2 * dh))


def kernel(x_prompt, x_sample, c_prompt, c_sample, state_mlstm_C, state_mlstm_n, state_mlstm_m, cache_k, cache_v, ada_w, ada_b, norm_g, mlstm_w_in, mlstm_b_gates, mlstm_head_g, mlstm_w_out, kv_norm_g, w_k, w_v, k_norm_g, attn_w_q, q_norm_g, attn_lambda, attn_head_g, attn_w_o, rel_bias, ffn_w_gate, ffn_w_up, ffn_w_down):
    bf = lambda w: w.astype(BF16)
    p = dict(norm_g=norm_g, mlstm_w_in=bf(mlstm_w_in), mlstm_b_gates=mlstm_b_gates,
             mlstm_head_g=mlstm_head_g, mlstm_w_out=bf(mlstm_w_out), kv_norm_g=kv_norm_g, w_k=bf(w_k),
             w_v=bf(w_v), k_norm_g=k_norm_g, attn_w_q=bf(attn_w_q), q_norm_g=q_norm_g,
             attn_lambda=attn_lambda, attn_head_g=attn_head_g, attn_w_o=bf(attn_w_o), rel_bias=rel_bias,
             ffn_w_gate=bf(ffn_w_gate), ffn_w_up=bf(ffn_w_up), ffn_w_down=bf(ffn_w_down))
    bp = x_prompt.shape[0]
    n_a, _, heads, dk, dv = state_mlstm_C.shape
    mod = _adaln(jnp.concatenate([c_prompt, c_sample], axis=0), ada_w, ada_b)
    zero_state = (jnp.zeros((n_a, bp, heads, dk, dv), F32), jnp.zeros((n_a, bp, heads, dk), F32),
                  jnp.zeros((n_a, bp, heads), F32))
    y_p, c_p, n_p, m_p, k_p, v_p = _trunk(x_prompt, mod[:, :bp], zero_state, None, MLSTM_PROMPT_CHUNK,
                                          TOKEN_ROWS, p)
    y_s, c_s, n_s, m_s, k_s, v_s = _trunk(x_sample, mod[:, bp:], (state_mlstm_C, state_mlstm_n, state_mlstm_m),
                                          _cache_layouts(cache_k, cache_v, CACHE_TILE), x_sample.shape[1],
                                          TOKEN_ROWS, p)
    return (y_p, y_s, c_p, n_p, m_p, k_p, v_p, c_s, n_s, m_s, k_s, v_s)
```

```python
import functools
import math

import jax
import jax.numpy as jnp
from jax import lax
from jax.experimental import pallas as pl
from jax.experimental.pallas import tpu as pltpu

F32 = jnp.float32
BF16 = jnp.bfloat16

EPS = 1e-6
NEG_BIG = -1e30
CHUNK = 64
N_BUCKETS = 32
MAX_DISTANCE = 128
M_HEADS = 8
A_HEADS = 8
GATE_PAD = 128
LANES = 128
SUBLANES = 8
ONES_ROWS = 16
LOG2E = math.log2(math.e)

V7X_VMEM_BYTES = 64 * 1024 * 1024
VMEM_LIMIT = V7X_VMEM_BYTES - 8 * 1024 * 1024

ATTN_TILE = 256
FAR_TILES = 2
CACHE_TILE = 512
KEY_RING = 3
MLSTM_PROMPT_CHUNK = 256
FFN_CHUNK = 256
ROW_CHUNK = 128
TOKEN_ROWS = 512
ADALN_TILE = 1536


def _params(*sem):
    return pltpu.CompilerParams(dimension_semantics=sem, vmem_limit_bytes=VMEM_LIMIT)


def _rms(x, g):
    return x * lax.rsqrt(jnp.mean(x * x, axis=-1, keepdims=True) + EPS) * g


def _log_sigmoid(x):
    return jnp.minimum(x, 0.0) - jnp.log1p(jnp.exp(-jnp.abs(x)))


def _dot(a, b):
    return jnp.dot(a, b, preferred_element_type=F32)


def _dot_nt(a, b):
    return lax.dot_general(a, b, (((1,), (1,)), ((), ())), preferred_element_type=F32)


def _resident(shape):
    nd = len(shape)
    return pl.BlockSpec(shape, lambda *_: (0,) * nd, pipeline_mode=pl.Buffered(1))


def _adaln_kernel(c_ref, w_ref, b_ref, o_ref):
    c = c_ref[...]
    a = (c * jax.nn.sigmoid(c)).astype(BF16)
    o_ref[0] = _dot(a, w_ref[0].astype(BF16)) + b_ref[0]


def _adaln(c, ada_w, ada_b):
    depth, d, n = ada_w.shape
    rows = c.shape[0]
    tn = ADALN_TILE
    assert n % tn == 0
    return pl.pallas_call(
        _adaln_kernel,
        grid=(depth, n // tn),
        in_specs=[
            pl.BlockSpec((rows, d), lambda l, j: (0, 0)),
            pl.BlockSpec((1, d, tn), lambda l, j: (l, 0, j)),
            pl.BlockSpec((1, 1, tn), lambda l, j: (l, 0, j)),
        ],
        out_specs=pl.BlockSpec((1, rows, tn), lambda l, j: (l, 0, j)),
        out_shape=jax.ShapeDtypeStruct((depth, rows, n), F32),
        compiler_params=_params("parallel", "parallel"),
        name="adaln",
    )(c, ada_w, ada_b.reshape(depth, 1, n))


def _modulated(x_ref, g_ref, sh_ref, sc_ref):
    x = x_ref[...]
    bb, tt, d = x.shape
    xm = _rms(x, g_ref[...]) * (1.0 + sc_ref[...]) + sh_ref[...]
    return xm.reshape(bb * tt, d).astype(BF16)


def _group_rms_feature_major(yt, gain_col):
    width = gain_col.shape[0]
    groups = []
    for g in range(yt.shape[0] // width):
        seg = yt[g * width:(g + 1) * width]
        groups.append(seg * lax.rsqrt(jnp.mean(seg * seg, axis=0, keepdims=True) + EPS) * gain_col)
    return groups


def _store_values_ext(dst_ref, lead, vt, col, width, heads):
    dv = vt.shape[0] // heads
    ext = dv + ONES_ROWS
    ones = jnp.ones((ONES_ROWS, width), BF16)
    for hh in range(heads):
        dst_ref[lead + (slice(hh * ext, hh * ext + dv), slice(None))] = vt[hh * dv:(hh + 1) * dv, col:col + width]
        dst_ref[lead + (slice(hh * ext + dv, (hh + 1) * ext), slice(None))] = ones


def _mlstm_proj_kernel(x_ref, g_ref, sh_ref, sc_ref, w_ref, wf_ref, wg_ref, wgt_ref,
                       o_ref, of_ref, og_ref, ogt_ref, *, n_chunk):
    bb, tt, n = o_ref.shape
    xm = _modulated(x_ref, g_ref, sh_ref, sc_ref)
    for c in range(n // n_chunk):
        y = _dot(xm, w_ref[:, c * n_chunk:(c + 1) * n_chunk])
        o_ref[:, :, c * n_chunk:(c + 1) * n_chunk] = y.reshape(bb, tt, n_chunk).astype(o_ref.dtype)
    vt = _dot_nt(wf_ref[...], xm).astype(BF16)
    for i in range(bb):
        _store_values_ext(of_ref, (i,), vt, i * tt, tt, M_HEADS)
        ogt_ref[i] = _dot_nt(wgt_ref[...], xm[i * tt:(i + 1) * tt])
    og_ref[...] = _dot(xm, wg_ref[...]).reshape(bb, tt, 2 * GATE_PAD)


def _q_proj_kernel(x_ref, g_ref, sh_ref, sc_ref, wt_ref, qg_ref, o_ref, *, qk_scale):
    bb, n, tt = o_ref.shape
    xm = _modulated(x_ref, g_ref, sh_ref, sc_ref)
    groups = _group_rms_feature_major(_dot_nt(wt_ref[...], xm), qg_ref[...])
    width = qg_ref.shape[0]
    for g, seg in enumerate(groups):
        seg = (seg * qk_scale).astype(o_ref.dtype)
        for i in range(bb):
            o_ref[i, g * width:(g + 1) * width, :] = seg[:, i * tt:(i + 1) * tt]


def _row_blocks(b, t, rows):
    if t >= rows:
        assert t % rows == 0
        return 1, rows
    assert rows % t == 0 and b % (rows // t) == 0
    return rows // t, t


def _token_specs(x, rows):
    b, t, d = x.shape
    bb, tt = _row_blocks(b, t, rows)
    per_b = pl.BlockSpec((bb, 1, d), lambda i, j: (i, 0, 0))
    return bb, tt, (b // bb, t // tt), pl.BlockSpec((bb, tt, d), lambda i, j: (i, j, 0)), per_b


def _mlstm_proj(x, gain, shift, scale, w, w_feat, w_gate, *, rows):
    b, t, d = x.shape
    n = w.shape[1]
    bb, tt, grid, x_spec, per_b = _token_specs(x, rows)
    tok = lambda i, j: (i, j, 0)
    feat = lambda i, j: (i, 0, j)
    ng = w_gate.shape[1]
    half = ng // 2
    wg = (jnp.zeros((d, 2 * GATE_PAD), BF16).at[:, :half].set(w_gate[:, :half])
          .at[:, GATE_PAD:GATE_PAD + half].set(w_gate[:, half:]))
    n_feat = w_feat.shape[1]
    d_ext = n_feat + M_HEADS * ONES_ROWS
    return pl.pallas_call(
        functools.partial(_mlstm_proj_kernel, n_chunk=min(n, 1024)),
        grid=grid,
        in_specs=[x_spec, _resident((1, d)), per_b, per_b, _resident((d, n)),
                  _resident((n_feat, d)), _resident((d, 2 * GATE_PAD)), _resident((ng, d))],
        out_specs=[pl.BlockSpec((bb, tt, n), tok), pl.BlockSpec((bb, d_ext, tt), feat),
                   pl.BlockSpec((bb, tt, 2 * GATE_PAD), tok), pl.BlockSpec((bb, ng, tt), feat)],
        out_shape=[jax.ShapeDtypeStruct((b, t, n), BF16), jax.ShapeDtypeStruct((b, d_ext, t), BF16),
                   jax.ShapeDtypeStruct((b, t, 2 * GATE_PAD), F32), jax.ShapeDtypeStruct((b, ng, t), F32)],
        compiler_params=_params("parallel", "parallel"), name="mlstm_proj",
    )(x, gain.reshape(1, d), shift, scale, w, w_feat.T, wg, w_gate.T)


def _q_proj(x, gain, shift, scale, w, qk_gain, qk_scale, *, rows):
    b, t, d = x.shape
    n = w.shape[1]
    bb, tt, grid, x_spec, per_b = _token_specs(x, rows)
    return pl.pallas_call(
        functools.partial(_q_proj_kernel, qk_scale=qk_scale),
        grid=grid,
        in_specs=[x_spec, _resident((1, d)), per_b, per_b, _resident((n, d)), _resident((qk_gain.shape[-1], 1))],
        out_specs=pl.BlockSpec((bb, n, tt), lambda i, j: (i, 0, j)),
        out_shape=jax.ShapeDtypeStruct((b, n, t), BF16),
        compiler_params=_params("parallel", "parallel"), name="q_proj",
    )(x, gain.reshape(1, d), shift, scale, w.T, qk_gain.reshape(-1, 1))


def _kv_kernel(x_ref, g_ref, wkt_ref, wv_ref, wvt_ref, kg_ref, k_ref, v_ref, kb_ref, vt_ref):
    x = x_ref[...]
    bb, tt, d = x.shape
    h = _rms(x, g_ref[...]).reshape(bb * tt, d).astype(BF16)
    groups = _group_rms_feature_major(_dot_nt(wkt_ref[...], h), kg_ref[...])
    per_tile = LANES // kg_ref.shape[0]
    for c in range(d // LANES):
        k_tok = jnp.concatenate(groups[c * per_tile:(c + 1) * per_tile], axis=0).T.reshape(bb, tt, LANES)
        k_ref[:, :, c * LANES:(c + 1) * LANES] = k_tok
        kb_ref[:, :, c * LANES:(c + 1) * LANES] = k_tok.astype(BF16)
    v_ref[...] = _dot(h, wv_ref[...]).reshape(bb, tt, d)
    vt = _dot_nt(wvt_ref[...], h).astype(BF16)
    per, tile = vt_ref.shape[1], vt_ref.shape[3]
    for i in range(bb):
        for j in range(per):
            _store_values_ext(vt_ref, (i, j), vt, (i * per + j) * tile, tile, A_HEADS)


def _shared_kv(x, kv_g, w_k, w_v, k_g, *, rows, attn_tile):
    b, t, d = x.shape
    bb, tt = _row_blocks(b, t, rows)
    assert tt % attn_tile == 0
    per = tt // attn_tile
    d_ext = d + A_HEADS * ONES_ROWS
    tok = lambda i, j: (i, j, 0)
    blk = pl.BlockSpec((bb, tt, d), tok)
    return pl.pallas_call(
        _kv_kernel,
        grid=(b // bb, t // tt),
        in_specs=[blk, _resident((1, d)), _resident((d, d)), _resident((d, d)), _resident((d, d)),
                  _resident((k_g.shape[-1], 1))],
        out_specs=[blk, blk, blk, pl.BlockSpec((bb, per, d_ext, attn_tile), lambda i, j: (i, j, 0, 0))],
        out_shape=[jax.ShapeDtypeStruct((b, t, d), F32), jax.ShapeDtypeStruct((b, t, d), F32),
                   jax.ShapeDtypeStruct((b, t, d), BF16),
                   jax.ShapeDtypeStruct((b, t // attn_tile, d_ext, attn_tile), BF16)],
        compiler_params=_params("parallel", "parallel"),
        name="shared_kv",
    )(x, kv_g.reshape(1, d), w_k.T, w_v, w_v.T, k_g.reshape(-1, 1))


def _mixer_ffn_kernel(x_ref, a_ref, wp_ref, gate1_ref, g_ref, sh_ref, sc_ref, gate2_ref, wg_ref, wu_ref, wd_ref,
                      o_ref, acc_ref):
    bb, tt, d = x_ref.shape
    f = wg_ref.shape[1]
    a = a_ref[...].reshape(bb * tt, a_ref.shape[-1])
    o_ref[...] = x_ref[...] + gate1_ref[...] * _dot(a, wp_ref[...]).reshape(bb, tt, d)
    xm = _modulated(o_ref, g_ref, sh_ref, sc_ref)
    for c in range(f // FFN_CHUNK):
        sl = slice(c * FFN_CHUNK, (c + 1) * FFN_CHUNK)
        hg = _dot(xm, wg_ref[:, sl])
        hu = _dot(xm, wu_ref[:, sl])
        act = (hg * jax.nn.sigmoid(hg) * hu).astype(BF16)
        part = _dot(act, wd_ref[sl, :])
        if c == 0:
            acc_ref[...] = part
        else:
            acc_ref[...] += part
    o_ref[...] += gate2_ref[...] * acc_ref[...].reshape(bb, tt, d)


def _mixer_ffn(x, a, w_proj, gate1, gain, shift, scale, gate2, w_gate, w_up, w_down, *, rows):
    b, t, d = x.shape
    k = a.shape[-1]
    f = w_gate.shape[1]
    assert f % FFN_CHUNK == 0
    bb, tt = _row_blocks(b, t, rows)
    tok = lambda i, j: (i, j, 0)
    per_b = pl.BlockSpec((bb, 1, d), lambda i, j: (i, 0, 0))
    return pl.pallas_call(
        _mixer_ffn_kernel,
        grid=(b // bb, t // tt),
        in_specs=[pl.BlockSpec((bb, tt, d), tok), pl.BlockSpec((bb, tt, k), tok), _resident((k, d)), per_b,
                  _resident((1, d)), per_b, per_b, per_b,
                  _resident((d, f)), _resident((d, f)), _resident((f, d))],
        out_specs=pl.BlockSpec((bb, tt, d), tok),
        out_shape=jax.ShapeDtypeStruct((b, t, d), F32),
        scratch_shapes=[pltpu.VMEM((bb * tt, d), F32)],
        compiler_params=_params("parallel", "parallel"),
        name="mixer_ffn",
    )(x, a, w_proj, gate1, gain.reshape(1, d), shift, scale, gate2, w_gate, w_up, w_down)


def _mlstm_kernel(q_ref, k_ref, o_ref, vt_ref, g_ref, gt_ref, brow_ref, bcol_ref, hg_ref,
                  c0_ref, n0_ref, m0_ref, h_ref, c_out, n_out, m_out,
                  ct_s, m_s, mnew_s, acol_s, br_s, inter_s, wn_s, wold_s, qk_s, qc_s, sb_s, sv_s, stat_s):
    ci = pl.program_id(1)
    length = q_ref.shape[1]
    ext = ct_s.shape[1]
    dv = ext - ONES_ROWS
    dk = q_ref.shape[2] // M_HEADS
    heads = range(M_HEADS)

    @pl.when(ci == 0)
    def _():
        ct_s[...] = jnp.zeros(ct_s.shape, F32)
        for h in heads:
            c0 = jnp.concatenate([c0_ref[0, h], jnp.zeros((dv - dk, dv), F32)], axis=0)
            ct_s[h, :dv, :] = c0.T
            ct_s[h, dv:, :dk] = jnp.broadcast_to(n0_ref[0, h:h + 1, :], (ONES_ROWS, dk))
        m_s[...] = m0_ref[0]

    row = lax.broadcasted_iota(jnp.int32, (length, length), 0)
    col = lax.broadcasted_iota(jnp.int32, (length, length), 1)
    causal = row <= col
    hi = lax.Precision.HIGHEST
    g = g_ref[0] + brow_ref[...]
    b_cols = jnp.dot((col <= row).astype(F32), _log_sigmoid(g[:, GATE_PAD:]), precision=hi,
                     preferred_element_type=F32)
    acol_s[...] = g[:, :GATE_PAD] - b_cols
    gt = gt_ref[0] + bcol_ref[...]
    ig_rows = gt[:M_HEADS]
    br_s[...] = jnp.dot(_log_sigmoid(gt[M_HEADS:]), causal.astype(F32), precision=hi,
                        preferred_element_type=F32)
    b_rows = br_s[...]
    b_end = br_s[:, length - 1:length]
    m_prev = m_s[...]
    g_rows = b_end - b_rows + ig_rows
    m_new = jnp.maximum(b_end + m_prev, jnp.max(g_rows, axis=-1, keepdims=True))
    mnew_s[...] = m_new
    wold_s[...] = jnp.exp(b_end + m_prev - m_new)
    wn_s[...] = jnp.exp(g_rows - m_new)
    inter_s[...] = b_rows + m_prev

    def q_of(h):
        return q_ref[0, :, h * dk:(h + 1) * dk]

    def k_of(h):
        return k_ref[0, :, h * dk:(h + 1) * dk] * (dk ** -0.5)

    def vt_of(h):
        return vt_ref[0, h * ext:(h + 1) * ext, :]

    for h in heads:
        qk_s[h] = _dot_nt(k_of(h), q_of(h))
        qc_s[h] = _dot_nt(ct_s[h, :, :dk].astype(BF16), q_of(h))
    for h in heads:
        inter = inter_s[h:h + 1, :]
        dmat = jnp.where(causal, acol_s[:, h:h + 1] + br_s[h:h + 1, :], -jnp.inf)
        m_t = jnp.maximum(inter, jnp.max(dmat, axis=0, keepdims=True))
        sb_s[h] = (qk_s[h] * jnp.exp(dmat - m_t)).astype(BF16)
        stat_s[0, h] = jnp.exp(inter - m_t)
        stat_s[1, h] = jnp.exp(-m_t)
    for h in heads:
        sv_s[h] = _dot(vt_of(h), sb_s[h])
    for h in heads:
        w_inter = stat_s[0, h]
        tot = w_inter * qc_s[h] + sv_s[h]
        hh = tot[:dv] / jnp.maximum(jnp.abs(tot[dv:dv + 1]), stat_s[1, h])
        hh = hh * lax.rsqrt(jnp.mean(hh * hh, axis=0, keepdims=True) + EPS)
        og = o_ref[0, :, h * dv:(h + 1) * dv].astype(F32)
        h_ref[0, :, h * dv:(h + 1) * dv] = (_to_token_major(hh) * hg_ref[:, h * dv:(h + 1) * dv]
                                            * jax.nn.sigmoid(og)).astype(h_ref.dtype)
    for h in heads:
        vtw = (vt_of(h).astype(F32) * wn_s[h:h + 1, :]).astype(BF16)
        ct_s[h, :, :dk] = wold_s[h:h + 1, :] * ct_s[h, :, :dk] + _dot(vtw, k_of(h))
    m_s[...] = mnew_s[...]

    @pl.when(ci == pl.num_programs(1) - 1)
    def _():
        for h in heads:
            c_out[0, h] = ct_s[h, :dv, :].T[:dk]
            n_out[0, h:h + 1, :] = ct_s[h, dv:dv + 1, :dk]
        m_out[0] = m_s[...]


def _to_token_major(x):
    dv, length = x.shape
    if length % LANES:
        pad = LANES - length % LANES
        x = jnp.concatenate([x, jnp.zeros((dv, pad), x.dtype)], axis=1)
    return x.T[:length]


def _mlstm(qko, vt, g, gt, b_gates, head_g, c0, n0, m0, *, chunk):
    b, t, _ = qko.shape
    heads, dk, dv = c0.shape[1:]
    d = heads * dv
    qk = heads * dk
    ext = dv + ONES_ROWS
    assert qk * 2 == d and t % chunk == 0 and heads == M_HEADS and vt.shape[1] == heads * ext
    nc = t // chunk
    two_h = 2 * heads
    brow = (jnp.zeros((1, 2 * GATE_PAD), F32).at[0, :heads].set(b_gates[:heads])
            .at[0, GATE_PAD:GATE_PAD + heads].set(b_gates[heads:]))
    st4 = lambda i, c: (i, 0, 0, 0)
    st3 = lambda i, c: (i, 0, 0)
    outs = pl.pallas_call(
        _mlstm_kernel,
        grid=(b, nc),
        in_specs=[
            pl.BlockSpec((1, chunk, qk), lambda i, c: (i, c, 0)),
            pl.BlockSpec((1, chunk, qk), lambda i, c: (i, c, 1)),
            pl.BlockSpec((1, chunk, d), lambda i, c: (i, c, 1)),
            pl.BlockSpec((1, heads * ext, chunk), lambda i, c: (i, 0, c)),
            pl.BlockSpec((1, chunk, 2 * GATE_PAD), lambda i, c: (i, c, 0)),
            pl.BlockSpec((1, two_h, chunk), lambda i, c: (i, 0, c)),
            _resident((1, 2 * GATE_PAD)), _resident((two_h, 1)), _resident((1, d)),
            pl.BlockSpec((1, heads, dk, dv), st4),
            pl.BlockSpec((1, heads, dk), st3),
            pl.BlockSpec((1, heads, 1), st3),
        ],
        out_specs=[
            pl.BlockSpec((1, chunk, d), lambda i, c: (i, c, 0)),
            pl.BlockSpec((1, heads, dk, dv), st4),
            pl.BlockSpec((1, heads, dk), st3),
            pl.BlockSpec((1, heads, 1), st3),
        ],
        out_shape=[
            jax.ShapeDtypeStruct((b, t, d), BF16),
            jax.ShapeDtypeStruct((b, heads, dk, dv), F32),
            jax.ShapeDtypeStruct((b, heads, dk), F32),
            jax.ShapeDtypeStruct((b, heads, 1), F32),
        ],
        scratch_shapes=[pltpu.VMEM((heads, ext, dv), F32),
                        pltpu.VMEM((heads, 1), F32), pltpu.VMEM((heads, 1), F32),
                        pltpu.VMEM((chunk, GATE_PAD), F32),
                        pltpu.VMEM((heads, chunk), F32), pltpu.VMEM((heads, chunk), F32),
                        pltpu.VMEM((heads, chunk), F32), pltpu.VMEM((heads, 1), F32),
                        pltpu.VMEM((heads, chunk, chunk), F32),
                        pltpu.VMEM((heads, ext, chunk), F32),
                        pltpu.VMEM((heads, chunk, chunk), BF16),
                        pltpu.VMEM((heads, ext, chunk), F32),
                        pltpu.VMEM((2, heads, 1, chunk), F32)],
        compiler_params=_params("parallel", "arbitrary"),
        name="mlstm",
    )(qko, qko, qko, vt, g, gt, brow, b_gates.reshape(two_h, 1), head_g.reshape(1, d),
      c0, n0, m0.reshape(b, heads, 1))
    hn, c_new, n_new, m_new = outs
    return hn, c_new, n_new, m_new.reshape(b, heads)


def _rel_bucket(rel):
    nb = N_BUCKETS // 2
    max_exact = nb // 2
    ret = jnp.where(rel > 0, nb, 0)
    n = jnp.abs(rel)
    large = max_exact + (jnp.log(jnp.maximum(n, 1).astype(F32) / max_exact)
                         / math.log(MAX_DISTANCE / max_exact) * (nb - max_exact)).astype(jnp.int32)
    large = jnp.minimum(large, nb - 1)
    return ret + jnp.where(n < max_exact, n, large)


def _bias_tile(rel_bias, q0, k0, tq, tk):
    table = rel_bias.astype(F32)
    span = tq + tk
    r = jnp.arange(span)
    r = jnp.where(r < tk, r, r - span)
    vec = (table[_rel_bucket(r + (k0 - q0))] - table[N_BUCKETS // 2 - 1]).T
    bias = jnp.tile(vec, (1, tq))[:, :tq * (span - 1)].reshape(-1, tq, span - 1)[:, :, :tk]
    q_chunk = (q0 + lax.broadcasted_iota(jnp.int32, (tq, tk), 0)) // CHUNK
    k_chunk = (k0 + lax.broadcasted_iota(jnp.int32, (tq, tk), 1)) // CHUNK
    return jnp.where((k_chunk <= q_chunk)[None], bias * LOG2E, NEG_BIG)


def _attn_prompt_kernel(q_ref, k_ref, vt_ref, bias_ref, lam_ref, hg_ref, o_ref,
                        m_s, alpha_s, acc_s, s_scr, p_scr, *, lam_init):
    qi = pl.program_id(1)
    tile = q_ref.shape[2]
    ext = acc_s.shape[1]
    dv = o_ref.shape[2] // A_HEADS
    dh = dv // 2
    n_chain = 2 * A_HEADS
    m_s[...] = jnp.full(m_s.shape, NEG_BIG, F32)
    acc_s[...] = jnp.zeros(acc_s.shape, F32)

    def kv_tiles(j, slots, count):
        rows = count * tile
        koff = pl.multiple_of(j * tile, tile)
        for hc in range(n_chain):
            lo = hc * dh
            s_scr[hc, :rows] = _dot(k_ref[0, pl.ds(koff, rows), lo:lo + dh], q_ref[0, lo:lo + dh, :])
        for hc in range(n_chain):
            m_prev = m_s[hc]
            m_part = None
            for r in range(0, rows, ROW_CHUNK):
                s = s_scr[hc, r:r + ROW_CHUNK]
                if slots is not None:
                    t, rt = divmod(r, tile)
                    s = s + bias_ref[hc // 2, slots[t], rt:rt + ROW_CHUNK]
                    s_scr[hc, r:r + ROW_CHUNK] = s
                part = jnp.max(s.reshape(ROW_CHUNK // SUBLANES, SUBLANES, tile), axis=0)
                m_part = part if m_part is None else jnp.maximum(m_part, part)
            m_new = jnp.maximum(m_prev, jnp.max(m_part, axis=0, keepdims=True))
            alpha_s[hc] = jnp.exp2(m_prev - m_new)
            m_s[hc] = m_new
        for hc in range(n_chain):
            for r in range(0, rows, ROW_CHUNK):
                p_scr[hc, r:r + ROW_CHUNK] = jnp.exp2(s_scr[hc, r:r + ROW_CHUNK] - m_s[hc]).astype(BF16)
        for hc in range(n_chain):
            h = hc // 2
            pv = _dot(vt_ref[0, j, h * ext:(h + 1) * ext, :], p_scr[hc, :tile])
            for t in range(1, count):
                pv = pv + _dot(vt_ref[0, j + t, h * ext:(h + 1) * ext, :], p_scr[hc, t * tile:(t + 1) * tile])
            acc_s[hc] = alpha_s[hc] * acc_s[hc] + pv

    n_far = jnp.maximum(qi - 1, 0)

    def far_body(i, carry):
        kv_tiles(FAR_TILES * i, None, FAR_TILES)
        return carry

    lax.fori_loop(0, n_far // FAR_TILES, far_body, 0)

    def far_rest(j, carry):
        kv_tiles(j, None, 1)
        return carry

    lax.fori_loop(n_far - n_far % FAR_TILES, n_far, far_rest, 0)

    @pl.when(qi > 0)
    def _():
        kv_tiles(qi - 1, (0, 1), 2)

    @pl.when(qi == 0)
    def _():
        kv_tiles(qi, (1,), 1)


    lp = lam_ref[...]
    lam = (jnp.exp(jnp.sum(lp[0:1] * lp[1:2], axis=-1, keepdims=True))
           - jnp.exp(jnp.sum(lp[2:3] * lp[3:4], axis=-1, keepdims=True)) + lam_init)
    for h in range(A_HEADS):
        inv0 = 1.0 / acc_s[2 * h, dv:dv + 1]
        inv1 = 1.0 / acc_s[2 * h + 1, dv:dv + 1]
        o = acc_s[2 * h, :dv] * inv0 - (lam * inv1) * acc_s[2 * h + 1, :dv]
        o = o * lax.rsqrt(jnp.mean(o * o, axis=0, keepdims=True) + EPS) * hg_ref[...] * (1.0 - lam_init)
        o_ref[0, :, h * dv:(h + 1) * dv] = o.T.astype(o_ref.dtype)


def _attn_prompt(q, k, vt, rel_bias, lam_p, head_g, lam_init):
    b, d, t = q.shape
    tile = ATTN_TILE
    assert t % tile == 0 and tile % CHUNK == 0 and tile >= MAX_DISTANCE and vt.shape[-1] == tile
    n = t // tile
    dv = d // A_HEADS
    ext = vt.shape[2] // A_HEADS
    n_chain = 2 * A_HEADS
    bias = jnp.stack([_bias_tile(rel_bias, tile, 0, tile, tile), _bias_tile(rel_bias, 0, 0, tile, tile)],
                     axis=1).swapaxes(-1, -2)
    return pl.pallas_call(
        functools.partial(_attn_prompt_kernel, lam_init=lam_init),
        grid=(b, n),
        in_specs=[pl.BlockSpec((1, d, tile), lambda i, qi: (i, 0, qi)),
                  pl.BlockSpec((1, t, d), lambda i, qi: (i, 0, 0), pipeline_mode=pl.Buffered(1)),
                  pl.BlockSpec((1,) + vt.shape[1:], lambda i, qi: (i, 0, 0, 0), pipeline_mode=pl.Buffered(1)),
                  _resident(bias.shape), _resident(lam_p.shape), _resident((dv, 1))],
        out_specs=pl.BlockSpec((1, tile, d), lambda i, qi: (i, qi, 0)),
        out_shape=jax.ShapeDtypeStruct((b, t, d), BF16),
        scratch_shapes=[pltpu.VMEM((n_chain, 1, tile), F32), pltpu.VMEM((n_chain, 1, tile), F32),
                        pltpu.VMEM((n_chain, ext, tile), F32),
                        pltpu.VMEM((n_chain, FAR_TILES * tile, tile), F32),
                        pltpu.VMEM((n_chain, FAR_TILES * tile, tile), BF16)],
        compiler_params=_params("parallel", "arbitrary"),
        name="attn_prompt",
    )(q, k, vt, bias, lam_p, head_g.reshape(dv, 1))


def _attn_sample_kernel(q_ref, kc_ref, vtc_ref, kn_ref, vtn_ref, bias_c_ref, bias_n_ref, lam_ref, hg_ref,
                        o_ref, qbd_s, m_s, alpha_s, acc_s, s_scr, p_scr, *, lam_init):
    tq = q_ref.shape[2]
    tile = vtc_ref.shape[3]
    n_cache = vtc_ref.shape[1]
    ext = acc_s.shape[1]
    dv = o_ref.shape[2] // A_HEADS
    gw = qbd_s.shape[1]
    n_group = qbd_s.shape[0]
    n_sub = gw // tq
    row_blk = lax.broadcasted_iota(jnp.int32, (gw, gw), 0) // (gw // n_sub)
    col_blk = lax.broadcasted_iota(jnp.int32, (gw, gw), 1) // tq
    for g in range(n_group):
        qg = q_ref[0, g * gw:(g + 1) * gw, :]
        qrep = jnp.concatenate([qg] * n_sub, axis=1)
        qbd_s[g] = jnp.where(row_blk == col_blk, qrep, jnp.zeros_like(qrep))
    m_s[...] = jnp.full(m_s.shape, NEG_BIG, F32)
    acc_s[...] = jnp.zeros(acc_s.shape, F32)

    def kv_tile(k_tile, vt_tile, bias_ref, rows):
        for g in range(n_group):
            s_scr[g, :rows] = _dot(k_tile(g), qbd_s[g])
        for g in range(n_group):
            s = s_scr[g, :rows]
            if bias_ref is not None:
                s = s + bias_ref[g]
                s_scr[g, :rows] = s
            m_prev = m_s[g]
            m_new = jnp.maximum(m_prev, jnp.max(s, axis=0, keepdims=True))
            alpha_s[g] = jnp.exp2(m_prev - m_new)
            m_s[g] = m_new
        step = min(ROW_CHUNK, rows)
        for g in range(n_group):
            for r in range(0, rows, step):
                p_scr[g, r:r + step] = jnp.exp2(s_scr[g, r:r + step] - m_s[g]).astype(BF16)
        for h in range(A_HEADS):
            g, half = divmod(h, 2)
            lanes = slice(half * 2 * tq, (half + 1) * 2 * tq)
            acc_s[h] = alpha_s[g, :, lanes] * acc_s[h] + _dot(vt_tile(h), p_scr[g, :rows, lanes])

    ones = jnp.ones((ext - dv, tile), BF16)

    def cache_tile(j, bias_ref):
        koff = pl.multiple_of(j * tile, tile)
        kv_tile(lambda g: kc_ref[0, pl.ds(koff, tile), g * gw:(g + 1) * gw],
                lambda h: jnp.concatenate([vtc_ref[0, j, h * dv:(h + 1) * dv, :], ones], axis=0),
                bias_ref, tile)

    def far_body(j, carry):
        cache_tile(j, None)
        return carry

    lax.fori_loop(0, n_cache - 1, far_body, 0)
    cache_tile(n_cache - 1, bias_c_ref)
    kv_tile(lambda g: kn_ref[0, :, g * gw:(g + 1) * gw],
            lambda h: vtn_ref[0, 0, h * ext:(h + 1) * ext, :], bias_n_ref, tq)

    lp = lam_ref[...]
    lam = (jnp.exp(jnp.sum(lp[0:1] * lp[1:2], axis=-1, keepdims=True))
           - jnp.exp(jnp.sum(lp[2:3] * lp[3:4], axis=-1, keepdims=True)) + lam_init)
    for h in range(A_HEADS):
        both = (acc_s[h, :dv] * (1.0 / acc_s[h, dv:dv + 1])).T
        o = both[:tq] - lam * both[tq:]
        o = _rms(o, hg_ref[...]) * (1.0 - lam_init)
        o_ref[0, :, h * dv:(h + 1) * dv] = o.astype(o_ref.dtype)


def _attn_sample(q, cache_k, cache_vt, new_k, new_vt, rel_bias, lam_p, head_g, lam_init):
    b, d, t = q.shape
    past = cache_k.shape[1]
    n_cache, _, tile = cache_vt.shape[1:]
    d_ext = new_vt.shape[2]
    assert past == n_cache * tile and past % CHUNK == 0 and tile >= MAX_DISTANCE
    dv = d // A_HEADS
    ext = d_ext // A_HEADS
    gw = 4 * t
    assert gw == 2 * dv and A_HEADS % 2 == 0
    n_group = A_HEADS // 2

    def packed(bias):
        bt = bias.swapaxes(-1, -2).reshape(n_group, 2, -1, t)
        return jnp.concatenate([bt[:, 0], bt[:, 0], bt[:, 1], bt[:, 1]], axis=-1)

    bias_c = packed(_bias_tile(rel_bias, past, past - tile, t, tile))
    bias_n = packed(_bias_tile(rel_bias, past, past, t, t))
    tok_spec = pl.BlockSpec((1, t, d), lambda i: (i, 0, 0))
    return pl.pallas_call(
        functools.partial(_attn_sample_kernel, lam_init=lam_init),
        grid=(b,),
        in_specs=[pl.BlockSpec((1, d, t), lambda i: (i, 0, 0)),
                  pl.BlockSpec((1, past, d), lambda i: (i, 0, 0)),
                  pl.BlockSpec((1, n_cache, d, tile), lambda i: (i, 0, 0, 0)),
                  tok_spec,
                  pl.BlockSpec((1, 1, d_ext, t), lambda i: (i, 0, 0, 0)),
                  _resident(bias_c.shape), _resident(bias_n.shape), _resident(lam_p.shape),
                  _resident((1, dv))],
        out_specs=tok_spec,
        out_shape=jax.ShapeDtypeStruct((b, t, d), BF16),
        scratch_shapes=[pltpu.VMEM((n_group, gw, gw), BF16),
                        pltpu.VMEM((n_group, 1, gw), F32), pltpu.VMEM((n_group, 1, gw), F32),
                        pltpu.VMEM((A_HEADS, ext, 2 * t), F32),
                        pltpu.VMEM((n_group, tile, gw), F32), pltpu.VMEM((n_group, tile, gw), BF16)],
        compiler_params=_params("parallel"),
        name="attn_sample",
    )(q, cache_k, cache_vt, new_k, new_vt, bias_c, bias_n, lam_p, head_g.reshape(1, dv))


def _cache_layouts(cache_k, cache_v, tile):
    b, past, heads, two, dh = cache_k.shape
    dv = two * dh
    d = heads * dv
    n = past // tile
    per = math.gcd(n, 4)
    span = per * tile
    per_seq = past // span
    n_slab = d // LANES
    k = pl.pallas_call(
        functools.partial(_keys_to_token_major_kernel, per_seq=per_seq),
        grid=(b * per_seq,),
        in_specs=[pl.BlockSpec(memory_space=pl.ANY)],
        out_specs=pl.BlockSpec((1, span, d), lambda s: (s // per_seq, s % per_seq, 0)),
        out_shape=jax.ShapeDtypeStruct((b, past, d), BF16),
        scratch_shapes=[pltpu.VMEM((KEY_RING, d, span), F32), pltpu.SemaphoreType.DMA((KEY_RING, n_slab))],
        compiler_params=_params("arbitrary"),
        name="cache_keys_t",
    )(cache_k.transpose(0, 2, 3, 4, 1).reshape(b, d, past))
    vt = pl.pallas_call(
        functools.partial(_values_to_feature_major_kernel, heads=heads),
        grid=(b, n // per),
        in_specs=[pl.BlockSpec((1, span * heads, dv), lambda i, j: (i, j, 0))],
        out_specs=pl.BlockSpec((1, per, d, tile), lambda i, j: (i, j, 0, 0)),
        out_shape=jax.ShapeDtypeStruct((b, n, d, tile), BF16),
        compiler_params=_params("parallel", "parallel"),
        name="cache_values_t",
    )(cache_v.reshape(b, past * heads, dv))
    return k, vt


def _keys_to_token_major_kernel(k_hbm, o_ref, buf, sem, *, per_seq):
    s = pl.program_id(0)
    n_steps = pl.num_programs(0)
    span = o_ref.shape[1]
    n_slab = buf.shape[1] // LANES

    def copy(step, c):
        slot = step % KEY_RING
        src = k_hbm.at[step // per_seq, pl.ds(c * LANES, LANES), pl.ds((step % per_seq) * span, span)]
        return pltpu.make_async_copy(src, buf.at[slot, pl.ds(c * LANES, LANES)], sem.at[slot, c])

    def start(step):
        for c in range(n_slab):
            copy(step, c).start()

    @pl.when(s == 0)
    def _():
        for ahead in range(KEY_RING - 1):
            @pl.when(ahead < n_steps)
            def _():
                start(ahead)

    @pl.when(s + KEY_RING - 1 < n_steps)
    def _():
        start(s + KEY_RING - 1)

    slot = s % KEY_RING
    for c in range(n_slab):
        copy(s, c).wait()
        o_ref[0, :, c * LANES:(c + 1) * LANES] = buf[slot, c * LANES:(c + 1) * LANES, :].T.astype(BF16)


def _values_to_feature_major_kernel(v_ref, o_ref, *, heads):
    per, tile = o_ref.shape[1], o_ref.shape[3]
    dv = v_ref.shape[2]
    for j in range(per):
        for h in range(heads):
            rows = v_ref[0, pl.ds(j * tile * heads + h, tile, stride=heads), :]
            o_ref[0, j, h * dv:(h + 1) * dv, :] = rows.T.astype(BF16)


def _trunk(x, mod, mstate, past, chunk, rows, p):
    b, t, d = x.shape
    depth = p["norm_g"].shape[0]
    n_a = p["mlstm_w_in"].shape[0]
    qk = M_HEADS * (d // M_HEADS // 2)
    new_c, new_n, new_m = [], [], []
    prompt = past is None
    k_f32 = v_f32 = k_bf = vt_bf = None
    for l in range(depth):
        sh1, sc1, g1, sh2, sc2, g2 = [mod[l, :, i * d:(i + 1) * d].reshape(b, 1, d) for i in range(6)]
        if l < n_a:
            w_in = p["mlstm_w_in"][l]
            w_qko = jnp.concatenate([w_in[:, :2 * qk], w_in[:, 2 * qk + d:2 * qk + 2 * d]], axis=1)
            qko, vt, g, gt = _mlstm_proj(x, p["norm_g"][l, 0], sh1, sc1, w_qko, w_in[:, 2 * qk:2 * qk + d],
                                         w_in[:, 2 * qk + 2 * d:], rows=rows)
            hn, c_l, n_l, m_l = _mlstm(qko, vt, g, gt, p["mlstm_b_gates"][l], p["mlstm_head_g"][l],
                                       mstate[0][l], mstate[1][l], mstate[2][l], chunk=chunk)
            new_c.append(c_l)
            new_n.append(n_l)
            new_m.append(m_l)
            mixed, w_proj = hn, p["mlstm_w_out"][l]
        else:
            j = l - n_a
            lam_init = 0.8 - 0.6 * math.exp(-0.3 * l)
            dh = p["q_norm_g"].shape[-1]
            q = _q_proj(x, p["norm_g"][l, 0], sh1, sc1, p["attn_w_q"][j], p["q_norm_g"][j],
                        dh ** -0.5 * LOG2E, rows=rows)
            if prompt:
                o = _attn_prompt(q, k_bf, vt_bf, p["rel_bias"], p["attn_lambda"][j], p["attn_head_g"][j],
                                 lam_init)
            else:
                o = _attn_sample(q, past[0], past[1], k_bf, vt_bf, p["rel_bias"], p["attn_lambda"][j],
                                 p["attn_head_g"][j], lam_init)
            mixed, w_proj = o, p["attn_w_o"][j]
        x = _mixer_ffn(x, mixed, w_proj, g1, p["norm_g"][l, 1], sh2, sc2, g2, p["ffn_w_gate"][l],
                       p["ffn_w_up"][l], p["ffn_w_down"][l], rows=rows)
        if l == n_a - 1:
            k_f32, v_f32, k_bf, vt_bf = _shared_kv(x, p["kv_norm_g"], p["w_k"], p["w_v"], p["k_norm_g"],
                                                   rows=rows, attn_tile=ATTN_TILE if prompt else t)
    dh = p["k_norm_g"].shape[-1]
    return (x, jnp.stack(new_c), jnp.stack(new_n), jnp.stack(new_m),
            k_f32.reshape(b, t, A_HEADS, 2, dh), v_f32.reshape(b, t, A_HEADS, 2 * dh))


def kernel(x_prompt, x_sample, c_prompt, c_sample, state_mlstm_C, state_mlstm_n, state_mlstm_m, cache_k, cache_v, ada_w, ada_b, norm_g, mlstm_w_in, mlstm_b_gates, mlstm_head_g, mlstm_w_out, kv_norm_g, w_k, w_v, k_norm_g, attn_w_q, q_norm_g, attn_lambda, attn_head_g, attn_w_o, rel_bias, ffn_w_gate, ffn_w_up, ffn_w_down):
    bf = lambda w: w.astype(BF16)
    p = dict(norm_g=norm_g, mlstm_w_in=bf(mlstm_w_in), mlstm_b_gates=mlstm_b_gates,
             mlstm_head_g=mlstm_head_g, mlstm_w_out=bf(mlstm_w_out), kv_norm_g=kv_norm_g, w_k=bf(w_k),
             w_v=bf(w_v), k_norm_g=k_norm_g, attn_w_q=bf(attn_w_q), q_norm_g=q_norm_g,
             attn_lambda=attn_lambda, attn_head_g=attn_head_g, attn_w_o=bf(attn_w_o), rel_bias=rel_bias,
             ffn_w_gate=bf(ffn_w_gate), ffn_w_up=bf(ffn_w_up), ffn_w_down=bf(ffn_w_down))
    bp = x_prompt.shape[0]
    n_a, _, heads, dk, dv = state_mlstm_C.shape
    mod = _adaln(jnp.concatenate([c_prompt, c_sample], axis=0), ada_w, ada_b)
    zero_state = (jnp.zeros((n_a, bp, heads, dk, dv), F32), jnp.zeros((n_a, bp, heads, dk), F32),
                  jnp.zeros((n_a, bp, heads), F32))
    y_p, c_p, n_p, m_p, k_p, v_p = _trunk(x_prompt, mod[:, :bp], zero_state, None, MLSTM_PROMPT_CHUNK,
                                          TOKEN_ROWS, p)
    y_s, c_s, n_s, m_s, k_s, v_s = _trunk(x_sample, mod[:, bp:], (state_mlstm_C, state_mlstm_n, state_mlstm_m),
                                          _cache_layouts(cache_k, cache_v, CACHE_TILE), x_sample.shape[1],
                                          TOKEN_ROWS, p)
    return (y_p, y_s, c_p, n_p, m_p, k_p, v_p, c_s, n_s, m_s, k_s, v_s)
```
